```python
import jax, jax.numpy as jnp
from jax import lax
import numpy as np

D_MODEL = 1024
BATCH = 8
SEQ = 2048
DEPTH = 2
DEC_BATCH = 128
DEC_SEQ = 1
PAST_LEN = 16384
PAGE_SIZE = 128

N_MIXERS = 2
N_ML = (DEPTH + 1) // 2
N_CV = DEPTH // 2
ML_HEADS = 8
ML_DQK = D_MODEL // (2 * ML_HEADS)
ML_DV = D_MODEL // ML_HEADS
ML_HQ = ML_HEADS * ML_DQK
ML_HV = ML_HEADS * ML_DV
ML_IN = 2 * ML_HQ + 2 * ML_HV + 2 * ML_HEADS
ML_CHUNK = 64
GATE_SOFTCAP = 15.0
CONV_W = 3
D_FF = 2816
N_ADA = 9
EPS = 1e-6

kernel_name = 'hybrid_mlstm_shortconv_macaron_decoder_step'


def rmsnorm(x, g):
    xf = x.astype(jnp.float32)
    r = lax.rsqrt(jnp.mean(xf * xf, axis=-1, keepdims=True) + EPS)
    return (xf * r * g.astype(jnp.float32)).astype(x.dtype)


def swiglu(h, w_gate, w_up, w_down):
    return (jax.nn.silu(h @ w_gate) * (h @ w_up)) @ w_down


def mlstm_chunk_step(carry, xs):
    C, n, m = carry
    q, k, v, ig, lf = xs
    L = q.shape[2]
    b = jnp.cumsum(lf, axis=-1)
    causal = jnp.tril(jnp.ones((L, L), dtype=bool))
    d_log = jnp.where(causal, b[..., :, None] - b[..., None, :] + ig[..., None, :], -jnp.inf)
    st_log = b + m[..., None]
    m_t = jnp.maximum(st_log, jnp.max(d_log, axis=-1))
    w_intra = jnp.exp(d_log - m_t[..., None])
    w_state = jnp.exp(st_log - m_t)
    s = jnp.einsum('bhtd,bhsd->bhts', q, k) * w_intra
    num = w_state[..., None] * jnp.einsum('bhtd,bhde->bhte', q, C) + jnp.einsum('bhts,bhse->bhte', s, v)
    den = w_state * jnp.einsum('bhtd,bhd->bht', q, n) + jnp.sum(s, axis=-1)
    h = num / jnp.maximum(jnp.abs(den), jnp.exp(-m_t))[..., None]
    m_new = m_t[..., -1]
    b_last = b[..., -1]
    w_decay = jnp.exp(b_last + m - m_new)
    w_k = jnp.exp(b_last[..., None] - b + ig - m_new[..., None])
    C_new = w_decay[..., None, None] * C + jnp.einsum('bhs,bhsd,bhse->bhde', w_k, k, v)
    n_new = w_decay[..., None] * n + jnp.einsum('bhs,bhsd->bhd', w_k, k)
    return (C_new, n_new, m_new), h


def mlstm_mixer(h, C0, n0, m0, w_in, b_i, b_f, g_head, w_out):
    B, T, _ = h.shape
    proj = (h @ w_in).astype(jnp.float32)
    q_, k_, v_, o_, i_, f_ = jnp.split(
        proj, [ML_HQ, 2 * ML_HQ, 2 * ML_HQ + ML_HV, 2 * ML_HQ + 2 * ML_HV, 2 * ML_HQ + 2 * ML_HV + ML_HEADS], axis=-1)
    heads = lambda t, d: t.reshape(B, T, ML_HEADS, d).transpose(0, 2, 1, 3)
    q = heads(q_, ML_DQK) * (ML_DQK ** -0.5)
    k = heads(k_, ML_DQK)
    v = heads(v_, ML_DV)
    ig = (GATE_SOFTCAP * jnp.tanh((i_ + b_i.astype(jnp.float32)) / GATE_SOFTCAP)).transpose(0, 2, 1)
    lf = jax.nn.log_sigmoid(f_ + b_f.astype(jnp.float32)).transpose(0, 2, 1)
    L = ML_CHUNK if T % ML_CHUNK == 0 else T
    nc = T // L
    chunks = lambda t: jnp.moveaxis(t.reshape(t.shape[:2] + (nc, L) + t.shape[3:]), 2, 0)
    init = (C0.astype(jnp.float32), n0.astype(jnp.float32), m0.astype(jnp.float32))
    (C, n, m), hs = lax.scan(mlstm_chunk_step, init, (chunks(q), chunks(k), chunks(v), chunks(ig), chunks(lf)))
    hs = hs.transpose(1, 0, 3, 2, 4).reshape(B, T, ML_HEADS, ML_DV)
    hn = hs * lax.rsqrt(jnp.mean(hs * hs, axis=-1, keepdims=True) + EPS) * g_head.astype(jnp.float32)
    y = (jax.nn.sigmoid(o_) * hn.reshape(B, T, ML_HV)).astype(h.dtype) @ w_out
    return y, (C, n, m)


def conv_mixer(h, buf, w_in, conv_w, w_out):
    T = h.shape[1]
    bg, cg, xin = jnp.split(h @ w_in, 3, axis=-1)
    u = cg * xin
    upad = jnp.concatenate([buf.astype(u.dtype), u], axis=1)
    conv = conv_w[0] * upad[:, 0:T]
    for j in range(1, CONV_W):
        conv = conv + conv_w[j] * upad[:, j:j + T]
    y = (bg * conv) @ w_out
    return y, upad[:, T:]


def trunk(x, c, ml_C, ml_n, ml_m, cv_buf, w_ada, b_ada, g_pre, g_post, ffn_wg, ffn_wu, ffn_wd,
          ml_w_in, ml_b_i, ml_b_f, ml_g_head, ml_w_out, cv_w_in, cv_conv_w, cv_w_out):
    B = x.shape[0]
    new_C, new_n, new_m, new_buf = [], [], [], []
    for l in range(DEPTH):
        mod = (jax.nn.silu(c) @ w_ada[l] + b_ada[l]).reshape(B, N_ADA, D_MODEL)[:, :, None, :]
        shift = lambda s: mod[:, 3 * s]
        scale = lambda s: mod[:, 3 * s + 1]
        gate = lambda s: 1.0 + mod[:, 3 * s + 2]
        h = rmsnorm(x, g_pre[l, 0]) * (1.0 + scale(0)) + shift(0)
        x = x + 0.5 * gate(0) * rmsnorm(swiglu(h, ffn_wg[l, 0], ffn_wu[l, 0], ffn_wd[l, 0]), g_post[l, 0])
        h = rmsnorm(x, g_pre[l, 1]) * (1.0 + scale(1)) + shift(1)
        j = l // N_MIXERS
        if l % N_MIXERS == 0:
            out, (Cj, nj, mj) = mlstm_mixer(h, ml_C[j], ml_n[j], ml_m[j], ml_w_in[j], ml_b_i[j], ml_b_f[j],
                                            ml_g_head[j], ml_w_out[j])
            new_C.append(Cj); new_n.append(nj); new_m.append(mj)
        else:
            out, bj = conv_mixer(h, cv_buf[j], cv_w_in[j], cv_conv_w[j], cv_w_out[j])
            new_buf.append(bj)
        x = x + gate(1) * rmsnorm(out, g_post[l, 1])
        h = rmsnorm(x, g_pre[l, 2]) * (1.0 + scale(2)) + shift(2)
        x = x + 0.5 * gate(2) * rmsnorm(swiglu(h, ffn_wg[l, 1], ffn_wu[l, 1], ffn_wd[l, 1]), g_post[l, 2])
    return x, jnp.stack(new_C), jnp.stack(new_n), jnp.stack(new_m), jnp.stack(new_buf)


def setup_inputs(seed: int = 0) -> dict:
    key = jax.random.key(seed)
    ks = jax.random.split(key, 24)
    nrm = lambda k, shape, s=1.0: s * jax.random.normal(k, shape, dtype=jnp.float32)
    D = D_MODEL
    return {
        'x_prompt': nrm(ks[0], (BATCH, SEQ, D)),
        'x_sample': nrm(ks[1], (DEC_BATCH, DEC_SEQ, D)),
        'c_prompt': nrm(ks[2], (BATCH, D)),
        'c_sample': nrm(ks[3], (DEC_BATCH, D)),
        'state_mlstm_C': nrm(ks[4], (N_ML, DEC_BATCH, ML_HEADS, ML_DQK, ML_DV), 0.3),
        'state_mlstm_n': jnp.abs(nrm(ks[5], (N_ML, DEC_BATCH, ML_HEADS, ML_DQK))),
        'state_mlstm_m': nrm(ks[6], (N_ML, DEC_BATCH, ML_HEADS)),
        'state_conv': nrm(ks[7], (N_CV, DEC_BATCH, CONV_W - 1, D)),
        'w_ada': nrm(ks[8], (DEPTH, D, N_ADA * D), 0.5 * D ** -0.5),
        'b_ada': nrm(ks[9], (DEPTH, N_ADA * D), 0.02),
        'g_pre': 1.0 + nrm(ks[10], (DEPTH, 3, D), 0.05),
        'g_post': 1.0 + nrm(ks[11], (DEPTH, 3, D), 0.05),
        'ffn_wg': nrm(ks[12], (DEPTH, 2, D, D_FF), D ** -0.5),
        'ffn_wu': nrm(ks[13], (DEPTH, 2, D, D_FF), D ** -0.5),
        'ffn_wd': nrm(ks[14], (DEPTH, 2, D_FF, D), D_FF ** -0.5),
        'ml_w_in': nrm(ks[15], (N_ML, D, ML_IN), D ** -0.5),
        'ml_b_i': nrm(ks[16], (N_ML, ML_HEADS), 0.1) - 1.0,
        'ml_b_f': 3.0 + nrm(ks[17], (N_ML, ML_HEADS), 0.5),
        'ml_g_head': 1.0 + nrm(ks[18], (N_ML, ML_HEADS, ML_DV), 0.05),
        'ml_w_out': nrm(ks[19], (N_ML, ML_HV, D), ML_HV ** -0.5),
        'cv_w_in': nrm(ks[20], (N_CV, D, 3 * D), D ** -0.5),
        'cv_conv_w': nrm(ks[21], (N_CV, CONV_W, D), CONV_W ** -0.5),
        'cv_w_out': nrm(ks[22], (N_CV, D, D), D ** -0.5),
    }


def reference(x_prompt, x_sample, c_prompt, c_sample, state_mlstm_C, state_mlstm_n, state_mlstm_m, state_conv,
              w_ada, b_ada, g_pre, g_post, ffn_wg, ffn_wu, ffn_wd,
              ml_w_in, ml_b_i, ml_b_f, ml_g_head, ml_w_out, cv_w_in, cv_conv_w, cv_w_out):
    weights = (w_ada, b_ada, g_pre, g_post, ffn_wg, ffn_wu, ffn_wd,
               ml_w_in, ml_b_i, ml_b_f, ml_g_head, ml_w_out, cv_w_in, cv_conv_w, cv_w_out)
    zC = jnp.zeros((N_ML, BATCH, ML_HEADS, ML_DQK, ML_DV), jnp.float32)
    zn = jnp.zeros((N_ML, BATCH, ML_HEADS, ML_DQK), jnp.float32)
    zm = jnp.zeros((N_ML, BATCH, ML_HEADS), jnp.float32)
    zb = jnp.zeros((N_CV, BATCH, CONV_W - 1, D_MODEL), x_prompt.dtype)
    y_prompt, pC, pn, pm, pb = trunk(x_prompt, c_prompt, zC, zn, zm, zb, *weights)
    y_sample, sC, sn, sm, sb = trunk(x_sample, c_sample, state_mlstm_C, state_mlstm_n, state_mlstm_m,
                                     state_conv, *weights)
    return (y_prompt, y_sample, pC, pn, pm, pb, sC, sn, sm, sb)
```

```python
import functools

import jax
import jax.numpy as jnp
from jax import lax
from jax.experimental import pallas as pl
from jax.experimental.pallas import tpu as pltpu

F32 = jnp.float32
BF16 = jnp.bfloat16

D_MODEL = 1024
N_HEADS = 8
D_QK = 64
D_V = 128
D_HQ = N_HEADS * D_QK
D_HV = N_HEADS * D_V
D_FF = 2816
N_ADA = 9
CONV_W = 3
GATE_SOFTCAP = 15.0
EPS = 1e-6

LANES = 128
MXU_COLS = 256
VMEM_LIMIT_BYTES = 56 * 1024 * 1024

FFN_ROWS = 512
FFN_SUB = MXU_COLS
MIX_ROWS = 256
ML_CHUNK = 128
SAMPLE_BLOCK = 8


def _params(*sem):
    return pltpu.CompilerParams(dimension_semantics=sem, vmem_limit_bytes=VMEM_LIMIT_BYTES)


def _sigmoid(x):
    return 1.0 / (1.0 + jnp.exp(-x))


def _rms(x):
    return lax.rsqrt(jnp.mean(x * x, axis=-1, keepdims=True) + EPS)


def _dot(a, b):
    return jnp.dot(a, b, preferred_element_type=F32)


def _dot_nt(a, b):
    return lax.dot_general(a, b, (((1,), (1,)), ((), ())), preferred_element_type=F32)


def _dot_tn(a, b):
    return lax.dot_general(a, b, (((0,), (0,)), ((), ())), preferred_element_type=F32)


def _log_sigmoid(x):
    return -(jnp.maximum(-x, 0.0) + jnp.log1p(jnp.exp(-jnp.abs(x))))


def _ada_kernel(c_ref, w_ref, b_ref, o_ref):
    c = c_ref[...]
    a = (c * _sigmoid(c)).astype(BF16)
    o_ref[...] = _dot(a, w_ref[...].astype(BF16)) + b_ref[...]


def _ada(c_all, w_ada, b_ada):
    depth = w_ada.shape[0]
    rows = c_all.shape[0]
    return pl.pallas_call(
        _ada_kernel,
        grid=(depth, N_ADA),
        in_specs=[
            pl.BlockSpec((rows, D_MODEL), lambda l, j: (0, 0)),
            pl.BlockSpec((None, D_MODEL, D_MODEL), lambda l, j: (l, 0, j)),
            pl.BlockSpec((None, None, 1, D_MODEL), lambda l, j: (l, j, 0, 0)),
        ],
        out_specs=pl.BlockSpec((None, None, rows, D_MODEL), lambda l, j: (l, j, 0, 0)),
        out_shape=jax.ShapeDtypeStruct((depth, N_ADA, rows, D_MODEL), F32),
        compiler_params=_params("arbitrary", "arbitrary"),
        name="ada",
    )(c_all, w_ada, b_ada.reshape(depth, N_ADA, 1, D_MODEL))


def _ffn_kernel(x_ref, mod_ref, gpre_ref, gpost_ref, wg_ref, wu_ref, wd_ref, o_ref, hn_ref, acc_ref, *, sub):
    x = x_ref[...]
    h = x * _rms(x) * gpre_ref[...] * (1.0 + mod_ref[1]) + mod_ref[0]
    hn_ref[...] = h.astype(BF16)
    hn = hn_ref[...]
    d_ff = wg_ref.shape[1]
    for c in range(d_ff // sub):
        cols = slice(c * sub, (c + 1) * sub)
        g = _dot(hn, wg_ref[:, cols])
        u = _dot(hn, wu_ref[:, cols])
        a = (g * _sigmoid(g) * u).astype(BF16)
        part = _dot(a, wd_ref[cols, :])
        if c == 0:
            acc_ref[...] = part
        else:
            acc_ref[...] += part
    y = acc_ref[...]
    o_ref[...] = x + (0.5 * (1.0 + mod_ref[2])) * (y * _rms(y) * gpost_ref[...])


def _ffn(x, mod, g_pre, g_post, wg, wu, wd, *, rows):
    groups, seq, _ = x.shape
    mod_rows = mod.shape[2]
    mrows = 1 if mod_rows == 1 else rows
    mod_map = (lambda g, t: (0, g, 0, 0)) if mod_rows == 1 else (lambda g, t: (0, g, t, 0))
    resident = dict(pipeline_mode=pl.Buffered(1))
    return pl.pallas_call(
        functools.partial(_ffn_kernel, sub=FFN_SUB),
        grid=(groups, seq // rows),
        in_specs=[
            pl.BlockSpec((None, rows, D_MODEL), lambda g, t: (g, t, 0)),
            pl.BlockSpec((3, None, mrows, D_MODEL), mod_map),
            pl.BlockSpec((1, D_MODEL), lambda g, t: (0, 0)),
            pl.BlockSpec((1, D_MODEL), lambda g, t: (0, 0)),
            pl.BlockSpec((D_MODEL, D_FF), lambda g, t: (0, 0), **resident),
            pl.BlockSpec((D_MODEL, D_FF), lambda g, t: (0, 0), **resident),
            pl.BlockSpec((D_FF, D_MODEL), lambda g, t: (0, 0), **resident),
        ],
        out_specs=pl.BlockSpec((None, rows, D_MODEL), lambda g, t: (g, t, 0)),
        out_shape=jax.ShapeDtypeStruct(x.shape, F32),
        scratch_shapes=[pltpu.VMEM((rows, D_MODEL), BF16), pltpu.VMEM((rows, D_MODEL), F32)],
        compiler_params=_params("arbitrary", "arbitrary"),
        name="ffn",
    )(x, mod, g_pre.reshape(1, D_MODEL), g_post.reshape(1, D_MODEL), wg, wu, wd)


def _cumsum_rows(x):
    n = x.shape[0]
    row = lax.broadcasted_iota(jnp.int32, x.shape, 0)
    shift = 1
    while shift < n:
        x = x + jnp.where(row >= shift, pltpu.roll(x, shift, 0), 0.0)
        shift *= 2
    return x


def _mlstm_kernel(x_ref, mod_ref, gpre_ref, gpost_ref, wqkvo_ref, wif_ref, bif_ref, ghead_ref, wout_ref,
                  o_ref, c_out_ref, n_out_ref, m_out_ref,
                  c_s, n_s, m_s, y_s, *, chunk):
    t = pl.program_id(1)

    @pl.when(t == 0)
    def _():
        c_s[...] = jnp.zeros_like(c_s)
        n_s[...] = jnp.zeros_like(n_s)
        m_s[...] = jnp.zeros_like(m_s)

    x = x_ref[...]
    rows = x.shape[0]
    h_in = (x * _rms(x) * gpre_ref[...] * (1.0 + mod_ref[1]) + mod_ref[0]).astype(BF16)
    proj = _dot(h_in, wqkvo_ref[...])
    gates = _dot(h_in, wif_ref[...])
    ig_all = GATE_SOFTCAP * jnp.tanh((gates[:, :LANES] + bif_ref[0:1, :]) / GATE_SOFTCAP)
    lf_all = _log_sigmoid(gates[:, LANES:] + bif_ref[1:2, :])

    tri = (lax.broadcasted_iota(jnp.int32, (chunk, chunk), 0)
           >= lax.broadcasted_iota(jnp.int32, (chunk, chunk), 1))
    lane = lax.broadcasted_iota(jnp.int32, (1, LANES), 1)

    for c in range(rows // chunk):
        rs = slice(c * chunk, (c + 1) * chunk)
        b = _cumsum_rows(lf_all[rs])
        col = ig_all[rs] - b
        row = col.T
        m_prev = m_s[...]
        st = b + m_prev
        m_next = m_prev
        for hd in range(N_HEADS):
            q32 = proj[rs, hd * D_QK:(hd + 1) * D_QK] * (D_QK ** -0.5)
            k32 = proj[rs, D_HQ + hd * D_QK:D_HQ + (hd + 1) * D_QK]
            v = proj[rs, 2 * D_HQ + hd * D_V:2 * D_HQ + (hd + 1) * D_V].astype(BF16)
            og = proj[rs, 2 * D_HQ + D_HV + hd * D_V:2 * D_HQ + D_HV + (hd + 1) * D_V]
            q = q32.astype(BF16)
            k = k32.astype(BF16)
            d_log = jnp.where(tri, b[:, hd:hd + 1] + row[hd:hd + 1, :], -jnp.inf)
            st_h = st[:, hd:hd + 1]
            m_t = jnp.maximum(st_h, jnp.max(d_log, axis=-1, keepdims=True))
            w_intra = jnp.exp(d_log - m_t)
            w_state = jnp.exp(st_h - m_t)
            s = _dot_nt(q, k) * w_intra
            c_h = c_s[hd]
            n_h = n_s[hd]
            num = w_state * _dot(q, c_h.astype(BF16)) + _dot(s.astype(BF16), v)
            den = w_state * jnp.sum(q32 * n_h, axis=-1, keepdims=True) + jnp.sum(s, axis=-1, keepdims=True)
            hh = num / jnp.maximum(jnp.abs(den), jnp.exp(-m_t))
            hn = hh * _rms(hh) * ghead_ref[hd:hd + 1, :]
            y_s[rs, hd * D_V:(hd + 1) * D_V] = (_sigmoid(og) * hn).astype(BF16)
            m_new = m_t[chunk - 1:chunk, :]
            b_last = b[chunk - 1:chunk, hd:hd + 1]
            w_decay = jnp.exp(b_last + m_prev[:, hd:hd + 1] - m_new)
            w_k = jnp.exp(b_last - m_new + col[:, hd:hd + 1])
            kw = k32 * w_k
            c_s[hd] = w_decay * c_h + _dot_tn(kw.astype(BF16), v)
            n_s[hd] = w_decay * n_h + jnp.sum(kw, axis=0, keepdims=True)
            m_next = jnp.where(lane == hd, m_new, m_next)
        m_s[...] = m_next

    y = _dot(y_s[...], wout_ref[...])
    o_ref[...] = x + (1.0 + mod_ref[2]) * (y * _rms(y) * gpost_ref[...])

    @pl.when(t == pl.num_programs(1) - 1)
    def _():
        c_out_ref[...] = c_s[...]
        n_out_ref[...] = n_s[...]
        m_out_ref[...] = m_s[...]


def _mlstm_prompt(x, mod, g_pre, g_post, w_qkvo, w_if, b_if, g_head, w_out):
    batch, seq, _ = x.shape
    rows = MIX_ROWS
    const = lambda b, t: (0, 0)
    return pl.pallas_call(
        functools.partial(_mlstm_kernel, chunk=ML_CHUNK),
        grid=(batch, seq // rows),
        in_specs=[
            pl.BlockSpec((None, rows, D_MODEL), lambda b, t: (b, t, 0)),
            pl.BlockSpec((3, None, 1, D_MODEL), lambda b, t: (0, b, 0, 0)),
            pl.BlockSpec((1, D_MODEL), const),
            pl.BlockSpec((1, D_MODEL), const),
            pl.BlockSpec(w_qkvo.shape, const),
            pl.BlockSpec(w_if.shape, const),
            pl.BlockSpec(b_if.shape, const),
            pl.BlockSpec(g_head.shape, const),
            pl.BlockSpec(w_out.shape, const),
        ],
        out_specs=[
            pl.BlockSpec((None, rows, D_MODEL), lambda b, t: (b, t, 0)),
            pl.BlockSpec((None, N_HEADS, D_QK, D_V), lambda b, t: (b, 0, 0, 0)),
            pl.BlockSpec((None, N_HEADS, 1, D_QK), lambda b, t: (b, 0, 0, 0)),
            pl.BlockSpec((None, 1, LANES), lambda b, t: (b, 0, 0)),
        ],
        out_shape=[
            jax.ShapeDtypeStruct(x.shape, F32),
            jax.ShapeDtypeStruct((batch, N_HEADS, D_QK, D_V), F32),
            jax.ShapeDtypeStruct((batch, N_HEADS, 1, D_QK), F32),
            jax.ShapeDtypeStruct((batch, 1, LANES), F32),
        ],
        scratch_shapes=[
            pltpu.VMEM((N_HEADS, D_QK, D_V), F32),
            pltpu.VMEM((N_HEADS, 1, D_QK), F32),
            pltpu.VMEM((1, LANES), F32),
            pltpu.VMEM((rows, D_HV), BF16),
        ],
        compiler_params=_params("arbitrary", "arbitrary"),
        name="mlstm_prompt",
    )(x, mod, g_pre.reshape(1, D_MODEL), g_post.reshape(1, D_MODEL), w_qkvo, w_if, b_if, g_head, w_out)


def _pair_expand(cols):
    lane = lax.broadcasted_iota(jnp.int32, (cols[0].shape[0], LANES), 1)
    return jnp.concatenate(
        [jnp.where(lane < D_QK, cols[2 * p], cols[2 * p + 1]) for p in range(N_HEADS // 2)], axis=1)


def _mls_proj_kernel(x_ref, mod_ref, gpre_ref, wqkvo_ref, wif_ref, bif_ref, n_ref, m_ref,
                     qs_ref, kw_ref, ws_ref, v_ref, og_ref, a_ref, bc_ref, n_out_ref, m_out_ref):
    x = x_ref[...]
    rows = x.shape[0]
    h_in = (x * _rms(x) * gpre_ref[...] * (1.0 + mod_ref[1]) + mod_ref[0]).astype(BF16)
    proj = _dot(h_in, wqkvo_ref[...])
    gates = _dot(h_in, wif_ref[...])
    ig = GATE_SOFTCAP * jnp.tanh((gates[:, :LANES] + bif_ref[0:1, :]) / GATE_SOFTCAP)
    lf = _log_sigmoid(gates[:, LANES:] + bif_ref[1:2, :])
    m_prev = m_ref[...]
    st = lf + m_prev
    m_t = jnp.maximum(st, ig)
    w_i = jnp.exp(ig - m_t)
    w_s = jnp.exp(st - m_t)
    q = proj[:, :D_HQ] * (D_QK ** -0.5)
    k = proj[:, D_HQ:2 * D_HQ]
    n_prev = n_ref[...]
    qk = q * k
    qn = q * n_prev
    lane = lax.broadcasted_iota(jnp.int32, (rows, LANES), 1)
    lo = lane < D_QK

    def head_sums(z):
        out = []
        for p in range(N_HEADS // 2):
            zp = z[:, p * LANES:(p + 1) * LANES]
            out.append(jnp.sum(jnp.where(lo, zp, 0.0), axis=-1, keepdims=True))
            out.append(jnp.sum(jnp.where(lo, 0.0, zp), axis=-1, keepdims=True))
        return out

    qk_h = head_sums(qk)
    qn_h = head_sums(qn)
    a_all = jnp.zeros((rows, LANES), F32)
    b_all = jnp.zeros((rows, LANES), F32)
    ws_cols, wi_cols = [], []
    for hd in range(N_HEADS):
        ws_h = w_s[:, hd:hd + 1]
        wi_h = w_i[:, hd:hd + 1]
        s_h = qk_h[hd] * wi_h
        den = ws_h * qn_h[hd] + s_h
        inv = 1.0 / jnp.maximum(jnp.abs(den), jnp.exp(-m_t[:, hd:hd + 1]))
        a_all = jnp.where(lane == hd, ws_h * inv, a_all)
        b_all = jnp.where(lane == hd, s_h * inv, b_all)
        ws_cols.append(ws_h)
        wi_cols.append(wi_h)
    ws_x = _pair_expand(ws_cols)
    wi_x = _pair_expand(wi_cols)
    kw = k * wi_x
    qs_ref[...] = q
    kw_ref[...] = kw
    ws_ref[...] = ws_x
    v_ref[...] = proj[:, 2 * D_HQ:2 * D_HQ + D_HV]
    og_ref[...] = proj[:, 2 * D_HQ + D_HV:]
    a_ref[...] = a_all
    bc_ref[...] = b_all
    n_out_ref[...] = ws_x * n_prev + kw
    m_out_ref[...] = m_t


def _mls_state_kernel(c_ref, qt_ref, kwt_ref, wst_ref, v_ref, og_ref, a_ref, bc_ref, ghead_ref,
                      c_out_ref, y_ref):
    for i in range(c_ref.shape[0]):
        nst = []
        v_i = v_ref[i]
        for hd in range(N_HEADS):
            rs = slice(hd * D_QK, (hd + 1) * D_QK)
            c_h = c_ref[i, hd]
            qc = qt_ref[rs, i:i + 1]
            nst.append(jnp.sum(c_h * qc, axis=0, keepdims=True))
            c_out_ref[i, hd] = c_h * wst_ref[rs, i:i + 1] + kwt_ref[rs, i:i + 1] * v_i[hd:hd + 1, :]
        nst = jnp.concatenate(nst, axis=0)
        hh = a_ref[i] * nst + bc_ref[i] * v_i
        hn = hh * _rms(hh) * ghead_ref[...]
        y_ref[i] = _sigmoid(og_ref[i]) * hn


def _mls_out_kernel(x_ref, y_ref, mod_ref, gpost_ref, wout_ref, o_ref):
    y = _dot(y_ref[...].astype(BF16), wout_ref[...])
    o_ref[...] = x_ref[...] + (1.0 + mod_ref[2]) * (y * _rms(y) * gpost_ref[...])


def _mlstm_sample(x, mod, g_pre, g_post, w_qkvo, w_if, b_if, g_head, w_out, c0, n0, m0):
    rows = x.shape[0]
    blk = SAMPLE_BLOCK
    nblk = rows // blk
    m_pad = jnp.pad(m0, ((0, 0), (0, LANES - N_HEADS)))
    full = lambda shape: pl.BlockSpec(shape, lambda: tuple(0 for _ in shape))
    rq = jax.ShapeDtypeStruct((rows, D_HQ), F32)
    rv = jax.ShapeDtypeStruct((rows, D_HV), F32)
    rl = jax.ShapeDtypeStruct((rows, LANES), F32)
    ins = (x, mod, g_pre.reshape(1, D_MODEL), w_qkvo, w_if, b_if, n0.reshape(rows, D_HQ), m_pad)
    qs, kw, ws, v, og, a, bc, n_new, m_new = pl.pallas_call(
        _mls_proj_kernel,
        in_specs=[full(z.shape) for z in ins],
        out_specs=[full(s.shape) for s in (rq, rq, rq, rv, rv, rl, rl, rq, rl)],
        out_shape=[rq, rq, rq, rv, rv, rl, rl, rq, rl],
        compiler_params=_params(),
        name="mlstm_sample_proj",
    )(*ins)

    blocked_t = lambda z: z.reshape(nblk, blk, D_HQ).transpose(0, 2, 1)
    per_head = lambda z: z.reshape(rows, N_HEADS, D_V)
    scal = lambda z: z[:, :N_HEADS].reshape(rows, N_HEADS, 1)
    col_spec = pl.BlockSpec((None, D_HQ, blk), lambda i: (i, 0, 0))
    hv_spec = pl.BlockSpec((blk, N_HEADS, D_V), lambda i: (i, 0, 0))
    sc_spec = pl.BlockSpec((blk, N_HEADS, 1), lambda i: (i, 0, 0))
    c_spec = pl.BlockSpec((blk, N_HEADS, D_QK, D_V), lambda i: (i, 0, 0, 0))
    c_new, y3 = pl.pallas_call(
        _mls_state_kernel,
        grid=(nblk,),
        in_specs=[c_spec, col_spec, col_spec, col_spec, hv_spec, hv_spec, sc_spec, sc_spec,
                  pl.BlockSpec((N_HEADS, D_V), lambda i: (0, 0))],
        out_specs=[c_spec, hv_spec],
        out_shape=[jax.ShapeDtypeStruct(c0.shape, F32), jax.ShapeDtypeStruct((rows, N_HEADS, D_V), F32)],
        compiler_params=_params("arbitrary"),
        name="mlstm_sample_state",
    )(c0, blocked_t(qs), blocked_t(kw), blocked_t(ws), per_head(v), per_head(og), scal(a), scal(bc), g_head)

    outs = (x, y3.reshape(rows, D_HV), mod, g_post.reshape(1, D_MODEL), w_out)
    y = pl.pallas_call(
        _mls_out_kernel,
        in_specs=[full(z.shape) for z in outs],
        out_specs=full(x.shape),
        out_shape=jax.ShapeDtypeStruct(x.shape, F32),
        compiler_params=_params(),
        name="mlstm_sample_out",
    )(*outs)
    return y, c_new, n_new.reshape(rows, N_HEADS, D_QK), m_new[:, :N_HEADS]


def _conv_kernel(x_ref, mod_ref, gpre_ref, gpost_ref, win_ref, cw_ref, wout_ref, o_ref, buf_out_ref, tail_s):
    t = pl.program_id(1)

    @pl.when(t == 0)
    def _():
        tail_s[...] = jnp.zeros_like(tail_s)

    x = x_ref[...]
    rows = x.shape[0]
    h_in = (x * _rms(x) * gpre_ref[...] * (1.0 + mod_ref[1]) + mod_ref[0]).astype(BF16)
    p = _dot(h_in, win_ref[...])
    bg = p[:, :D_MODEL]
    u = p[:, D_MODEL:2 * D_MODEL] * p[:, 2 * D_MODEL:]
    prev1 = tail_s[7:8, :]
    prev2 = tail_s[6:7, :]
    ridx = lax.broadcasted_iota(jnp.int32, u.shape, 0)
    u1 = jnp.where(ridx == 0, prev1, pltpu.roll(u, 1, 0))
    u2 = jnp.where(ridx == 0, prev2, jnp.where(ridx == 1, prev1, pltpu.roll(u, 2, 0)))
    conv = cw_ref[0:1, :] * u2 + cw_ref[1:2, :] * u1 + cw_ref[2:3, :] * u
    tail_s[...] = u[rows - 8:, :]
    y = _dot((bg * conv).astype(BF16), wout_ref[...])
    o_ref[...] = x + (1.0 + mod_ref[2]) * (y * _rms(y) * gpost_ref[...])

    @pl.when(t == pl.num_programs(1) - 1)
    def _():
        buf_out_ref[...] = u[rows - (CONV_W - 1):, :]


def _conv_prompt(x, mod, g_pre, g_post, w_in, conv_w, w_out):
    batch, seq, _ = x.shape
    rows = MIX_ROWS
    const = lambda b, t: (0, 0)
    return pl.pallas_call(
        _conv_kernel,
        grid=(batch, seq // rows),
        in_specs=[
            pl.BlockSpec((None, rows, D_MODEL), lambda b, t: (b, t, 0)),
            pl.BlockSpec((3, None, 1, D_MODEL), lambda b, t: (0, b, 0, 0)),
            pl.BlockSpec((1, D_MODEL), const),
            pl.BlockSpec((1, D_MODEL), const),
            pl.BlockSpec(w_in.shape, const),
            pl.BlockSpec(conv_w.shape, const),
            pl.BlockSpec(w_out.shape, const),
        ],
        out_specs=[
            pl.BlockSpec((None, rows, D_MODEL), lambda b, t: (b, t, 0)),
            pl.BlockSpec((None, CONV_W - 1, D_MODEL), lambda b, t: (b, 0, 0)),
        ],
        out_shape=[jax.ShapeDtypeStruct(x.shape, F32),
                   jax.ShapeDtypeStruct((batch, CONV_W - 1, D_MODEL), F32)],
        scratch_shapes=[pltpu.VMEM((8, D_MODEL), F32)],
        compiler_params=_params("arbitrary", "arbitrary"),
        name="conv_prompt",
    )(x, mod, g_pre.reshape(1, D_MODEL), g_post.reshape(1, D_MODEL), w_in, conv_w, w_out)


def _conv_sample_kernel(x_ref, mod_ref, gpre_ref, gpost_ref, win_ref, cw_ref, wout_ref, buf_ref,
                        o_ref, buf_out_ref):
    x = x_ref[...]
    h_in = (x * _rms(x) * gpre_ref[...] * (1.0 + mod_ref[1]) + mod_ref[0]).astype(BF16)
    p = _dot(h_in, win_ref[...])
    bg = p[:, :D_MODEL]
    u = p[:, D_MODEL:2 * D_MODEL] * p[:, 2 * D_MODEL:]
    conv = cw_ref[0:1, :] * buf_ref[0] + cw_ref[1:2, :] * buf_ref[1] + cw_ref[2:3, :] * u
    y = _dot((bg * conv).astype(BF16), wout_ref[...])
    o_ref[...] = x + (1.0 + mod_ref[2]) * (y * _rms(y) * gpost_ref[...])
    buf_out_ref[0] = buf_ref[1]
    buf_out_ref[1] = u


def _conv_sample(x, mod, g_pre, g_post, w_in, conv_w, w_out, buf):
    full = lambda shape: pl.BlockSpec(shape, lambda: tuple(0 for _ in shape))
    ins = (x, mod, g_pre.reshape(1, D_MODEL), g_post.reshape(1, D_MODEL), w_in, conv_w, w_out, buf)
    return pl.pallas_call(
        _conv_sample_kernel,
        in_specs=[full(z.shape) for z in ins],
        out_specs=[full(x.shape), full(buf.shape)],
        out_shape=[jax.ShapeDtypeStruct(x.shape, F32), jax.ShapeDtypeStruct(buf.shape, F32)],
        compiler_params=_params(),
        name="conv_sample",
    )(*ins)


def kernel(x_prompt, x_sample, c_prompt, c_sample, state_mlstm_C, state_mlstm_n, state_mlstm_m, state_conv,
           w_ada, b_ada, g_pre, g_post, ffn_wg, ffn_wu, ffn_wd,
           ml_w_in, ml_b_i, ml_b_f, ml_g_head, ml_w_out, cv_w_in, cv_conv_w, cv_w_out):
    depth = w_ada.shape[0]
    batch, seq, _ = x_prompt.shape
    n_sample = x_sample.shape[0]
    assert x_sample.shape[1] == 1 and seq % ML_CHUNK == 0

    mod = _ada(jnp.concatenate([c_prompt, c_sample], axis=0), w_ada, b_ada)
    mod_p = mod[:, :, :batch].reshape(depth, N_ADA, batch, 1, D_MODEL)
    mod_s = mod[:, :, batch:]

    wg = ffn_wg.astype(BF16)
    wu = ffn_wu.astype(BF16)
    wd = ffn_wd.astype(BF16)
    n_qkvo = 2 * D_HQ + 2 * D_HV
    ml_qkvo = ml_w_in[:, :, :n_qkvo].astype(BF16)
    pad_gate = lambda w: jnp.pad(w, ((0, 0), (0, 0), (0, LANES - N_HEADS)))
    ml_if = jnp.concatenate([pad_gate(ml_w_in[:, :, n_qkvo:n_qkvo + N_HEADS]),
                             pad_gate(ml_w_in[:, :, n_qkvo + N_HEADS:])], axis=-1).astype(BF16)
    pad_bias = lambda v: jnp.pad(v, ((0, 0), (0, LANES - N_HEADS)))
    ml_bif = jnp.stack([pad_bias(ml_b_i), pad_bias(ml_b_f)], axis=1)
    ml_out = ml_w_out.astype(BF16)
    cv_in = cv_w_in.astype(BF16)
    cv_out = cv_w_out.astype(BF16)

    xp = x_prompt
    xs = x_sample.reshape(1, n_sample, D_MODEL)
    p_c, p_n, p_m, p_buf, s_c, s_n, s_m, s_buf = [], [], [], [], [], [], [], []
    for l in range(depth):
        mp = lambda s: mod_p[l, 3 * s:3 * s + 3]
        ms = lambda s: mod_s[l, 3 * s:3 * s + 3]
        xp = _ffn(xp, mp(0), g_pre[l, 0], g_post[l, 0], wg[l, 0], wu[l, 0], wd[l, 0], rows=FFN_ROWS)
        xs = _ffn(xs, ms(0)[:, None], g_pre[l, 0], g_post[l, 0], wg[l, 0], wu[l, 0], wd[l, 0], rows=n_sample)
        j = l // 2
        if l % 2 == 0:
            xp, c_j, n_j, m_j = _mlstm_prompt(xp, mp(1), g_pre[l, 1], g_post[l, 1], ml_qkvo[j], ml_if[j],
                                              ml_bif[j], ml_g_head[j], ml_out[j])
            p_c.append(c_j)
            p_n.append(n_j.reshape(batch, N_HEADS, D_QK))
            p_m.append(m_j[:, 0, :N_HEADS])
            ys, c_j, n_j, m_j = _mlstm_sample(xs[0], ms(1), g_pre[l, 1], g_post[l, 1], ml_qkvo[j], ml_if[j],
                                              ml_bif[j], ml_g_head[j], ml_out[j],
                                              state_mlstm_C[j], state_mlstm_n[j], state_mlstm_m[j])
            xs = ys[None]
            s_c.append(c_j)
            s_n.append(n_j)
            s_m.append(m_j)
        else:
            xp, buf_j = _conv_prompt(xp, mp(1), g_pre[l, 1], g_post[l, 1], cv_in[j], cv_conv_w[j], cv_out[j])
            p_buf.append(buf_j)
            ys, buf_j = _conv_sample(xs[0], ms(1), g_pre[l, 1], g_post[l, 1], cv_in[j], cv_conv_w[j], cv_out[j],
                                     state_conv[j].transpose(1, 0, 2))
            xs = ys[None]
            s_buf.append(buf_j.transpose(1, 0, 2))
        xp = _ffn(xp, mp(2), g_pre[l, 2], g_post[l, 2], wg[l, 1], wu[l, 1], wd[l, 1], rows=FFN_ROWS)
        xs = _ffn(xs, ms(2)[:, None], g_pre[l, 2], g_post[l, 2], wg[l, 1], wu[l, 1], wd[l, 1], rows=n_sample)

    return (xp, xs.reshape(n_sample, 1, D_MODEL),
            jnp.stack(p_c), jnp.stack(p_n), jnp.stack(p_m), jnp.stack(p_buf),
            jnp.stack(s_c), jnp.stack(s_n), jnp.stack(s_m), jnp.stack(s_buf))
```

```python
import functools

import jax
import jax.numpy as jnp
from jax import lax
from jax.experimental import pallas as pl
from jax.experimental.pallas import tpu as pltpu

F32 = jnp.float32
BF16 = jnp.bfloat16

D_MODEL = 1024
N_HEADS = 8
D_QK = 64
D_V = 128
D_HQ = N_HEADS * D_QK
D_HV = N_HEADS * D_V
D_FF = 2816
N_ADA = 9
CONV_W = 3
GATE_SOFTCAP = 15.0
EPS = 1e-6

LANES = 128
BF16_ROWS = 16
MXU_COLS = 256
VMEM_LIMIT_BYTES = 56 * 1024 * 1024

FFN_ROWS = 512
FFN_SUB = MXU_COLS
MIX_ROWS = 512
ML_CHUNK = LANES
ML_STATE_ROWS = D_V + BF16_ROWS
SAMPLE_BLOCK = 8


def _params(*sem):
    return pltpu.CompilerParams(dimension_semantics=sem, vmem_limit_bytes=VMEM_LIMIT_BYTES)


def _sigmoid(x):
    return 1.0 / (1.0 + jnp.exp(-x))


def _rms(x):
    return lax.rsqrt(jnp.mean(x * x, axis=-1, keepdims=True) + EPS)


def _dot(a, b):
    return jnp.dot(a, b, preferred_element_type=F32)


def _dot_nt(a, b):
    return lax.dot_general(a, b, (((1,), (1,)), ((), ())), preferred_element_type=F32)


def _dot_tn(a, b):
    return lax.dot_general(a, b, (((0,), (0,)), ((), ())), preferred_element_type=F32)


def _log_sigmoid(x):
    return -(jnp.maximum(-x, 0.0) + jnp.log1p(jnp.exp(-jnp.abs(x))))


def _resident(shape):
    return pl.BlockSpec(shape, lambda *_: tuple(0 for _ in shape), pipeline_mode=pl.Buffered(1))


def _ada_kernel(c_ref, w_ref, b_ref, o_ref):
    c = c_ref[...]
    a = (c * _sigmoid(c)).astype(BF16)
    o_ref[...] = _dot(a, w_ref[...].astype(BF16)) + b_ref[...]


def _ada(c_all, w_ada, b_ada):
    depth = w_ada.shape[0]
    rows = c_all.shape[0]
    return pl.pallas_call(
        _ada_kernel,
        grid=(depth, N_ADA),
        in_specs=[
            pl.BlockSpec((rows, D_MODEL), lambda l, j: (0, 0)),
            pl.BlockSpec((None, D_MODEL, D_MODEL), lambda l, j: (l, 0, j)),
            pl.BlockSpec((None, None, 1, D_MODEL), lambda l, j: (l, j, 0, 0)),
        ],
        out_specs=pl.BlockSpec((None, None, rows, D_MODEL), lambda l, j: (l, j, 0, 0)),
        out_shape=jax.ShapeDtypeStruct((depth, N_ADA, rows, D_MODEL), F32),
        compiler_params=_params("arbitrary", "arbitrary"),
        name="ada",
    )(c_all, w_ada, b_ada.reshape(depth, N_ADA, 1, D_MODEL))


def _ffn_kernel(x_ref, mod_ref, gpre_ref, gpost_ref, wg_ref, wu_ref, wd_ref, o_ref, hn_ref, acc_ref, *, sub):
    x = x_ref[...]
    h = x * _rms(x) * gpre_ref[...] * (1.0 + mod_ref[1]) + mod_ref[0]
    hn_ref[...] = h.astype(BF16)
    hn = hn_ref[...]
    d_ff = wg_ref.shape[1]
    for c in range(d_ff // sub):
        cols = slice(c * sub, (c + 1) * sub)
        g = _dot(hn, wg_ref[:, cols])
        u = _dot(hn, wu_ref[:, cols])
        a = (g * _sigmoid(g) * u).astype(BF16)
        part = _dot(a, wd_ref[cols, :])
        if c == 0:
            acc_ref[...] = part
        else:
            acc_ref[...] += part
    y = acc_ref[...]
    o_ref[...] = x + (0.5 * (1.0 + mod_ref[2])) * (y * _rms(y) * gpost_ref[...])


def _ffn(x, mod, g_pre, g_post, wg, wu, wd, *, rows):
    groups, seq, _ = x.shape
    mod_rows = mod.shape[2]
    mrows = 1 if mod_rows == 1 else rows
    mod_map = (lambda g, t: (0, g, 0, 0)) if mod_rows == 1 else (lambda g, t: (0, g, t, 0))
    return pl.pallas_call(
        functools.partial(_ffn_kernel, sub=FFN_SUB),
        grid=(groups, seq // rows),
        in_specs=[
            pl.BlockSpec((None, rows, D_MODEL), lambda g, t: (g, t, 0)),
            pl.BlockSpec((3, None, mrows, D_MODEL), mod_map),
            pl.BlockSpec((1, D_MODEL), lambda g, t: (0, 0)),
            pl.BlockSpec((1, D_MODEL), lambda g, t: (0, 0)),
            _resident((D_MODEL, D_FF)),
            _resident((D_MODEL, D_FF)),
            _resident((D_FF, D_MODEL)),
        ],
        out_specs=pl.BlockSpec((None, rows, D_MODEL), lambda g, t: (g, t, 0)),
        out_shape=jax.ShapeDtypeStruct(x.shape, F32),
        scratch_shapes=[pltpu.VMEM((rows, D_MODEL), BF16), pltpu.VMEM((rows, D_MODEL), F32)],
        compiler_params=_params("arbitrary", "arbitrary"),
        name="ffn",
    )(x, mod, g_pre.reshape(1, D_MODEL), g_post.reshape(1, D_MODEL), wg, wu, wd)


def _block_diag(a, b):
    za = jnp.zeros((a.shape[0], b.shape[1]), a.dtype)
    zb = jnp.zeros((b.shape[0], a.shape[1]), a.dtype)
    return jnp.concatenate([jnp.concatenate([a, za], axis=1), jnp.concatenate([zb, b], axis=1)], axis=0)


def _mlstm_kernel(x_ref, mod_ref, gpre_ref, gpost_ref, wt_ref, wk_ref, bif_ref, ght_ref, wout_ref,
                  o_ref, c_out_ref, n_out_ref, m_out_ref,
                  st_s, m_s, yt_s, *, chunk):
    t = pl.program_id(1)

    @pl.when(t == 0)
    def _():
        st_s[...] = jnp.zeros_like(st_s)
        m_s[...] = jnp.zeros_like(m_s)

    x = x_ref[...]
    rows = x.shape[0]
    h_in = (x * _rms(x) * gpre_ref[...] * (1.0 + mod_ref[1]) + mod_ref[0]).astype(BF16)
    proj_t = _dot_nt(wt_ref[...], h_in)
    k_all = _dot(h_in, wk_ref[...])
    r_q, r_v, r_o = 2 * N_HEADS, 2 * N_HEADS + D_HQ, 2 * N_HEADS + D_HQ + D_HV

    s_idx = lax.broadcasted_iota(jnp.int32, (chunk, chunk), 0)
    t_idx = lax.broadcasted_iota(jnp.int32, (chunk, chunk), 1)
    causal = s_idx <= t_idx
    tri = jnp.where(causal, 1.0, 0.0).astype(BF16)
    lane = lax.broadcasted_iota(jnp.int32, (1, LANES), 1)
    low = lane < D_QK
    ones_rows = jnp.ones((BF16_ROWS, 2 * chunk), BF16)
    zero_rows = jnp.zeros((LANES - N_HEADS, chunk), F32)

    ig_all = GATE_SOFTCAP * jnp.tanh((proj_t[0:N_HEADS, :] + bif_ref[0:N_HEADS, :]) / GATE_SOFTCAP)
    lf_all = _log_sigmoid(proj_t[N_HEADS:2 * N_HEADS, :] + bif_ref[N_HEADS:, :])

    for c in range(rows // chunk):
        cs = slice(c * chunk, (c + 1) * chunk)
        lf = lf_all[:, cs]
        lf_hi = lf.astype(BF16).astype(F32)
        lf_mid = (lf - lf_hi).astype(BF16).astype(F32)
        lf_lo = lf - lf_hi - lf_mid
        b3 = _dot(jnp.concatenate([lf_hi, lf_mid, lf_lo, jnp.zeros_like(lf)], axis=0).astype(BF16), tri)
        b = b3[0:N_HEADS] + b3[N_HEADS:2 * N_HEADS] + b3[2 * N_HEADS:3 * N_HEADS]
        col = ig_all[:, cs] - b
        b_last = jnp.broadcast_to(b[:, chunk - 1:chunk], b.shape)
        col_s = jnp.concatenate([col, zero_rows], axis=0).T

        for p in range(N_HEADS // 2):
            h0, h1 = 2 * p, 2 * p + 1
            qt = [(proj_t[r_q + h * D_QK:r_q + (h + 1) * D_QK, cs] * (D_QK ** -0.5)).astype(BF16) for h in (h0, h1)]
            vt = [proj_t[r_v + h * D_V:r_v + (h + 1) * D_V, cs] for h in (h0, h1)]
            k_pair = k_all[cs, p * LANES:(p + 1) * LANES]
            state = st_s[p]
            lhs1 = jnp.concatenate([k_pair.astype(BF16), state.astype(BF16)], axis=0)
            r1 = _dot(lhs1, _block_diag(qt[0], qt[1]))
            probs, w_state, inv_floor, w_k, w_decay = [], [], [], [], []
            for i, h in enumerate((h0, h1)):
                m_prev = m_s[h:h + 1, :]
                col_m = jnp.where(causal, col_s[:, h:h + 1], -jnp.inf)
                g = jnp.maximum(m_prev, jnp.max(col_m, axis=0, keepdims=True))
                probs.append((jnp.exp(col_m - g) * r1[:chunk, i * chunk:(i + 1) * chunk]).astype(BF16))
                m_t = b[h:h + 1, :] + g
                m_new = jnp.broadcast_to(m_t[:, chunk - 1:chunk], m_t.shape)
                w_state.append(jnp.exp(m_prev - g))
                inv_floor.append(jnp.exp(-m_t))
                w_decay.append(jnp.exp(b_last[h:h + 1, :] + m_prev - m_new))
                w_k.append(jnp.exp(b_last[h:h + 1, :] - m_new + col[h:h + 1, :]))
                m_s[h:h + 1, :] = m_new
            lhs2 = jnp.concatenate([jnp.concatenate([vt[0], vt[1]], axis=1).astype(BF16), ones_rows], axis=0)
            r2 = _dot(lhs2, _block_diag(probs[0], probs[1]))
            for i, h in enumerate((h0, h1)):
                ls = slice(i * chunk, (i + 1) * chunk)
                den = w_state[i] * r1[chunk + D_V:chunk + D_V + 1, ls] + r2[D_V:D_V + 1, ls]
                inv = 1.0 / jnp.maximum(jnp.abs(den), inv_floor[i])
                ht = (w_state[i] * r1[chunk:chunk + D_V, ls] + r2[:D_V, ls]) * inv
                rn = lax.rsqrt(jnp.mean(ht * ht, axis=0, keepdims=True) + EPS)
                og = proj_t[r_o + h * D_V:r_o + (h + 1) * D_V, cs]
                yt_s[h * D_V:(h + 1) * D_V, cs] = (_sigmoid(og) * (ht * rn * ght_ref[h])).astype(BF16)
            lhs3 = jnp.concatenate(
                [jnp.concatenate([vt[i] * w_k[i] for i in range(2)], axis=1),
                 jnp.concatenate([jnp.broadcast_to(w_k[i], (BF16_ROWS, chunk)) for i in range(2)], axis=1)],
                axis=0).astype(BF16)
            rhs3 = jnp.concatenate([jnp.where(low, k_pair, 0.0), jnp.where(low, 0.0, k_pair)], axis=0).astype(BF16)
            decay = jnp.where(low, w_decay[0], w_decay[1])
            st_s[p] = state * decay + _dot(lhs3, rhs3)

    y = _dot_tn(yt_s[...], wout_ref[...])
    o_ref[...] = x + (1.0 + mod_ref[2]) * (y * _rms(y) * gpost_ref[...])

    @pl.when(t == pl.num_programs(1) - 1)
    def _():
        for p in range(N_HEADS // 2):
            state = st_s[p]
            c_out_ref[p * LANES:(p + 1) * LANES, :] = state[:D_V, :].T
            n_out_ref[:, p * LANES:(p + 1) * LANES] = state[D_V:D_V + 1, :]
        m_out_ref[...] = m_s[...]


def _mlstm_prompt(x, mod, g_pre, g_post, w_t, w_k, b_if, g_head_t, w_out):
    batch, seq, _ = x.shape
    rows = MIX_ROWS
    return pl.pallas_call(
        functools.partial(_mlstm_kernel, chunk=ML_CHUNK),
        grid=(batch, seq // rows),
        in_specs=[
            pl.BlockSpec((None, rows, D_MODEL), lambda b, t: (b, t, 0)),
            pl.BlockSpec((3, None, 1, D_MODEL), lambda b, t: (0, b, 0, 0)),
            _resident((1, D_MODEL)),
            _resident((1, D_MODEL)),
            _resident(w_t.shape),
            _resident(w_k.shape),
            _resident(b_if.shape),
            _resident(g_head_t.shape),
            _resident(w_out.shape),
        ],
        out_specs=[
            pl.BlockSpec((None, rows, D_MODEL), lambda b, t: (b, t, 0)),
            pl.BlockSpec((None, D_HQ, D_V), lambda b, t: (b, 0, 0)),
            pl.BlockSpec((None, 1, D_HQ), lambda b, t: (b, 0, 0)),
            pl.BlockSpec((None, N_HEADS, LANES), lambda b, t: (b, 0, 0)),
        ],
        out_shape=[
            jax.ShapeDtypeStruct(x.shape, F32),
            jax.ShapeDtypeStruct((batch, D_HQ, D_V), F32),
            jax.ShapeDtypeStruct((batch, 1, D_HQ), F32),
            jax.ShapeDtypeStruct((batch, N_HEADS, LANES), F32),
        ],
        scratch_shapes=[
            pltpu.VMEM((N_HEADS // 2, ML_STATE_ROWS, LANES), F32),
            pltpu.VMEM((N_HEADS, LANES), F32),
            pltpu.VMEM((D_HV, rows), BF16),
        ],
        compiler_params=_params("arbitrary", "arbitrary"),
        name="mlstm_prompt",
    )(x, mod, g_pre.reshape(1, D_MODEL), g_post.reshape(1, D_MODEL), w_t, w_k, b_if, g_head_t, w_out)


def _pair_expand(cols):
    lane = lax.broadcasted_iota(jnp.int32, (cols[0].shape[0], LANES), 1)
    return jnp.concatenate(
        [jnp.where(lane < D_QK, cols[2 * p], cols[2 * p + 1]) for p in range(N_HEADS // 2)], axis=1)


def _mls_proj_kernel(x_ref, mod_ref, gpre_ref, wqkvo_ref, wif_ref, bif_ref, n_ref, m_ref,
                     qs_ref, kw_ref, ws_ref, v_ref, og_ref, a_ref, bc_ref, n_out_ref, m_out_ref):
    x = x_ref[...]
    rows = x.shape[0]
    h_in = (x * _rms(x) * gpre_ref[...] * (1.0 + mod_ref[1]) + mod_ref[0]).astype(BF16)
    proj = _dot(h_in, wqkvo_ref[...])
    gates = _dot(h_in, wif_ref[...])
    ig = GATE_SOFTCAP * jnp.tanh((gates[:, :LANES] + bif_ref[0:1, :]) / GATE_SOFTCAP)
    lf = _log_sigmoid(gates[:, LANES:] + bif_ref[1:2, :])
    m_prev = m_ref[...]
    st = lf + m_prev
    m_t = jnp.maximum(st, ig)
    w_i = jnp.exp(ig - m_t)
    w_s = jnp.exp(st - m_t)
    q = proj[:, :D_HQ] * (D_QK ** -0.5)
    k = proj[:, D_HQ:2 * D_HQ]
    n_prev = n_ref[...]
    qk = q * k
    qn = q * n_prev
    lane = lax.broadcasted_iota(jnp.int32, (rows, LANES), 1)
    lo = lane < D_QK

    def head_sums(z):
        out = []
        for p in range(N_HEADS // 2):
            zp = z[:, p * LANES:(p + 1) * LANES]
            out.append(jnp.sum(jnp.where(lo, zp, 0.0), axis=-1, keepdims=True))
            out.append(jnp.sum(jnp.where(lo, 0.0, zp), axis=-1, keepdims=True))
        return out

    qk_h = head_sums(qk)
    qn_h = head_sums(qn)
    a_all = jnp.zeros((rows, LANES), F32)
    b_all = jnp.zeros((rows, LANES), F32)
    ws_cols, wi_cols = [], []
    for hd in range(N_HEADS):
        ws_h = w_s[:, hd:hd + 1]
        wi_h = w_i[:, hd:hd + 1]
        s_h = qk_h[hd] * wi_h
        den = ws_h * qn_h[hd] + s_h
        inv = 1.0 / jnp.maximum(jnp.abs(den), jnp.exp(-m_t[:, hd:hd + 1]))
        a_all = jnp.where(lane == hd, ws_h * inv, a_all)
        b_all = jnp.where(lane == hd, s_h * inv, b_all)
        ws_cols.append(ws_h)
        wi_cols.append(wi_h)
    ws_x = _pair_expand(ws_cols)
    wi_x = _pair_expand(wi_cols)
    kw = k * wi_x
    qs_ref[...] = q
    kw_ref[...] = kw
    ws_ref[...] = ws_x
    v_ref[...] = proj[:, 2 * D_HQ:2 * D_HQ + D_HV]
    og_ref[...] = proj[:, 2 * D_HQ + D_HV:]
    a_ref[...] = a_all
    bc_ref[...] = b_all
    n_out_ref[...] = ws_x * n_prev + kw
    m_out_ref[...] = m_t


def _mls_state_kernel(c_ref, qt_ref, kwt_ref, wst_ref, v_ref, og_ref, a_ref, bc_ref, ghead_ref,
                      c_out_ref, y_ref):
    for i in range(c_ref.shape[0]):
        nst = []
        v_i = v_ref[i]
        for hd in range(N_HEADS):
            rs = slice(hd * D_QK, (hd + 1) * D_QK)
            c_h = c_ref[i, hd]
            qc = qt_ref[rs, i:i + 1]
            nst.append(jnp.sum(c_h * qc, axis=0, keepdims=True))
            c_out_ref[i, hd] = c_h * wst_ref[rs, i:i + 1] + kwt_ref[rs, i:i + 1] * v_i[hd:hd + 1, :]
        nst = jnp.concatenate(nst, axis=0)
        hh = a_ref[i] * nst + bc_ref[i] * v_i
        hn = hh * _rms(hh) * ghead_ref[...]
        y_ref[i] = _sigmoid(og_ref[i]) * hn


def _mls_out_kernel(x_ref, y_ref, mod_ref, gpost_ref, wout_ref, o_ref):
    y = _dot(y_ref[...].astype(BF16), wout_ref[...])
    o_ref[...] = x_ref[...] + (1.0 + mod_ref[2]) * (y * _rms(y) * gpost_ref[...])


def _mlstm_sample(x, mod, g_pre, g_post, w_qkvo, w_if, b_if, g_head, w_out, c0, n0, m0):
    rows = x.shape[0]
    blk = SAMPLE_BLOCK
    nblk = rows // blk
    m_pad = jnp.pad(m0, ((0, 0), (0, LANES - N_HEADS)))
    full = lambda shape: pl.BlockSpec(shape, lambda: tuple(0 for _ in shape))
    rq = jax.ShapeDtypeStruct((rows, D_HQ), F32)
    rv = jax.ShapeDtypeStruct((rows, D_HV), F32)
    rl = jax.ShapeDtypeStruct((rows, LANES), F32)
    ins = (x, mod, g_pre.reshape(1, D_MODEL), w_qkvo, w_if, b_if, n0.reshape(rows, D_HQ), m_pad)
    qs, kw, ws, v, og, a, bc, n_new, m_new = pl.pallas_call(
        _mls_proj_kernel,
        in_specs=[full(z.shape) for z in ins],
        out_specs=[full(s.shape) for s in (rq, rq, rq, rv, rv, rl, rl, rq, rl)],
        out_shape=[rq, rq, rq, rv, rv, rl, rl, rq, rl],
        compiler_params=_params(),
        name="mlstm_sample_proj",
    )(*ins)

    blocked_t = lambda z: z.reshape(nblk, blk, D_HQ).transpose(0, 2, 1)
    per_head = lambda z: z.reshape(rows, N_HEADS, D_V)
    scal = lambda z: z[:, :N_HEADS].reshape(rows, N_HEADS, 1)
    col_spec = pl.BlockSpec((None, D_HQ, blk), lambda i: (i, 0, 0))
    hv_spec = pl.BlockSpec((blk, N_HEADS, D_V), lambda i: (i, 0, 0))
    sc_spec = pl.BlockSpec((blk, N_HEADS, 1), lambda i: (i, 0, 0))
    c_spec = pl.BlockSpec((blk, N_HEADS, D_QK, D_V), lambda i: (i, 0, 0, 0))
    c_new, y3 = pl.pallas_call(
        _mls_state_kernel,
        grid=(nblk,),
        in_specs=[c_spec, col_spec, col_spec, col_spec, hv_spec, hv_spec, sc_spec, sc_spec,
                  pl.BlockSpec((N_HEADS, D_V), lambda i: (0, 0))],
        out_specs=[c_spec, hv_spec],
        out_shape=[jax.ShapeDtypeStruct(c0.shape, F32), jax.ShapeDtypeStruct((rows, N_HEADS, D_V), F32)],
        compiler_params=_params("arbitrary"),
        name="mlstm_sample_state",
    )(c0, blocked_t(qs), blocked_t(kw), blocked_t(ws), per_head(v), per_head(og), scal(a), scal(bc), g_head)

    outs = (x, y3.reshape(rows, D_HV), mod, g_post.reshape(1, D_MODEL), w_out)
    y = pl.pallas_call(
        _mls_out_kernel,
        in_specs=[full(z.shape) for z in outs],
        out_specs=full(x.shape),
        out_shape=jax.ShapeDtypeStruct(x.shape, F32),
        compiler_params=_params(),
        name="mlstm_sample_out",
    )(*outs)
    return y, c_new, n_new.reshape(rows, N_HEADS, D_QK), m_new[:, :N_HEADS]


def _conv_kernel(x_ref, mod_ref, gpre_ref, gpost_ref, win_ref, cw_ref, wout_ref, o_ref, buf_out_ref, tail_s):
    t = pl.program_id(1)

    @pl.when(t == 0)
    def _():
        tail_s[...] = jnp.zeros_like(tail_s)

    x = x_ref[...]
    rows = x.shape[0]
    h_in = (x * _rms(x) * gpre_ref[...] * (1.0 + mod_ref[1]) + mod_ref[0]).astype(BF16)
    p = _dot(h_in, win_ref[...])
    bg = p[:, :D_MODEL]
    u = p[:, D_MODEL:2 * D_MODEL] * p[:, 2 * D_MODEL:]
    prev1 = tail_s[7:8, :]
    prev2 = tail_s[6:7, :]
    ridx = lax.broadcasted_iota(jnp.int32, u.shape, 0)
    u1 = jnp.where(ridx == 0, prev1, pltpu.roll(u, 1, 0))
    u2 = jnp.where(ridx == 0, prev2, jnp.where(ridx == 1, prev1, pltpu.roll(u, 2, 0)))
    conv = cw_ref[0:1, :] * u2 + cw_ref[1:2, :] * u1 + cw_ref[2:3, :] * u
    tail_s[...] = u[rows - 8:, :]
    y = _dot((bg * conv).astype(BF16), wout_ref[...])
    o_ref[...] = x + (1.0 + mod_ref[2]) * (y * _rms(y) * gpost_ref[...])

    @pl.when(t == pl.num_programs(1) - 1)
    def _():
        buf_out_ref[...] = u[rows - (CONV_W - 1):, :]


def _conv_prompt(x, mod, g_pre, g_post, w_in, conv_w, w_out):
    batch, seq, _ = x.shape
    rows = MIX_ROWS
    return pl.pallas_call(
        _conv_kernel,
        grid=(batch, seq // rows),
        in_specs=[
            pl.BlockSpec((None, rows, D_MODEL), lambda b, t: (b, t, 0)),
            pl.BlockSpec((3, None, 1, D_MODEL), lambda b, t: (0, b, 0, 0)),
            _resident((1, D_MODEL)),
            _resident((1, D_MODEL)),
            _resident(w_in.shape),
            _resident(conv_w.shape),
            _resident(w_out.shape),
        ],
        out_specs=[
            pl.BlockSpec((None, rows, D_MODEL), lambda b, t: (b, t, 0)),
            pl.BlockSpec((None, CONV_W - 1, D_MODEL), lambda b, t: (b, 0, 0)),
        ],
        out_shape=[jax.ShapeDtypeStruct(x.shape, F32),
                   jax.ShapeDtypeStruct((batch, CONV_W - 1, D_MODEL), F32)],
        scratch_shapes=[pltpu.VMEM((8, D_MODEL), F32)],
        compiler_params=_params("arbitrary", "arbitrary"),
        name="conv_prompt",
    )(x, mod, g_pre.reshape(1, D_MODEL), g_post.reshape(1, D_MODEL), w_in, conv_w, w_out)


def _conv_sample_kernel(x_ref, mod_ref, gpre_ref, gpost_ref, win_ref, cw_ref, wout_ref, buf_ref,
                        o_ref, buf_out_ref):
    x = x_ref[...]
    h_in = (x * _rms(x) * gpre_ref[...] * (1.0 + mod_ref[1]) + mod_ref[0]).astype(BF16)
    p = _dot(h_in, win_ref[...])
    bg = p[:, :D_MODEL]
    u = p[:, D_MODEL:2 * D_MODEL] * p[:, 2 * D_MODEL:]
    conv = cw_ref[0:1, :] * buf_ref[0] + cw_ref[1:2, :] * buf_ref[1] + cw_ref[2:3, :] * u
    y = _dot((bg * conv).astype(BF16), wout_ref[...])
    o_ref[...] = x + (1.0 + mod_ref[2]) * (y * _rms(y) * gpost_ref[...])
    buf_out_ref[0] = buf_ref[1]
    buf_out_ref[1] = u


def _conv_sample(x, mod, g_pre, g_post, w_in, conv_w, w_out, buf):
    full = lambda shape: pl.BlockSpec(shape, lambda: tuple(0 for _ in shape))
    ins = (x, mod, g_pre.reshape(1, D_MODEL), g_post.reshape(1, D_MODEL), w_in, conv_w, w_out, buf)
    return pl.pallas_call(
        _conv_sample_kernel,
        in_specs=[full(z.shape) for z in ins],
        out_specs=[full(x.shape), full(buf.shape)],
        out_shape=[jax.ShapeDtypeStruct(x.shape, F32), jax.ShapeDtypeStruct(buf.shape, F32)],
        compiler_params=_params(),
        name="conv_sample",
    )(*ins)


def kernel(x_prompt, x_sample, c_prompt, c_sample, state_mlstm_C, state_mlstm_n, state_mlstm_m, state_conv,
           w_ada, b_ada, g_pre, g_post, ffn_wg, ffn_wu, ffn_wd,
           ml_w_in, ml_b_i, ml_b_f, ml_g_head, ml_w_out, cv_w_in, cv_conv_w, cv_w_out):
    depth = w_ada.shape[0]
    batch, seq, _ = x_prompt.shape
    n_sample = x_sample.shape[0]
    n_ml = ml_w_in.shape[0]
    assert x_sample.shape[1] == 1 and seq % ML_CHUNK == 0

    mod = _ada(jnp.concatenate([c_prompt, c_sample], axis=0), w_ada, b_ada)
    mod_p = mod[:, :, :batch].reshape(depth, N_ADA, batch, 1, D_MODEL)
    mod_s = mod[:, :, batch:]

    wg = ffn_wg.astype(BF16)
    wu = ffn_wu.astype(BF16)
    wd = ffn_wd.astype(BF16)
    n_qkvo = 2 * D_HQ + 2 * D_HV
    w_q, w_k = ml_w_in[:, :, :D_HQ], ml_w_in[:, :, D_HQ:2 * D_HQ]
    w_vo, w_gates = ml_w_in[:, :, 2 * D_HQ:n_qkvo], ml_w_in[:, :, n_qkvo:]
    ml_t = jnp.concatenate([w_gates, w_q, w_vo], axis=-1).transpose(0, 2, 1).astype(BF16)
    ml_k = w_k.astype(BF16)
    ml_bif_t = jnp.broadcast_to(jnp.concatenate([ml_b_i, ml_b_f], axis=-1)[:, :, None], (n_ml, 2 * N_HEADS, MIX_ROWS))
    ml_ghead_t = jnp.broadcast_to(ml_g_head[:, :, :, None], (n_ml, N_HEADS, D_V, LANES))
    ml_qkvo = ml_w_in[:, :, :n_qkvo].astype(BF16)
    pad_gate = lambda w: jnp.pad(w, ((0, 0), (0, 0), (0, LANES - N_HEADS)))
    ml_if = jnp.concatenate([pad_gate(w_gates[:, :, :N_HEADS]), pad_gate(w_gates[:, :, N_HEADS:])],
                            axis=-1).astype(BF16)
    pad_bias = lambda v: jnp.pad(v, ((0, 0), (0, LANES - N_HEADS)))
    ml_bif = jnp.stack([pad_bias(ml_b_i), pad_bias(ml_b_f)], axis=1)
    ml_out = ml_w_out.astype(BF16)
    cv_in = cv_w_in.astype(BF16)
    cv_out = cv_w_out.astype(BF16)

    xp = x_prompt
    xs = x_sample.reshape(1, n_sample, D_MODEL)
    p_c, p_n, p_m, p_buf, s_c, s_n, s_m, s_buf = [], [], [], [], [], [], [], []
    for l in range(depth):
        mp = lambda s: mod_p[l, 3 * s:3 * s + 3]
        ms = lambda s: mod_s[l, 3 * s:3 * s + 3]
        xp = _ffn(xp, mp(0), g_pre[l, 0], g_post[l, 0], wg[l, 0], wu[l, 0], wd[l, 0], rows=FFN_ROWS)
        xs = _ffn(xs, ms(0)[:, None], g_pre[l, 0], g_post[l, 0], wg[l, 0], wu[l, 0], wd[l, 0], rows=n_sample)
        j = l // 2
        if l % 2 == 0:
            xp, c_j, n_j, m_j = _mlstm_prompt(xp, mp(1), g_pre[l, 1], g_post[l, 1], ml_t[j], ml_k[j],
                                              ml_bif_t[j], ml_ghead_t[j], ml_out[j])
            p_c.append(c_j.reshape(batch, N_HEADS, D_QK, D_V))
            p_n.append(n_j.reshape(batch, N_HEADS, D_QK))
            p_m.append(m_j[:, :, 0])
            ys, c_j, n_j, m_j = _mlstm_sample(xs[0], ms(1), g_pre[l, 1], g_post[l, 1], ml_qkvo[j], ml_if[j],
                                              ml_bif[j], ml_g_head[j], ml_out[j],
                                              state_mlstm_C[j], state_mlstm_n[j], state_mlstm_m[j])
            xs = ys[None]
            s_c.append(c_j)
            s_n.append(n_j)
            s_m.append(m_j)
        else:
            xp, buf_j = _conv_prompt(xp, mp(1), g_pre[l, 1], g_post[l, 1], cv_in[j], cv_conv_w[j], cv_out[j])
            p_buf.append(buf_j)
            ys, buf_j = _conv_sample(xs[0], ms(1), g_pre[l, 1], g_post[l, 1], cv_in[j], cv_conv_w[j], cv_out[j],
                                     state_conv[j].transpose(1, 0, 2))
            xs = ys[None]
            s_buf.append(buf_j.transpose(1, 0, 2))
        xp = _ffn(xp, mp(2), g_pre[l, 2], g_post[l, 2], wg[l, 1], wu[l, 1], wd[l, 1], rows=FFN_ROWS)
        xs = _ffn(xs, ms(2)[:, None], g_pre[l, 2], g_post[l, 2], wg[l, 1], wu[l, 1], wd[l, 1], rows=n_sample)

    return (xp, xs.reshape(n_sample, 1, D_MODEL),
            jnp.stack(p_c), jnp.stack(p_n), jnp.stack(p_m), jnp.stack(p_buf),
            jnp.stack(s_c), jnp.stack(s_n), jnp.stack(s_m), jnp.stack(s_buf))
```

```python
import functools

import jax
import jax.numpy as jnp
from jax import lax
from jax.experimental import pallas as pl
from jax.experimental.pallas import tpu as pltpu

F32 = jnp.float32
BF16 = jnp.bfloat16

D_MODEL = 1024
N_HEADS = 8
D_QK = 64
D_V = 128
D_HQ = N_HEADS * D_QK
D_HV = N_HEADS * D_V
D_FF = 2816
N_ADA = 9
CONV_W = 3
GATE_SOFTCAP = 15.0
EPS = 1e-6

LANES = 128
BF16_ROWS = 16
MXU_COLS = 256
VMEM_LIMIT_BYTES = 56 * 1024 * 1024

FFN_ROWS = 512
FFN_PIECES = 8
FFN_SUB = MXU_COLS
MIX_ROWS = 512
ML_CHUNK = LANES
ML_STATE_ROWS = D_V + BF16_ROWS
SAMPLE_BLOCK = 8


def _params(*sem):
    return pltpu.CompilerParams(dimension_semantics=sem, vmem_limit_bytes=VMEM_LIMIT_BYTES)


def _sigmoid(x):
    return 1.0 / (1.0 + jnp.exp(-x))


def _rms(x):
    return lax.rsqrt(jnp.mean(x * x, axis=-1, keepdims=True) + EPS)


def _dot(a, b):
    return jnp.dot(a, b, preferred_element_type=F32)


def _dot_nt(a, b):
    return lax.dot_general(a, b, (((1,), (1,)), ((), ())), preferred_element_type=F32)


def _dot_tn(a, b):
    return lax.dot_general(a, b, (((0,), (0,)), ((), ())), preferred_element_type=F32)


def _log_sigmoid(x):
    return -(jnp.maximum(-x, 0.0) + jnp.log1p(jnp.exp(-jnp.abs(x))))


def _resident(shape):
    return pl.BlockSpec(shape, lambda *_: tuple(0 for _ in shape), pipeline_mode=pl.Buffered(1))


def _ada_kernel(cp_ref, cs_ref, w_ref, b_ref, op_ref, os_ref):
    w = w_ref[...].astype(BF16)
    for c_ref, o_ref in ((cp_ref, op_ref), (cs_ref, os_ref)):
        c = c_ref[...]
        o_ref[...] = _dot((c * _sigmoid(c)).astype(BF16), w) + b_ref[...]


def _ada(c_prompt, c_sample, w_ada, b_ada):
    depth = w_ada.shape[0]
    out = lambda c: (pl.BlockSpec((None, None, c.shape[0], D_MODEL), lambda l, j: (l, j, 0, 0)),
                     jax.ShapeDtypeStruct((depth, N_ADA, c.shape[0], D_MODEL), F32))
    (spec_p, shape_p), (spec_s, shape_s) = out(c_prompt), out(c_sample)
    return pl.pallas_call(
        _ada_kernel,
        grid=(depth, N_ADA),
        in_specs=[
            pl.BlockSpec(c_prompt.shape, lambda l, j: (0, 0)),
            pl.BlockSpec(c_sample.shape, lambda l, j: (0, 0)),
            pl.BlockSpec((None, D_MODEL, D_MODEL), lambda l, j: (l, 0, j)),
            pl.BlockSpec((None, None, 1, D_MODEL), lambda l, j: (l, j, 0, 0)),
        ],
        out_specs=[spec_p, spec_s],
        out_shape=[shape_p, shape_s],
        compiler_params=_params("arbitrary", "arbitrary"),
        name="ada",
    )(c_prompt, c_sample, w_ada, b_ada.reshape(depth, N_ADA, 1, D_MODEL))


def _group_mod(mod_ref, j):
    return mod_ref[j, pl.ds(pl.program_id(0), 1), :]


def _swiglu_chunk(hn, wg_ref, wu_ref, wd_ref, c, sub):
    cols = slice(c * sub, (c + 1) * sub)
    g = _dot(hn, wg_ref[:, cols])
    u = _dot(hn, wu_ref[:, cols])
    return _dot((g * _sigmoid(g) * u).astype(BF16), wd_ref[cols, :])


def _ffn_rows_kernel(x_ref, mod_ref, gpre_ref, gpost_ref, wg_ref, wu_ref, wd_ref, o_ref, acc_ref, *, sub):
    x = x_ref[...]
    hn = (x * _rms(x) * gpre_ref[...] * (1.0 + mod_ref[1]) + mod_ref[0]).astype(BF16)
    for c in range(wg_ref.shape[1] // sub):
        down = _swiglu_chunk(hn, wg_ref, wu_ref, wd_ref, c, sub)
        if c == 0:
            acc_ref[...] = down
        else:
            acc_ref[...] += down
    y = acc_ref[...]
    o_ref[...] = x + (0.5 * (1.0 + mod_ref[2])) * (y * _rms(y) * gpost_ref[...])


def _order_before_next_read(ref, *values):
    zero = None
    for v in values:
        tok = v[v.shape[0] - 8:, v.shape[1] - LANES:]
        bits = pltpu.bitcast(tok, jnp.uint32)
        bits = lax.shift_right_logical(lax.shift_right_logical(bits, jnp.uint32(16)), jnp.uint32(16))
        zero = bits if zero is None else zero | bits
    zero = pltpu.bitcast(zero, F32)
    tile = ref[0:BF16_ROWS, 0:LANES].astype(F32)
    ref[0:BF16_ROWS, 0:LANES] = (tile + jnp.concatenate([zero, zero], axis=0)).astype(ref.dtype)


def _ffn_pipe_kernel(xn_ref, xd_ref, mod_ref, gpre_ref, gpost_ref, wg_ref, wu_ref, wd_ref, o_ref, hn_s, acc_s, *,
                     sub, n_tiles, tiles_per_group, pieces):
    s = pl.program_id(0)
    rows = xn_ref.shape[0]
    piece = rows // pieces
    n_chunks = wg_ref.shape[1] // sub
    group_in = jnp.minimum(s, n_tiles - 1) // tiles_per_group
    group_out = jnp.clip(s - 2, 0, n_tiles - 1) // tiles_per_group

    def norm_piece(k, slot):
        rs = slice(k * piece, (k + 1) * piece)
        x = xn_ref[rs, :]
        scale = mod_ref[1, pl.ds(group_in, 1), :]
        shift = mod_ref[0, pl.ds(group_in, 1), :]
        h = x * _rms(x) * gpre_ref[...] * (1.0 + scale) + shift
        hn_s[slot, rs, :] = h.astype(BF16)
        return h

    def out_piece(k, slot):
        rs = slice(k * piece, (k + 1) * piece)
        y = acc_s[slot, rs, :]
        gate = mod_ref[2, pl.ds(group_out, 1), :]
        out = xd_ref[rs, :] + (0.5 * (1.0 + gate)) * (y * _rms(y) * gpost_ref[...])
        o_ref[rs, :] = out
        return out

    @pl.when(s == 0)
    def _():
        acc_s[...] = jnp.zeros_like(acc_s)
        for k in range(pieces):
            norm_piece(k, 0)

    for parity in (0, 1):
        @pl.when((s >= 1) & (s <= n_tiles) & (s % 2 == parity))
        def _():
            for c in range(n_chunks):
                down = _swiglu_chunk(hn_s[1 - parity], wg_ref, wu_ref, wd_ref, c, sub)
                if c == 0:
                    acc_s[1 - parity] = down
                else:
                    acc_s[1 - parity] += down
                done = ([norm_piece(c, parity)] if c < pieces else []) + (
                    [out_piece(c - 2, parity)] if 2 <= c < pieces + 2 else [])
                if done:
                    _order_before_next_read(hn_s.at[1 - parity], *done)

    @pl.when(s == n_tiles + 1)
    def _():
        for k in range(pieces):
            out_piece(k, (n_tiles + 1) % 2)


def _ffn_weight_specs(layer, half):
    weight = lambda shape: pl.BlockSpec((None, None) + shape, lambda *_: (layer, half, 0, 0),
                                        pipeline_mode=pl.Buffered(1))
    gain = pl.BlockSpec((None, None, 1, D_MODEL), lambda *_: (layer, 2 * half, 0, 0))
    return [gain, gain, weight((D_MODEL, D_FF)), weight((D_MODEL, D_FF)), weight((D_FF, D_MODEL))]


def _ffn_rows(x, mod, g_pre, g_post, wg, wu, wd, layer, half):
    rows = x.shape[0]
    return pl.pallas_call(
        functools.partial(_ffn_rows_kernel, sub=FFN_SUB),
        grid=(1,),
        in_specs=[
            pl.BlockSpec((rows, D_MODEL), lambda i: (0, 0)),
            pl.BlockSpec((None, 3, rows, D_MODEL), lambda i: (layer, 2 * half, 0, 0)),
            *_ffn_weight_specs(layer, half),
        ],
        out_specs=pl.BlockSpec((rows, D_MODEL), lambda i: (0, 0)),
        out_shape=jax.ShapeDtypeStruct(x.shape, F32),
        scratch_shapes=[pltpu.VMEM((rows, D_MODEL), F32)],
        compiler_params=_params("arbitrary"),
        name="ffn_rows",
    )(x, mod, g_pre, g_post, wg, wu, wd)


def _ffn_tiles(x, mod, g_pre, g_post, wg, wu, wd, layer, half):
    groups, seq, _ = x.shape
    rows = FFN_ROWS
    tpg = seq // rows
    n_tiles = groups * tpg
    tile_in = lambda s: jnp.minimum(s, n_tiles - 1)
    tile_out = lambda s: jnp.clip(s - 2, 0, n_tiles - 1)
    x_spec = lambda tile: pl.BlockSpec((None, rows, D_MODEL), lambda s: (tile(s) // tpg, tile(s) % tpg, 0))
    return pl.pallas_call(
        functools.partial(_ffn_pipe_kernel, sub=FFN_SUB, n_tiles=n_tiles, tiles_per_group=tpg,
                          pieces=FFN_PIECES),
        grid=(n_tiles + 2,),
        in_specs=[
            x_spec(tile_in),
            x_spec(tile_out),
            pl.BlockSpec((None, 3, groups, D_MODEL), lambda s: (layer, 2 * half, 0, 0)),
            *_ffn_weight_specs(layer, half),
        ],
        out_specs=x_spec(tile_out),
        out_shape=jax.ShapeDtypeStruct(x.shape, F32),
        scratch_shapes=[pltpu.VMEM((2, rows, D_MODEL), BF16), pltpu.VMEM((2, rows, D_MODEL), F32)],
        compiler_params=_params("arbitrary"),
        name="ffn_tiles",
    )(x, x, mod, g_pre, g_post, wg, wu, wd)


def _block_diag(a, b):
    za = jnp.zeros((a.shape[0], b.shape[1]), a.dtype)
    zb = jnp.zeros((b.shape[0], a.shape[1]), a.dtype)
    return jnp.concatenate([jnp.concatenate([a, za], axis=1), jnp.concatenate([zb, b], axis=1)], axis=0)


def _mlstm_kernel(x_ref, mod_ref, gpre_ref, gpost_ref, wt_ref, wk_ref, bif_ref, ght_ref, wout_ref,
                  o_ref, c_out_ref, n_out_ref, m_out_ref,
                  st_s, m_s, yt_s, *, chunk):
    t = pl.program_id(1)

    @pl.when(t == 0)
    def _():
        st_s[...] = jnp.zeros_like(st_s)
        m_s[...] = jnp.zeros_like(m_s)

    x = x_ref[...]
    rows = x.shape[0]
    h_in = (x * _rms(x) * gpre_ref[...] * (1.0 + _group_mod(mod_ref, 1)) + _group_mod(mod_ref, 0)).astype(BF16)
    proj_t = _dot_nt(wt_ref[...], h_in)
    k_all = _dot(h_in, wk_ref[...])
    r_q, r_v, r_o = 2 * N_HEADS, 2 * N_HEADS + D_HQ, 2 * N_HEADS + D_HQ + D_HV

    s_idx = lax.broadcasted_iota(jnp.int32, (chunk, chunk), 0)
    t_idx = lax.broadcasted_iota(jnp.int32, (chunk, chunk), 1)
    causal = s_idx <= t_idx
    tri = jnp.where(causal, 1.0, 0.0).astype(BF16)
    lane = lax.broadcasted_iota(jnp.int32, (1, LANES), 1)
    low = lane < D_QK
    ones_rows = jnp.ones((BF16_ROWS, 2 * chunk), BF16)
    zero_rows = jnp.zeros((LANES - N_HEADS, chunk), F32)

    ig_all = GATE_SOFTCAP * jnp.tanh((proj_t[0:N_HEADS, :] + bif_ref[0:N_HEADS, :]) / GATE_SOFTCAP)
    lf_all = _log_sigmoid(proj_t[N_HEADS:2 * N_HEADS, :] + bif_ref[N_HEADS:, :])

    for c in range(rows // chunk):
        cs = slice(c * chunk, (c + 1) * chunk)
        lf = lf_all[:, cs]
        lf_hi = lf.astype(BF16).astype(F32)
        lf_mid = (lf - lf_hi).astype(BF16).astype(F32)
        lf_lo = lf - lf_hi - lf_mid
        b3 = _dot(jnp.concatenate([lf_hi, lf_mid, lf_lo, jnp.zeros_like(lf)], axis=0).astype(BF16), tri)
        b = b3[0:N_HEADS] + b3[N_HEADS:2 * N_HEADS] + b3[2 * N_HEADS:3 * N_HEADS]
        col = ig_all[:, cs] - b
        b_last = jnp.broadcast_to(b[:, chunk - 1:chunk], b.shape)
        col_s = jnp.concatenate([col, zero_rows], axis=0).T

        for p in range(N_HEADS // 2):
            h0, h1 = 2 * p, 2 * p + 1
            qt = [(proj_t[r_q + h * D_QK:r_q + (h + 1) * D_QK, cs] * (D_QK ** -0.5)).astype(BF16) for h in (h0, h1)]
            vt = [proj_t[r_v + h * D_V:r_v + (h + 1) * D_V, cs] for h in (h0, h1)]
            k_pair = k_all[cs, p * LANES:(p + 1) * LANES]
            state = st_s[p]
            lhs1 = jnp.concatenate([k_pair.astype(BF16), state.astype(BF16)], axis=0)
            r1 = _dot(lhs1, _block_diag(qt[0], qt[1]))
            probs, w_state, inv_floor, w_k, w_decay = [], [], [], [], []
            for i, h in enumerate((h0, h1)):
                m_prev = m_s[h:h + 1, :]
                col_m = jnp.where(causal, col_s[:, h:h + 1], -jnp.inf)
                g = jnp.maximum(m_prev, jnp.max(col_m, axis=0, keepdims=True))
                probs.append((jnp.exp(col_m - g) * r1[:chunk, i * chunk:(i + 1) * chunk]).astype(BF16))
                m_t = b[h:h + 1, :] + g
                m_new = jnp.broadcast_to(m_t[:, chunk - 1:chunk], m_t.shape)
                w_state.append(jnp.exp(m_prev - g))
                inv_floor.append(jnp.exp(-m_t))
                w_decay.append(jnp.exp(b_last[h:h + 1, :] + m_prev - m_new))
                w_k.append(jnp.exp(b_last[h:h + 1, :] - m_new + col[h:h + 1, :]))
                m_s[h:h + 1, :] = m_new
            lhs2 = jnp.concatenate([jnp.concatenate([vt[0], vt[1]], axis=1).astype(BF16), ones_rows], axis=0)
            r2 = _dot(lhs2, _block_diag(probs[0], probs[1]))
            for i, h in enumerate((h0, h1)):
                ls = slice(i * chunk, (i + 1) * chunk)
                den = w_state[i] * r1[chunk + D_V:chunk + D_V + 1, ls] + r2[D_V:D_V + 1, ls]
                inv = 1.0 / jnp.maximum(jnp.abs(den), inv_floor[i])
                ht = (w_state[i] * r1[chunk:chunk + D_V, ls] + r2[:D_V, ls]) * inv
                rn = lax.rsqrt(jnp.mean(ht * ht, axis=0, keepdims=True) + EPS)
                og = proj_t[r_o + h * D_V:r_o + (h + 1) * D_V, cs]
                yt_s[h * D_V:(h + 1) * D_V, cs] = (_sigmoid(og) * (ht * rn * ght_ref[h])).astype(BF16)
            lhs3 = jnp.concatenate(
                [jnp.concatenate([vt[i] * w_k[i] for i in range(2)], axis=1),
                 jnp.concatenate([jnp.broadcast_to(w_k[i], (BF16_ROWS, chunk)) for i in range(2)], axis=1)],
                axis=0).astype(BF16)
            rhs3 = jnp.concatenate([jnp.where(low, k_pair, 0.0), jnp.where(low, 0.0, k_pair)], axis=0).astype(BF16)
            decay = jnp.where(low, w_decay[0], w_decay[1])
            st_s[p] = state * decay + _dot(lhs3, rhs3)

    y = _dot_tn(yt_s[...], wout_ref[...])
    o_ref[...] = x + (1.0 + _group_mod(mod_ref, 2)) * (y * _rms(y) * gpost_ref[...])

    @pl.when(t == pl.num_programs(1) - 1)
    def _():
        for p in range(N_HEADS // 2):
            state = st_s[p]
            c_out_ref[p * LANES:(p + 1) * LANES, :] = state[:D_V, :].T
            n_out_ref[:, p * LANES:(p + 1) * LANES] = state[D_V:D_V + 1, :]
        m_out_ref[...] = m_s[...]


def _mixer_mod_specs(layer, batch):
    gain = pl.BlockSpec((None, None, 1, D_MODEL), lambda b, t: (layer, 1, 0, 0))
    return [pl.BlockSpec((None, 3, batch, D_MODEL), lambda b, t: (layer, 1, 0, 0)), gain, gain]


def _mlstm_prompt(x, mod, g_pre, g_post, layer, w_t, w_k, b_if, g_head_t, w_out):
    batch, seq, _ = x.shape
    rows = MIX_ROWS
    return pl.pallas_call(
        functools.partial(_mlstm_kernel, chunk=ML_CHUNK),
        grid=(batch, seq // rows),
        in_specs=[
            pl.BlockSpec((None, rows, D_MODEL), lambda b, t: (b, t, 0)),
            *_mixer_mod_specs(layer, batch),
            _resident(w_t.shape),
            _resident(w_k.shape),
            _resident(b_if.shape),
            _resident(g_head_t.shape),
            _resident(w_out.shape),
        ],
        out_specs=[
            pl.BlockSpec((None, rows, D_MODEL), lambda b, t: (b, t, 0)),
            pl.BlockSpec((None, D_HQ, D_V), lambda b, t: (b, 0, 0)),
            pl.BlockSpec((None, 1, D_HQ), lambda b, t: (b, 0, 0)),
            pl.BlockSpec((None, N_HEADS, LANES), lambda b, t: (b, 0, 0)),
        ],
        out_shape=[
            jax.ShapeDtypeStruct(x.shape, F32),
            jax.ShapeDtypeStruct((batch, D_HQ, D_V), F32),
            jax.ShapeDtypeStruct((batch, 1, D_HQ), F32),
            jax.ShapeDtypeStruct((batch, N_HEADS, LANES), F32),
        ],
        scratch_shapes=[
            pltpu.VMEM((N_HEADS // 2, ML_STATE_ROWS, LANES), F32),
            pltpu.VMEM((N_HEADS, LANES), F32),
            pltpu.VMEM((D_HV, rows), BF16),
        ],
        compiler_params=_params("arbitrary", "arbitrary"),
        name="mlstm_prompt",
    )(x, mod, g_pre, g_post, w_t, w_k, b_if, g_head_t, w_out)


def _pair_expand(cols):
    lane = lax.broadcasted_iota(jnp.int32, (cols[0].shape[0], LANES), 1)
    return jnp.concatenate(
        [jnp.where(lane < D_QK, cols[2 * p], cols[2 * p + 1]) for p in range(N_HEADS // 2)], axis=1)


def _mls_proj_kernel(x_ref, mod_ref, gpre_ref, wqkvo_ref, wif_ref, bif_ref, n_ref, m_ref,
                     qs_ref, kw_ref, ws_ref, v_ref, og_ref, a_ref, bc_ref, n_out_ref, m_out_ref):
    x = x_ref[...]
    rows = x.shape[0]
    h_in = (x * _rms(x) * gpre_ref[...] * (1.0 + mod_ref[1]) + mod_ref[0]).astype(BF16)
    proj = _dot(h_in, wqkvo_ref[...])
    gates = _dot(h_in, wif_ref[...])
    ig = GATE_SOFTCAP * jnp.tanh((gates[:, :LANES] + bif_ref[0:1, :]) / GATE_SOFTCAP)
    lf = _log_sigmoid(gates[:, LANES:] + bif_ref[1:2, :])
    m_prev = m_ref[...]
    st = lf + m_prev
    m_t = jnp.maximum(st, ig)
    w_i = jnp.exp(ig - m_t)
    w_s = jnp.exp(st - m_t)
    q = proj[:, :D_HQ] * (D_QK ** -0.5)
    k = proj[:, D_HQ:2 * D_HQ]
    n_prev = n_ref[...]
    qk = q * k
    qn = q * n_prev
    lane = lax.broadcasted_iota(jnp.int32, (rows, LANES), 1)
    lo = lane < D_QK

    def head_sums(z):
        out = []
        for p in range(N_HEADS // 2):
            zp = z[:, p * LANES:(p + 1) * LANES]
            out.append(jnp.sum(jnp.where(lo, zp, 0.0), axis=-1, keepdims=True))
            out.append(jnp.sum(jnp.where(lo, 0.0, zp), axis=-1, keepdims=True))
        return out

    qk_h = head_sums(qk)
    qn_h = head_sums(qn)
    a_all = jnp.zeros((rows, LANES), F32)
    b_all = jnp.zeros((rows, LANES), F32)
    ws_cols, wi_cols = [], []
    for hd in range(N_HEADS):
        ws_h = w_s[:, hd:hd + 1]
        wi_h = w_i[:, hd:hd + 1]
        s_h = qk_h[hd] * wi_h
        den = ws_h * qn_h[hd] + s_h
        inv = 1.0 / jnp.maximum(jnp.abs(den), jnp.exp(-m_t[:, hd:hd + 1]))
        a_all = jnp.where(lane == hd, ws_h * inv, a_all)
        b_all = jnp.where(lane == hd, s_h * inv, b_all)
        ws_cols.append(ws_h)
        wi_cols.append(wi_h)
    ws_x = _pair_expand(ws_cols)
    wi_x = _pair_expand(wi_cols)
    kw = k * wi_x
    qs_ref[...] = q
    kw_ref[...] = kw
    ws_ref[...] = ws_x
    v_ref[...] = proj[:, 2 * D_HQ:2 * D_HQ + D_HV]
    og_ref[...] = proj[:, 2 * D_HQ + D_HV:]
    a_ref[...] = a_all
    bc_ref[...] = b_all
    n_out_ref[...] = ws_x * n_prev + kw
    m_out_ref[...] = m_t


def _mls_state_kernel(c_ref, qt_ref, kwt_ref, wst_ref, v_ref, og_ref, a_ref, bc_ref, ghead_ref,
                      c_out_ref, y_ref):
    for i in range(c_ref.shape[0]):
        nst = []
        v_i = v_ref[i]
        for hd in range(N_HEADS):
            rs = slice(hd * D_QK, (hd + 1) * D_QK)
            c_h = c_ref[i, hd]
            qc = qt_ref[rs, i:i + 1]
            nst.append(jnp.sum(c_h * qc, axis=0, keepdims=True))
            c_out_ref[i, hd] = c_h * wst_ref[rs, i:i + 1] + kwt_ref[rs, i:i + 1] * v_i[hd:hd + 1, :]
        nst = jnp.concatenate(nst, axis=0)
        hh = a_ref[i] * nst + bc_ref[i] * v_i
        hn = hh * _rms(hh) * ghead_ref[...]
        y_ref[i] = _sigmoid(og_ref[i]) * hn


def _mls_out_kernel(x_ref, y_ref, mod_ref, gpost_ref, wout_ref, o_ref):
    y = _dot(y_ref[...].astype(BF16), wout_ref[...])
    o_ref[...] = x_ref[...] + (1.0 + mod_ref[2]) * (y * _rms(y) * gpost_ref[...])


def _mlstm_sample(x, mod, g_pre, g_post, w_qkvo, w_if, b_if, g_head, w_out, c0, n0, m0):
    rows = x.shape[0]
    blk = SAMPLE_BLOCK
    nblk = rows // blk
    m_pad = jnp.pad(m0, ((0, 0), (0, LANES - N_HEADS)))
    full = lambda shape: pl.BlockSpec(shape, lambda: tuple(0 for _ in shape))
    rq = jax.ShapeDtypeStruct((rows, D_HQ), F32)
    rv = jax.ShapeDtypeStruct((rows, D_HV), F32)
    rl = jax.ShapeDtypeStruct((rows, LANES), F32)
    ins = (x, mod, g_pre.reshape(1, D_MODEL), w_qkvo, w_if, b_if, n0.reshape(rows, D_HQ), m_pad)
    qs, kw, ws, v, og, a, bc, n_new, m_new = pl.pallas_call(
        _mls_proj_kernel,
        in_specs=[full(z.shape) for z in ins],
        out_specs=[full(s.shape) for s in (rq, rq, rq, rv, rv, rl, rl, rq, rl)],
        out_shape=[rq, rq, rq, rv, rv, rl, rl, rq, rl],
        compiler_params=_params(),
        name="mlstm_sample_proj",
    )(*ins)

    blocked_t = lambda z: z.reshape(nblk, blk, D_HQ).transpose(0, 2, 1)
    per_head = lambda z: z.reshape(rows, N_HEADS, D_V)
    scal = lambda z: z[:, :N_HEADS].reshape(rows, N_HEADS, 1)
    col_spec = pl.BlockSpec((None, D_HQ, blk), lambda i: (i, 0, 0))
    hv_spec = pl.BlockSpec((blk, N_HEADS, D_V), lambda i: (i, 0, 0))
    sc_spec = pl.BlockSpec((blk, N_HEADS, 1), lambda i: (i, 0, 0))
    c_spec = pl.BlockSpec((blk, N_HEADS, D_QK, D_V), lambda i: (i, 0, 0, 0))
    c_new, y3 = pl.pallas_call(
        _mls_state_kernel,
        grid=(nblk,),
        in_specs=[c_spec, col_spec, col_spec, col_spec, hv_spec, hv_spec, sc_spec, sc_spec,
                  pl.BlockSpec((N_HEADS, D_V), lambda i: (0, 0))],
        out_specs=[c_spec, hv_spec],
        out_shape=[jax.ShapeDtypeStruct(c0.shape, F32), jax.ShapeDtypeStruct((rows, N_HEADS, D_V), F32)],
        compiler_params=_params("arbitrary"),
        name="mlstm_sample_state",
    )(c0, blocked_t(qs), blocked_t(kw), blocked_t(ws), per_head(v), per_head(og), scal(a), scal(bc), g_head)

    outs = (x, y3.reshape(rows, D_HV), mod, g_post.reshape(1, D_MODEL), w_out)
    y = pl.pallas_call(
        _mls_out_kernel,
        in_specs=[full(z.shape) for z in outs],
        out_specs=full(x.shape),
        out_shape=jax.ShapeDtypeStruct(x.shape, F32),
        compiler_params=_params(),
        name="mlstm_sample_out",
    )(*outs)
    return y, c_new, n_new.reshape(rows, N_HEADS, D_QK), m_new[:, :N_HEADS]


def _conv_kernel(x_ref, mod_ref, gpre_ref, gpost_ref, win_ref, cw_ref, wout_ref, o_ref, buf_out_ref, tail_s):
    t = pl.program_id(1)

    @pl.when(t == 0)
    def _():
        tail_s[...] = jnp.zeros_like(tail_s)

    x = x_ref[...]
    rows = x.shape[0]
    h_in = (x * _rms(x) * gpre_ref[...] * (1.0 + _group_mod(mod_ref, 1)) + _group_mod(mod_ref, 0)).astype(BF16)
    p = _dot(h_in, win_ref[...])
    bg = p[:, :D_MODEL]
    u = p[:, D_MODEL:2 * D_MODEL] * p[:, 2 * D_MODEL:]
    prev1 = tail_s[7:8, :]
    prev2 = tail_s[6:7, :]
    ridx = lax.broadcasted_iota(jnp.int32, u.shape, 0)
    u1 = jnp.where(ridx == 0, prev1, pltpu.roll(u, 1, 0))
    u2 = jnp.where(ridx == 0, prev2, jnp.where(ridx == 1, prev1, pltpu.roll(u, 2, 0)))
    conv = cw_ref[0:1, :] * u2 + cw_ref[1:2, :] * u1 + cw_ref[2:3, :] * u
    tail_s[...] = u[rows - 8:, :]
    y = _dot((bg * conv).astype(BF16), wout_ref[...])
    o_ref[...] = x + (1.0 + _group_mod(mod_ref, 2)) * (y * _rms(y) * gpost_ref[...])

    @pl.when(t == pl.num_programs(1) - 1)
    def _():
        buf_out_ref[...] = u[rows - (CONV_W - 1):, :]


def _conv_prompt(x, mod, g_pre, g_post, layer, w_in, conv_w, w_out):
    batch, seq, _ = x.shape
    rows = MIX_ROWS
    return pl.pallas_call(
        _conv_kernel,
        grid=(batch, seq // rows),
        in_specs=[
            pl.BlockSpec((None, rows, D_MODEL), lambda b, t: (b, t, 0)),
            *_mixer_mod_specs(layer, batch),
            _resident(w_in.shape),
            _resident(conv_w.shape),
            _resident(w_out.shape),
        ],
        out_specs=[
            pl.BlockSpec((None, rows, D_MODEL), lambda b, t: (b, t, 0)),
            pl.BlockSpec((None, CONV_W - 1, D_MODEL), lambda b, t: (b, 0, 0)),
        ],
        out_shape=[jax.ShapeDtypeStruct(x.shape, F32),
                   jax.ShapeDtypeStruct((batch, CONV_W - 1, D_MODEL), F32)],
        scratch_shapes=[pltpu.VMEM((8, D_MODEL), F32)],
        compiler_params=_params("arbitrary", "arbitrary"),
        name="conv_prompt",
    )(x, mod, g_pre, g_post, w_in, conv_w, w_out)


def _conv_sample_kernel(x_ref, mod_ref, gpre_ref, gpost_ref, win_ref, cw_ref, wout_ref, buf_ref,
                        o_ref, buf_out_ref):
    x = x_ref[...]
    h_in = (x * _rms(x) * gpre_ref[...] * (1.0 + mod_ref[1]) + mod_ref[0]).astype(BF16)
    p = _dot(h_in, win_ref[...])
    bg = p[:, :D_MODEL]
    u = p[:, D_MODEL:2 * D_MODEL] * p[:, 2 * D_MODEL:]
    conv = cw_ref[0:1, :] * buf_ref[0] + cw_ref[1:2, :] * buf_ref[1] + cw_ref[2:3, :] * u
    y = _dot((bg * conv).astype(BF16), wout_ref[...])
    o_ref[...] = x + (1.0 + mod_ref[2]) * (y * _rms(y) * gpost_ref[...])
    buf_out_ref[0] = buf_ref[1]
    buf_out_ref[1] = u


def _conv_sample(x, mod, g_pre, g_post, w_in, conv_w, w_out, buf):
    full = lambda shape: pl.BlockSpec(shape, lambda: tuple(0 for _ in shape))
    ins = (x, mod, g_pre.reshape(1, D_MODEL), g_post.reshape(1, D_MODEL), w_in, conv_w, w_out, buf)
    return pl.pallas_call(
        _conv_sample_kernel,
        in_specs=[full(z.shape) for z in ins],
        out_specs=[full(x.shape), full(buf.shape)],
        out_shape=[jax.ShapeDtypeStruct(x.shape, F32), jax.ShapeDtypeStruct(buf.shape, F32)],
        compiler_params=_params(),
        name="conv_sample",
    )(*ins)


def kernel(x_prompt, x_sample, c_prompt, c_sample, state_mlstm_C, state_mlstm_n, state_mlstm_m, state_conv,
           w_ada, b_ada, g_pre, g_post, ffn_wg, ffn_wu, ffn_wd,
           ml_w_in, ml_b_i, ml_b_f, ml_g_head, ml_w_out, cv_w_in, cv_conv_w, cv_w_out):
    depth = w_ada.shape[0]
    batch, seq, _ = x_prompt.shape
    n_sample = x_sample.shape[0]
    n_ml = ml_w_in.shape[0]
    assert x_sample.shape[1] == 1 and seq % ML_CHUNK == 0

    mod_p, mod_s = _ada(c_prompt, c_sample, w_ada, b_ada)
    gp = g_pre.reshape(depth, 3, 1, D_MODEL)
    gq = g_post.reshape(depth, 3, 1, D_MODEL)

    wg = ffn_wg.astype(BF16)
    wu = ffn_wu.astype(BF16)
    wd = ffn_wd.astype(BF16)
    n_qkvo = 2 * D_HQ + 2 * D_HV
    w_q, w_k = ml_w_in[:, :, :D_HQ], ml_w_in[:, :, D_HQ:2 * D_HQ]
    w_vo, w_gates = ml_w_in[:, :, 2 * D_HQ:n_qkvo], ml_w_in[:, :, n_qkvo:]
    ml_t = jnp.concatenate([w_gates, w_q, w_vo], axis=-1).transpose(0, 2, 1).astype(BF16)
    ml_k = w_k.astype(BF16)
    ml_bif_t = jnp.broadcast_to(jnp.concatenate([ml_b_i, ml_b_f], axis=-1)[:, :, None], (n_ml, 2 * N_HEADS, MIX_ROWS))
    ml_ghead_t = jnp.broadcast_to(ml_g_head[:, :, :, None], (n_ml, N_HEADS, D_V, LANES))
    ml_qkvo = ml_w_in[:, :, :n_qkvo].astype(BF16)
    pad_gate = lambda w: jnp.pad(w, ((0, 0), (0, 0), (0, LANES - N_HEADS)))
    ml_if = jnp.concatenate([pad_gate(w_gates[:, :, :N_HEADS]), pad_gate(w_gates[:, :, N_HEADS:])],
                            axis=-1).astype(BF16)
    pad_bias = lambda v: jnp.pad(v, ((0, 0), (0, LANES - N_HEADS)))
    ml_bif = jnp.stack([pad_bias(ml_b_i), pad_bias(ml_b_f)], axis=1)
    ml_out = ml_w_out.astype(BF16)
    cv_in = cv_w_in.astype(BF16)
    cv_out = cv_w_out.astype(BF16)

    xp = x_prompt
    xs = x_sample.reshape(n_sample, D_MODEL)
    p_c, p_n, p_m, p_buf, s_c, s_n, s_m, s_buf = [], [], [], [], [], [], [], []
    for l in range(depth):
        ms = lambda s: mod_s[l, 3 * s:3 * s + 3]
        xp = _ffn_tiles(xp, mod_p, gp, gq, wg, wu, wd, l, 0)
        xs = _ffn_rows(xs, mod_s, gp, gq, wg, wu, wd, l, 0)
        j = l // 2
        if l % 2 == 0:
            xp, c_j, n_j, m_j = _mlstm_prompt(xp, mod_p, gp, gq, l, ml_t[j], ml_k[j],
                                              ml_bif_t[j], ml_ghead_t[j], ml_out[j])
            p_c.append(c_j.reshape(batch, N_HEADS, D_QK, D_V))
            p_n.append(n_j.reshape(batch, N_HEADS, D_QK))
            p_m.append(m_j[:, :, 0])
            ys, c_j, n_j, m_j = _mlstm_sample(xs, ms(1), g_pre[l, 1], g_post[l, 1], ml_qkvo[j], ml_if[j],
                                              ml_bif[j], ml_g_head[j], ml_out[j],
                                              state_mlstm_C[j], state_mlstm_n[j], state_mlstm_m[j])
            xs = ys
            s_c.append(c_j)
            s_n.append(n_j)
            s_m.append(m_j)
        else:
            xp, buf_j = _conv_prompt(xp, mod_p, gp, gq, l, cv_in[j], cv_conv_w[j], cv_out[j])
            p_buf.append(buf_j)
            ys, buf_j = _conv_sample(xs, ms(1), g_pre[l, 1], g_post[l, 1], cv_in[j], cv_conv_w[j], cv_out[j],
                                     state_conv[j].transpose(1, 0, 2))
            xs = ys
            s_buf.append(buf_j.transpose(1, 0, 2))
        xp = _ffn_tiles(xp, mod_p, gp, gq, wg, wu, wd, l, 1)
        xs = _ffn_rows(xs, mod_s, gp, gq, wg, wu, wd, l, 1)

    return (xp, xs.reshape(n_sample, 1, D_MODEL),
            jnp.stack(p_c), jnp.stack(p_n), jnp.stack(p_m), jnp.stack(p_buf),
            jnp.stack(s_c), jnp.stack(s_n), jnp.stack(s_m), jnp.stack(s_buf))
```

```python
import functools

import jax
import jax.numpy as jnp
from jax import lax
from jax.experimental import pallas as pl
from jax.experimental.pallas import tpu as pltpu

F32 = jnp.float32
BF16 = jnp.bfloat16

D_MODEL = 1024
N_HEADS = 8
D_QK = 64
D_V = 128
D_HQ = N_HEADS * D_QK
D_HV = N_HEADS * D_V
D_FF = 2816
N_ADA = 9
CONV_W = 3
GATE_SOFTCAP = 15.0
EPS = 1e-6

LANES = 128
BF16_ROWS = 16
MXU_COLS = 256
VMEM_LIMIT_BYTES = 56 * 1024 * 1024

FFN_ROWS = 1024
FFN_SUB = MXU_COLS
FFN_STREAM_COLS = MXU_COLS
ADA_COLS = 3 * D_MODEL
MIX_ROWS = 512
ML_CHUNK = LANES
ML_STATE_ROWS = D_V + BF16_ROWS
SAMPLE_BLOCK = 8


def _params(*sem):
    return pltpu.CompilerParams(dimension_semantics=sem, vmem_limit_bytes=VMEM_LIMIT_BYTES)


def _sigmoid(x):
    return 1.0 / (1.0 + jnp.exp(-x))


def _rms(x):
    return lax.rsqrt(jnp.mean(x * x, axis=-1, keepdims=True) + EPS)


def _dot(a, b):
    return jnp.dot(a, b, preferred_element_type=F32)


def _dot_nt(a, b):
    return lax.dot_general(a, b, (((1,), (1,)), ((), ())), preferred_element_type=F32)


def _dot_tn(a, b):
    return lax.dot_general(a, b, (((0,), (0,)), ((), ())), preferred_element_type=F32)


def _log_sigmoid(x):
    return -(jnp.maximum(-x, 0.0) + jnp.log1p(jnp.exp(-jnp.abs(x))))


def _resident(shape):
    return pl.BlockSpec(shape, lambda *_: tuple(0 for _ in shape), pipeline_mode=pl.Buffered(1))


def _ada_kernel(cp_ref, cs_ref, w_ref, b_ref, op_ref, os_ref):
    w = w_ref[...].astype(BF16)
    for c_ref, o_ref in ((cp_ref, op_ref), (cs_ref, os_ref)):
        c = c_ref[...]
        res = _dot((c * _sigmoid(c)).astype(BF16), w)
        for k in range(o_ref.shape[0]):
            o_ref[k] = res[:, k * D_MODEL:(k + 1) * D_MODEL] + b_ref[k]


def _ada(c_prompt, c_sample, w_ada, b_ada):
    depth = w_ada.shape[0]
    per_step = ADA_COLS // D_MODEL
    out = lambda c: (pl.BlockSpec((None, per_step, c.shape[0], D_MODEL), lambda l, j: (l, j, 0, 0)),
                     jax.ShapeDtypeStruct((depth, N_ADA, c.shape[0], D_MODEL), F32))
    (spec_p, shape_p), (spec_s, shape_s) = out(c_prompt), out(c_sample)
    return pl.pallas_call(
        _ada_kernel,
        grid=(depth, N_ADA // per_step),
        in_specs=[
            pl.BlockSpec(c_prompt.shape, lambda l, j: (0, 0)),
            pl.BlockSpec(c_sample.shape, lambda l, j: (0, 0)),
            pl.BlockSpec((None, D_MODEL, ADA_COLS), lambda l, j: (l, 0, j)),
            pl.BlockSpec((None, per_step, 1, D_MODEL), lambda l, j: (l, j, 0, 0)),
        ],
        out_specs=[spec_p, spec_s],
        out_shape=[shape_p, shape_s],
        compiler_params=_params("arbitrary", "arbitrary"),
        name="ada",
    )(c_prompt, c_sample, w_ada, b_ada.reshape(depth, N_ADA, 1, D_MODEL))


def _group_mod(mod_ref, j):
    return mod_ref[j, pl.ds(pl.program_id(0), 1), :]


def _swiglu_chunk(hn, wg_ref, wu_ref, wd_ref, c, sub):
    cols = slice(c * sub, (c + 1) * sub)
    g = _dot(hn, wg_ref[:, cols])
    u = _dot(hn, wu_ref[:, cols])
    return _dot((g * _sigmoid(g) * u).astype(BF16), wd_ref[cols, :])


def _ffn_tile_kernel(x_ref, mod_ref, gpre_ref, gpost_ref, wg_ref, wu_ref, wd_ref, o_ref, hn_ref, acc_ref, *, sub):
    x = x_ref[...]
    hn_ref[...] = (x * _rms(x) * gpre_ref[...] * (1.0 + _group_mod(mod_ref, 1)) + _group_mod(mod_ref, 0)
                   ).astype(BF16)
    hn = hn_ref[...]
    for c in range(wg_ref.shape[1] // sub):
        down = _swiglu_chunk(hn, wg_ref, wu_ref, wd_ref, c, sub)
        if c == 0:
            acc_ref[...] = down
        else:
            acc_ref[...] += down
    y = acc_ref[...]
    o_ref[...] = x + (0.5 * (1.0 + _group_mod(mod_ref, 2))) * (y * _rms(y) * gpost_ref[...])


def _ffn_rows_kernel(x_ref, mod_ref, gpre_ref, gpost_ref, wg_ref, wu_ref, wd_ref, o_ref, hn_ref, acc_ref):
    c = pl.program_id(0)

    @pl.when(c == 0)
    def _():
        x = x_ref[...]
        hn_ref[...] = (x * _rms(x) * gpre_ref[...] * (1.0 + mod_ref[1]) + mod_ref[0]).astype(BF16)
        acc_ref[...] = jnp.zeros_like(acc_ref)

    acc_ref[...] += _swiglu_chunk(hn_ref[...], wg_ref, wu_ref, wd_ref, 0, wg_ref.shape[1])

    @pl.when(c == pl.num_programs(0) - 1)
    def _():
        y = acc_ref[...]
        o_ref[...] = x_ref[...] + (0.5 * (1.0 + mod_ref[2])) * (y * _rms(y) * gpost_ref[...])


def _ffn_rows(x, mod, g_pre, g_post, wg, wu, wd, layer, half):
    rows = x.shape[0]
    cols = FFN_STREAM_COLS
    gain = pl.BlockSpec((None, None, 1, D_MODEL), lambda c: (layer, 2 * half, 0, 0))
    return pl.pallas_call(
        _ffn_rows_kernel,
        grid=(D_FF // cols,),
        in_specs=[
            pl.BlockSpec((rows, D_MODEL), lambda c: (0, 0)),
            pl.BlockSpec((None, 3, rows, D_MODEL), lambda c: (layer, 2 * half, 0, 0)),
            gain,
            gain,
            pl.BlockSpec((None, None, D_MODEL, cols), lambda c: (layer, half, 0, c)),
            pl.BlockSpec((None, None, D_MODEL, cols), lambda c: (layer, half, 0, c)),
            pl.BlockSpec((None, None, cols, D_MODEL), lambda c: (layer, half, c, 0)),
        ],
        out_specs=pl.BlockSpec((rows, D_MODEL), lambda c: (0, 0)),
        out_shape=jax.ShapeDtypeStruct(x.shape, F32),
        scratch_shapes=[pltpu.VMEM((rows, D_MODEL), BF16), pltpu.VMEM((rows, D_MODEL), F32)],
        compiler_params=_params("arbitrary"),
        name="ffn_rows",
    )(x, mod, g_pre, g_post, wg, wu, wd)


def _ffn_tiles(x, mod, g_pre, g_post, wg, wu, wd, layer, half):
    groups, seq, _ = x.shape
    rows = FFN_ROWS
    weight = lambda shape: pl.BlockSpec((None, None) + shape, lambda g, t: (layer, half, 0, 0),
                                        pipeline_mode=pl.Buffered(1))
    gain = pl.BlockSpec((None, None, 1, D_MODEL), lambda g, t: (layer, 2 * half, 0, 0))
    return pl.pallas_call(
        functools.partial(_ffn_tile_kernel, sub=FFN_SUB),
        grid=(groups, seq // rows),
        in_specs=[
            pl.BlockSpec((None, rows, D_MODEL), lambda g, t: (g, t, 0)),
            pl.BlockSpec((None, 3, groups, D_MODEL), lambda g, t: (layer, 2 * half, 0, 0)),
            gain,
            gain,
            weight((D_MODEL, D_FF)),
            weight((D_MODEL, D_FF)),
            weight((D_FF, D_MODEL)),
        ],
        out_specs=pl.BlockSpec((None, rows, D_MODEL), lambda g, t: (g, t, 0)),
        out_shape=jax.ShapeDtypeStruct(x.shape, F32),
        scratch_shapes=[pltpu.VMEM((rows, D_MODEL), BF16), pltpu.VMEM((rows, D_MODEL), F32)],
        compiler_params=_params("arbitrary", "arbitrary"),
        name="ffn_tiles",
    )(x, mod, g_pre, g_post, wg, wu, wd)


def _block_diag(a, b):
    za = jnp.zeros((a.shape[0], b.shape[1]), a.dtype)
    zb = jnp.zeros((b.shape[0], a.shape[1]), a.dtype)
    return jnp.concatenate([jnp.concatenate([a, za], axis=1), jnp.concatenate([zb, b], axis=1)], axis=0)


def _mlstm_kernel(x_ref, mod_ref, gpre_ref, gpost_ref, wt_ref, bif_ref, ght_ref, wout_ref,
                  o_ref, c_out_ref, n_out_ref, m_out_ref,
                  st_s, m_s, yt_s, *, chunk):
    t = pl.program_id(1)

    @pl.when(t == 0)
    def _():
        st_s[...] = jnp.zeros_like(st_s)
        m_s[...] = jnp.zeros_like(m_s)

    x = x_ref[...]
    rows = x.shape[0]
    h_in = (x * _rms(x) * gpre_ref[...] * (1.0 + _group_mod(mod_ref, 1)) + _group_mod(mod_ref, 0)).astype(BF16)
    proj_q = _dot_nt(wt_ref[0:D_HQ], h_in)
    proj_t = _dot_nt(wt_ref[2 * D_HQ:], h_in)
    k_all = _dot_nt(h_in, wt_ref[D_HQ:2 * D_HQ])
    r_v, r_o, r_i, r_f = 0, D_HV, 2 * D_HV, 2 * D_HV + N_HEADS

    s_idx = lax.broadcasted_iota(jnp.int32, (chunk, chunk), 0)
    t_idx = lax.broadcasted_iota(jnp.int32, (chunk, chunk), 1)
    causal = s_idx <= t_idx
    tri = jnp.where(causal, 1.0, 0.0).astype(BF16)
    lane = lax.broadcasted_iota(jnp.int32, (1, LANES), 1)
    low = lane < D_QK
    ones_rows = jnp.ones((BF16_ROWS, 2 * chunk), BF16)
    zero_rows = jnp.zeros((LANES - N_HEADS, chunk), F32)

    ig_all = GATE_SOFTCAP * jnp.tanh((proj_t[r_i:r_i + N_HEADS, :] + bif_ref[0:N_HEADS, :]) / GATE_SOFTCAP)
    lf_all = _log_sigmoid(proj_t[r_f:r_f + N_HEADS, :] + bif_ref[N_HEADS:, :])

    for c in range(rows // chunk):
        cs = slice(c * chunk, (c + 1) * chunk)
        lf = lf_all[:, cs]
        lf_hi = lf.astype(BF16).astype(F32)
        lf_mid = (lf - lf_hi).astype(BF16).astype(F32)
        lf_lo = lf - lf_hi - lf_mid
        b3 = _dot(jnp.concatenate([lf_hi, lf_mid, lf_lo, jnp.zeros_like(lf)], axis=0).astype(BF16), tri)
        b = b3[0:N_HEADS] + b3[N_HEADS:2 * N_HEADS] + b3[2 * N_HEADS:3 * N_HEADS]
        col = ig_all[:, cs] - b
        b_last = jnp.broadcast_to(b[:, chunk - 1:chunk], b.shape)
        col_s = jnp.concatenate([col, zero_rows], axis=0).T

        for p in range(N_HEADS // 2):
            h0, h1 = 2 * p, 2 * p + 1
            qt = [(proj_q[h * D_QK:(h + 1) * D_QK, cs] * (D_QK ** -0.5)).astype(BF16) for h in (h0, h1)]
            vt = [proj_t[r_v + h * D_V:r_v + (h + 1) * D_V, cs] for h in (h0, h1)]
            k_pair = k_all[cs, p * LANES:(p + 1) * LANES]
            state = st_s[p]
            lhs1 = jnp.concatenate([k_pair.astype(BF16), state.astype(BF16)], axis=0)
            r1 = _dot(lhs1, _block_diag(qt[0], qt[1]))
            probs, w_state, inv_floor, w_k, w_decay = [], [], [], [], []
            for i, h in enumerate((h0, h1)):
                m_prev = m_s[h:h + 1, :]
                col_m = jnp.where(causal, col_s[:, h:h + 1], -jnp.inf)
                g = jnp.maximum(m_prev, jnp.max(col_m, axis=0, keepdims=True))
                probs.append((jnp.exp(col_m - g) * r1[:chunk, i * chunk:(i + 1) * chunk]).astype(BF16))
                m_t = b[h:h + 1, :] + g
                m_new = jnp.broadcast_to(m_t[:, chunk - 1:chunk], m_t.shape)
                w_state.append(jnp.exp(m_prev - g))
                inv_floor.append(jnp.exp(-m_t))
                w_decay.append(jnp.exp(b_last[h:h + 1, :] + m_prev - m_new))
                w_k.append(jnp.exp(b_last[h:h + 1, :] - m_new + col[h:h + 1, :]))
                m_s[h:h + 1, :] = m_new
            lhs2 = jnp.concatenate([jnp.concatenate([vt[0], vt[1]], axis=1).astype(BF16), ones_rows], axis=0)
            r2 = _dot(lhs2, _block_diag(probs[0], probs[1]))
            for i, h in enumerate((h0, h1)):
                ls = slice(i * chunk, (i + 1) * chunk)
                den = w_state[i] * r1[chunk + D_V:chunk + D_V + 1, ls] + r2[D_V:D_V + 1, ls]
                inv = 1.0 / jnp.maximum(jnp.abs(den), inv_floor[i])
                ht = (w_state[i] * r1[chunk:chunk + D_V, ls] + r2[:D_V, ls]) * inv
                rn = lax.rsqrt(jnp.mean(ht * ht, axis=0, keepdims=True) + EPS)
                og = proj_t[r_o + h * D_V:r_o + (h + 1) * D_V, cs]
                yt_s[h * D_V:(h + 1) * D_V, cs] = (_sigmoid(og) * (ht * rn * ght_ref[h])).astype(BF16)
            lhs3 = jnp.concatenate(
                [jnp.concatenate([vt[i] * w_k[i] for i in range(2)], axis=1),
                 jnp.concatenate([jnp.broadcast_to(w_k[i], (BF16_ROWS, chunk)) for i in range(2)], axis=1)],
                axis=0).astype(BF16)
            rhs3 = jnp.concatenate([jnp.where(low, k_pair, 0.0), jnp.where(low, 0.0, k_pair)], axis=0).astype(BF16)
            decay = jnp.where(low, w_decay[0], w_decay[1])
            st_s[p] = state * decay + _dot(lhs3, rhs3)

    y = _dot_tn(yt_s[...], wout_ref[...])
    o_ref[...] = x + (1.0 + _group_mod(mod_ref, 2)) * (y * _rms(y) * gpost_ref[...])

    @pl.when(t == pl.num_programs(1) - 1)
    def _():
        for p in range(N_HEADS // 2):
            state = st_s[p]
            c_out_ref[p * LANES:(p + 1) * LANES, :] = state[:D_V, :].T
            n_out_ref[:, p * LANES:(p + 1) * LANES] = state[D_V:D_V + 1, :]
        m_out_ref[...] = m_s[...]


def _mixer_mod_specs(layer, batch):
    gain = pl.BlockSpec((None, None, 1, D_MODEL), lambda b, t: (layer, 1, 0, 0))
    return [pl.BlockSpec((None, 3, batch, D_MODEL), lambda b, t: (layer, 1, 0, 0)), gain, gain]


def _mlstm_prompt(x, mod, g_pre, g_post, layer, w_t, b_if, g_head_t, w_out):
    batch, seq, _ = x.shape
    rows = MIX_ROWS
    return pl.pallas_call(
        functools.partial(_mlstm_kernel, chunk=ML_CHUNK),
        grid=(batch, seq // rows),
        in_specs=[
            pl.BlockSpec((None, rows, D_MODEL), lambda b, t: (b, t, 0)),
            *_mixer_mod_specs(layer, batch),
            _resident(w_t.shape),
            _resident(b_if.shape),
            _resident(g_head_t.shape),
            _resident(w_out.shape),
        ],
        out_specs=[
            pl.BlockSpec((None, rows, D_MODEL), lambda b, t: (b, t, 0)),
            pl.BlockSpec((None, D_HQ, D_V), lambda b, t: (b, 0, 0)),
            pl.BlockSpec((None, 1, D_HQ), lambda b, t: (b, 0, 0)),
            pl.BlockSpec((None, N_HEADS, LANES), lambda b, t: (b, 0, 0)),
        ],
        out_shape=[
            jax.ShapeDtypeStruct(x.shape, F32),
            jax.ShapeDtypeStruct((batch, D_HQ, D_V), F32),
            jax.ShapeDtypeStruct((batch, 1, D_HQ), F32),
            jax.ShapeDtypeStruct((batch, N_HEADS, LANES), F32),
        ],
        scratch_shapes=[
            pltpu.VMEM((N_HEADS // 2, ML_STATE_ROWS, LANES), F32),
            pltpu.VMEM((N_HEADS, LANES), F32),
            pltpu.VMEM((D_HV, rows), BF16),
        ],
        compiler_params=_params("arbitrary", "arbitrary"),
        name="mlstm_prompt",
    )(x, mod, g_pre, g_post, w_t, b_if, g_head_t, w_out)


def _pair_expand(cols):
    lane = lax.broadcasted_iota(jnp.int32, (cols[0].shape[0], LANES), 1)
    return jnp.concatenate(
        [jnp.where(lane < D_QK, cols[2 * p], cols[2 * p + 1]) for p in range(N_HEADS // 2)], axis=1)


def _mls_proj_kernel(x_ref, mod_ref, gpre_ref, wt_ref, bif_ref, n_ref, m_ref,
                     qs_ref, kw_ref, ws_ref, v_ref, og_ref, a_ref, bc_ref, n_out_ref, m_out_ref):
    x = x_ref[...]
    rows = x.shape[0]
    h_in = (x * _rms(x) * gpre_ref[...] * (1.0 + mod_ref[1]) + mod_ref[0]).astype(BF16)
    proj = _dot_nt(h_in, wt_ref[...])
    gates = proj[:, 2 * D_HQ + 2 * D_HV:] + bif_ref[...]
    ig = (GATE_SOFTCAP * jnp.tanh(gates / GATE_SOFTCAP))[:, :N_HEADS]
    lf = _log_sigmoid(gates)[:, N_HEADS:]
    m_prev = m_ref[...]
    st = lf + m_prev
    m_t = jnp.maximum(st, ig)
    w_i = jnp.exp(ig - m_t)
    w_s = jnp.exp(st - m_t)
    q = proj[:, :D_HQ] * (D_QK ** -0.5)
    k = proj[:, D_HQ:2 * D_HQ]
    n_prev = n_ref[...]
    qk = q * k
    qn = q * n_prev
    lane = lax.broadcasted_iota(jnp.int32, (rows, LANES), 1)
    lo = lane < D_QK

    def head_sums(z):
        out = []
        for p in range(N_HEADS // 2):
            zp = z[:, p * LANES:(p + 1) * LANES]
            out.append(jnp.sum(jnp.where(lo, zp, 0.0), axis=-1, keepdims=True))
            out.append(jnp.sum(jnp.where(lo, 0.0, zp), axis=-1, keepdims=True))
        return out

    qk_h = head_sums(qk)
    qn_h = head_sums(qn)
    a_all = jnp.zeros((rows, LANES), F32)
    b_all = jnp.zeros((rows, LANES), F32)
    ws_cols, wi_cols = [], []
    for hd in range(N_HEADS):
        ws_h = w_s[:, hd:hd + 1]
        wi_h = w_i[:, hd:hd + 1]
        s_h = qk_h[hd] * wi_h
        den = ws_h * qn_h[hd] + s_h
        inv = 1.0 / jnp.maximum(jnp.abs(den), jnp.exp(-m_t[:, hd:hd + 1]))
        a_all = jnp.where(lane == hd, ws_h * inv, a_all)
        b_all = jnp.where(lane == hd, s_h * inv, b_all)
        ws_cols.append(ws_h)
        wi_cols.append(wi_h)
    ws_x = _pair_expand(ws_cols)
    wi_x = _pair_expand(wi_cols)
    kw = k * wi_x
    qs_ref[...] = q
    kw_ref[...] = kw
    ws_ref[...] = ws_x
    v_ref[...] = proj[:, 2 * D_HQ:2 * D_HQ + D_HV]
    og_ref[...] = proj[:, 2 * D_HQ + D_HV:2 * D_HQ + 2 * D_HV]
    a_ref[...] = a_all
    bc_ref[...] = b_all
    n_out_ref[...] = ws_x * n_prev + kw
    m_out_ref[...] = m_t


def _mls_state_kernel(c_ref, qt_ref, kwt_ref, wst_ref, v_ref, og_ref, a_ref, bc_ref, ghead_ref,
                      c_out_ref, y_ref):
    for i in range(c_ref.shape[0]):
        nst = []
        v_i = v_ref[i]
        for hd in range(N_HEADS):
            rs = slice(hd * D_QK, (hd + 1) * D_QK)
            c_h = c_ref[i, hd]
            qc = qt_ref[rs, i:i + 1]
            nst.append(jnp.sum(c_h * qc, axis=0, keepdims=True))
            c_out_ref[i, hd] = c_h * wst_ref[rs, i:i + 1] + kwt_ref[rs, i:i + 1] * v_i[hd:hd + 1, :]
        nst = jnp.concatenate(nst, axis=0)
        hh = a_ref[i] * nst + bc_ref[i] * v_i
        hn = hh * _rms(hh) * ghead_ref[...]
        y_ref[i] = _sigmoid(og_ref[i]) * hn


def _mls_out_kernel(x_ref, y_ref, mod_ref, gpost_ref, wout_ref, o_ref):
    y = _dot(y_ref[...].astype(BF16), wout_ref[...])
    o_ref[...] = x_ref[...] + (1.0 + mod_ref[2]) * (y * _rms(y) * gpost_ref[...])


def _mlstm_sample(x, mod, g_pre, g_post, w_t, b_if, g_head, w_out, c0, n0, m0):
    rows = x.shape[0]
    blk = SAMPLE_BLOCK
    nblk = rows // blk
    full = lambda shape: pl.BlockSpec(shape, lambda: tuple(0 for _ in shape))
    rq = jax.ShapeDtypeStruct((rows, D_HQ), F32)
    rv = jax.ShapeDtypeStruct((rows, D_HV), F32)
    rl = jax.ShapeDtypeStruct((rows, LANES), F32)
    rm = jax.ShapeDtypeStruct((rows, N_HEADS), F32)
    ins = (x, mod, g_pre.reshape(1, D_MODEL), w_t, b_if, n0.reshape(rows, D_HQ), m0)
    qs, kw, ws, v, og, a, bc, n_new, m_new = pl.pallas_call(
        _mls_proj_kernel,
        in_specs=[full(z.shape) for z in ins],
        out_specs=[full(s.shape) for s in (rq, rq, rq, rv, rv, rl, rl, rq, rm)],
        out_shape=[rq, rq, rq, rv, rv, rl, rl, rq, rm],
        compiler_params=_params(),
        name="mlstm_sample_proj",
    )(*ins)

    blocked_t = lambda z: z.reshape(nblk, blk, D_HQ).transpose(0, 2, 1)
    per_head = lambda z: z.reshape(rows, N_HEADS, D_V)
    scal = lambda z: z[:, :N_HEADS].reshape(rows, N_HEADS, 1)
    col_spec = pl.BlockSpec((None, D_HQ, blk), lambda i: (i, 0, 0))
    hv_spec = pl.BlockSpec((blk, N_HEADS, D_V), lambda i: (i, 0, 0))
    sc_spec = pl.BlockSpec((blk, N_HEADS, 1), lambda i: (i, 0, 0))
    c_spec = pl.BlockSpec((blk, N_HEADS, D_QK, D_V), lambda i: (i, 0, 0, 0))
    c_new, y3 = pl.pallas_call(
        _mls_state_kernel,
        grid=(nblk,),
        in_specs=[c_spec, col_spec, col_spec, col_spec, hv_spec, hv_spec, sc_spec, sc_spec,
                  pl.BlockSpec((N_HEADS, D_V), lambda i: (0, 0))],
        out_specs=[c_spec, hv_spec],
        out_shape=[jax.ShapeDtypeStruct(c0.shape, F32), jax.ShapeDtypeStruct((rows, N_HEADS, D_V), F32)],
        compiler_params=_params("arbitrary"),
        name="mlstm_sample_state",
    )(c0, blocked_t(qs), blocked_t(kw), blocked_t(ws), per_head(v), per_head(og), scal(a), scal(bc), g_head)

    outs = (x, y3.reshape(rows, D_HV), mod, g_post.reshape(1, D_MODEL), w_out)
    y = pl.pallas_call(
        _mls_out_kernel,
        in_specs=[full(z.shape) for z in outs],
        out_specs=full(x.shape),
        out_shape=jax.ShapeDtypeStruct(x.shape, F32),
        compiler_params=_params(),
        name="mlstm_sample_out",
    )(*outs)
    return y, c_new, n_new.reshape(rows, N_HEADS, D_QK), m_new


def _conv_kernel(x_ref, mod_ref, gpre_ref, gpost_ref, win_ref, cw_ref, wout_ref, o_ref, buf_out_ref, tail_s):
    t = pl.program_id(1)

    @pl.when(t == 0)
    def _():
        tail_s[...] = jnp.zeros_like(tail_s)

    x = x_ref[...]
    rows = x.shape[0]
    h_in = (x * _rms(x) * gpre_ref[...] * (1.0 + _group_mod(mod_ref, 1)) + _group_mod(mod_ref, 0)).astype(BF16)
    p = _dot(h_in, win_ref[...])
    bg = p[:, :D_MODEL]
    u = p[:, D_MODEL:2 * D_MODEL] * p[:, 2 * D_MODEL:]
    prev1 = tail_s[7:8, :]
    prev2 = tail_s[6:7, :]
    ridx = lax.broadcasted_iota(jnp.int32, u.shape, 0)
    u1 = jnp.where(ridx == 0, prev1, pltpu.roll(u, 1, 0))
    u2 = jnp.where(ridx == 0, prev2, jnp.where(ridx == 1, prev1, pltpu.roll(u, 2, 0)))
    conv = cw_ref[0:1, :] * u2 + cw_ref[1:2, :] * u1 + cw_ref[2:3, :] * u
    tail_s[...] = u[rows - 8:, :]
    y = _dot((bg * conv).astype(BF16), wout_ref[...])
    o_ref[...] = x + (1.0 + _group_mod(mod_ref, 2)) * (y * _rms(y) * gpost_ref[...])

    @pl.when(t == pl.num_programs(1) - 1)
    def _():
        buf_out_ref[...] = u[rows - (CONV_W - 1):, :]


def _conv_prompt(x, mod, g_pre, g_post, layer, w_in, conv_w, w_out):
    batch, seq, _ = x.shape
    rows = MIX_ROWS
    return pl.pallas_call(
        _conv_kernel,
        grid=(batch, seq // rows),
        in_specs=[
            pl.BlockSpec((None, rows, D_MODEL), lambda b, t: (b, t, 0)),
            *_mixer_mod_specs(layer, batch),
            _resident(w_in.shape),
            _resident(conv_w.shape),
            _resident(w_out.shape),
        ],
        out_specs=[
            pl.BlockSpec((None, rows, D_MODEL), lambda b, t: (b, t, 0)),
            pl.BlockSpec((None, CONV_W - 1, D_MODEL), lambda b, t: (b, 0, 0)),
        ],
        out_shape=[jax.ShapeDtypeStruct(x.shape, F32),
                   jax.ShapeDtypeStruct((batch, CONV_W - 1, D_MODEL), F32)],
        scratch_shapes=[pltpu.VMEM((8, D_MODEL), F32)],
        compiler_params=_params("arbitrary", "arbitrary"),
        name="conv_prompt",
    )(x, mod, g_pre, g_post, w_in, conv_w, w_out)


def _conv_sample_kernel(x_ref, mod_ref, gpre_ref, gpost_ref, win_ref, cw_ref, wout_ref, buf_ref,
                        o_ref, buf_out_ref):
    x = x_ref[...]
    h_in = (x * _rms(x) * gpre_ref[...] * (1.0 + mod_ref[1]) + mod_ref[0]).astype(BF16)
    p = _dot(h_in, win_ref[...])
    bg = p[:, :D_MODEL]
    u = p[:, D_MODEL:2 * D_MODEL] * p[:, 2 * D_MODEL:]
    conv = cw_ref[0:1, :] * buf_ref[0] + cw_ref[1:2, :] * buf_ref[1] + cw_ref[2:3, :] * u
    y = _dot((bg * conv).astype(BF16), wout_ref[...])
    o_ref[...] = x + (1.0 + mod_ref[2]) * (y * _rms(y) * gpost_ref[...])
    buf_out_ref[0] = buf_ref[1]
    buf_out_ref[1] = u


def _conv_sample(x, mod, g_pre, g_post, w_in, conv_w, w_out, buf):
    full = lambda shape: pl.BlockSpec(shape, lambda: tuple(0 for _ in shape))
    ins = (x, mod, g_pre.reshape(1, D_MODEL), g_post.reshape(1, D_MODEL), w_in, conv_w, w_out, buf)
    return pl.pallas_call(
        _conv_sample_kernel,
        in_specs=[full(z.shape) for z in ins],
        out_specs=[full(x.shape), full(buf.shape)],
        out_shape=[jax.ShapeDtypeStruct(x.shape, F32), jax.ShapeDtypeStruct(buf.shape, F32)],
        compiler_params=_params(),
        name="conv_sample",
    )(*ins)


def kernel(x_prompt, x_sample, c_prompt, c_sample, state_mlstm_C, state_mlstm_n, state_mlstm_m, state_conv,
           w_ada, b_ada, g_pre, g_post, ffn_wg, ffn_wu, ffn_wd,
           ml_w_in, ml_b_i, ml_b_f, ml_g_head, ml_w_out, cv_w_in, cv_conv_w, cv_w_out):
    depth = w_ada.shape[0]
    batch, seq, _ = x_prompt.shape
    n_sample = x_sample.shape[0]
    n_ml = ml_w_in.shape[0]
    assert x_sample.shape[1] == 1 and seq % ML_CHUNK == 0

    mod_p, mod_s = _ada(c_prompt, c_sample, w_ada, b_ada)
    gp = g_pre.reshape(depth, 3, 1, D_MODEL)
    gq = g_post.reshape(depth, 3, 1, D_MODEL)

    wg = ffn_wg.astype(BF16)
    wu = ffn_wu.astype(BF16)
    wd = ffn_wd.astype(BF16)
    ml_t = ml_w_in.transpose(0, 2, 1).astype(BF16)
    ml_bif = jnp.concatenate([ml_b_i, ml_b_f], axis=-1)
    ml_bif_t = jnp.broadcast_to(ml_bif[:, :, None], (n_ml, 2 * N_HEADS, MIX_ROWS))
    ml_ghead_t = jnp.broadcast_to(ml_g_head[:, :, :, None], (n_ml, N_HEADS, D_V, LANES))
    ml_out = ml_w_out.astype(BF16)
    cv_in = cv_w_in.astype(BF16)
    cv_out = cv_w_out.astype(BF16)

    xp = x_prompt
    xs = x_sample.reshape(n_sample, D_MODEL)
    p_c, p_n, p_m, p_buf, s_c, s_n, s_m, s_buf = [], [], [], [], [], [], [], []
    for l in range(depth):
        ms = lambda s: mod_s[l, 3 * s:3 * s + 3]
        xp = _ffn_tiles(xp, mod_p, gp, gq, wg, wu, wd, l, 0)
        xs = _ffn_rows(xs, mod_s, gp, gq, wg, wu, wd, l, 0)
        j = l // 2
        if l % 2 == 0:
            xp, c_j, n_j, m_j = _mlstm_prompt(xp, mod_p, gp, gq, l, ml_t[j], ml_bif_t[j], ml_ghead_t[j], ml_out[j])
            p_c.append(c_j.reshape(batch, N_HEADS, D_QK, D_V))
            p_n.append(n_j.reshape(batch, N_HEADS, D_QK))
            p_m.append(m_j[:, :, 0])
            ys, c_j, n_j, m_j = _mlstm_sample(xs, ms(1), g_pre[l, 1], g_post[l, 1], ml_t[j], ml_bif[j][None],
                                              ml_g_head[j], ml_out[j],
                                              state_mlstm_C[j], state_mlstm_n[j], state_mlstm_m[j])
            xs = ys
            s_c.append(c_j)
            s_n.append(n_j)
            s_m.append(m_j)
        else:
            xp, buf_j = _conv_prompt(xp, mod_p, gp, gq, l, cv_in[j], cv_conv_w[j], cv_out[j])
            p_buf.append(buf_j)
            ys, buf_j = _conv_sample(xs, ms(1), g_pre[l, 1], g_post[l, 1], cv_in[j], cv_conv_w[j], cv_out[j],
                                     state_conv[j].transpose(1, 0, 2))
            xs = ys
            s_buf.append(buf_j.transpose(1, 0, 2))
        xp = _ffn_tiles(xp, mod_p, gp, gq, wg, wu, wd, l, 1)
        xs = _ffn_rows(xs, mod_s, gp, gq, wg, wu, wd, l, 1)

    return (xp, xs.reshape(n_sample, 1, D_MODEL),
            jnp.stack(p_c), jnp.stack(p_n), jnp.stack(p_m), jnp.stack(p_buf),
            jnp.stack(s_c), jnp.stack(s_n), jnp.stack(s_m), jnp.stack(s_buf))
```

```python
import functools

import jax
import jax.numpy as jnp
from jax import lax
from jax.experimental import pallas as pl
from jax.experimental.pallas import tpu as pltpu

F32 = jnp.float32
BF16 = jnp.bfloat16

D_MODEL = 1024
N_HEADS = 8
D_QK = 64
D_V = 128
D_HQ = N_HEADS * D_QK
D_HV = N_HEADS * D_V
D_FF = 2816
N_ADA = 9
CONV_W = 3
GATE_SOFTCAP = 15.0
EPS = 1e-6

LANES = 128
BF16_ROWS = 16
MXU_COLS = 256
VMEM_LIMIT_BYTES = 56 * 1024 * 1024

FFN_ROWS = 1024
FFN_SUB = MXU_COLS
ADA_COLS = 3 * D_MODEL
MIX_ROWS = 1024
ML_CHUNK = LANES
ML_STATE_ROWS = D_V + BF16_ROWS
SAMPLE_BLOCK = 8


def _params(*sem):
    return pltpu.CompilerParams(dimension_semantics=sem, vmem_limit_bytes=VMEM_LIMIT_BYTES)


def _sigmoid(x):
    return 1.0 / (1.0 + jnp.exp(-x))


def _rms(x):
    return lax.rsqrt(jnp.mean(x * x, axis=-1, keepdims=True) + EPS)


def _norm_mod(x, gain, scale, shift):
    return x * _rms(x) * (gain * (1.0 + scale)) + shift


def _dot(a, b):
    return jnp.dot(a, b, preferred_element_type=F32)


def _dot_nt(a, b):
    return lax.dot_general(a, b, (((1,), (1,)), ((), ())), preferred_element_type=F32)


def _dot_tn(a, b):
    return lax.dot_general(a, b, (((0,), (0,)), ((), ())), preferred_element_type=F32)


def _log_sigmoid(x):
    return -(jnp.maximum(-x, 0.0) + jnp.log1p(jnp.exp(-jnp.abs(x))))


def _resident(shape):
    return pl.BlockSpec(shape, lambda *_: tuple(0 for _ in shape), pipeline_mode=pl.Buffered(1))


def _ada_kernel(cp_ref, cs_ref, w_ref, b_ref, op_ref, os_ref):
    w = w_ref[...].astype(BF16)
    for c_ref, o_ref in ((cp_ref, op_ref), (cs_ref, os_ref)):
        c = c_ref[...]
        res = _dot((c * _sigmoid(c)).astype(BF16), w)
        for k in range(o_ref.shape[0]):
            o_ref[k] = res[:, k * D_MODEL:(k + 1) * D_MODEL] + b_ref[k]


def _ada(c_prompt, c_sample, w_ada, b_ada):
    depth = w_ada.shape[0]
    per_step = ADA_COLS // D_MODEL
    out = lambda c: (pl.BlockSpec((None, per_step, c.shape[0], D_MODEL), lambda l, j: (l, j, 0, 0)),
                     jax.ShapeDtypeStruct((depth, N_ADA, c.shape[0], D_MODEL), F32))
    (spec_p, shape_p), (spec_s, shape_s) = out(c_prompt), out(c_sample)
    return pl.pallas_call(
        _ada_kernel,
        grid=(depth, N_ADA // per_step),
        in_specs=[
            pl.BlockSpec(c_prompt.shape, lambda l, j: (0, 0)),
            pl.BlockSpec(c_sample.shape, lambda l, j: (0, 0)),
            pl.BlockSpec((None, D_MODEL, ADA_COLS), lambda l, j: (l, 0, j)),
            pl.BlockSpec((None, per_step, 1, D_MODEL), lambda l, j: (l, j, 0, 0)),
        ],
        out_specs=[spec_p, spec_s],
        out_shape=[shape_p, shape_s],
        compiler_params=_params("arbitrary", "arbitrary"),
        name="ada",
    )(c_prompt, c_sample, w_ada, b_ada.reshape(depth, N_ADA, 1, D_MODEL))


def _group_mod(mod_ref, j):
    return mod_ref[j, pl.ds(pl.program_id(0), 1), :]


def _swiglu_chunk(hn, wg_ref, wu_ref, wd_ref, c, sub):
    cols = slice(c * sub, (c + 1) * sub)
    g = _dot(hn, wg_ref[:, cols])
    u = _dot(hn, wu_ref[:, cols])
    return _dot((g * _sigmoid(g) * u).astype(BF16), wd_ref[cols, :])


def _ffn_kernel(xp_ref, xs_ref, modp_ref, mods_ref, gpre_ref, gpost_ref, wg_hbm, wu_hbm, wd_hbm,
                op_ref, os_ref,
                wg_s, wu_s, wd_s, up_stage, dn_stage, sem, hn_s, acc_s, *,
                layer, half, n_tiles, tiles_per_group, sub):
    s = pl.program_id(0)
    n_chunks = wg_s.shape[1] // sub

    def chunk_copies(c):
        slot = c % 2
        cols = pl.ds(c * sub, sub)
        return (pltpu.make_async_copy(wg_hbm.at[layer, half, :, cols], up_stage.at[slot, 0], sem.at[slot, 0]),
                pltpu.make_async_copy(wu_hbm.at[layer, half, :, cols], up_stage.at[slot, 1], sem.at[slot, 1]),
                pltpu.make_async_copy(wd_hbm.at[layer, half, cols, :], dn_stage.at[slot], sem.at[slot, 2]))

    def stage_chunk(c):
        if c + 1 < n_chunks:
            for copy in chunk_copies(c + 1):
                copy.start()
        for copy in chunk_copies(c):
            copy.wait()
        cols = slice(c * sub, (c + 1) * sub)
        wg_s[:, cols] = up_stage[c % 2, 0].astype(BF16)
        wu_s[:, cols] = up_stage[c % 2, 1].astype(BF16)
        wd_s[cols, :] = dn_stage[c % 2].astype(BF16)

    def half_step(x_ref, o_ref, scale, shift, gate, stage_weights):
        rows = x_ref.shape[0]
        x = x_ref[...]
        hn_s[0:rows, :] = _norm_mod(x, gpre_ref[...], scale, shift).astype(BF16)
        hn = hn_s[0:rows, :]
        for c in range(n_chunks):
            if stage_weights:
                stage_chunk(c)
            down = _swiglu_chunk(hn, wg_s, wu_s, wd_s, c, sub)
            if c == 0:
                acc_s[0:rows, :] = down
            else:
                acc_s[0:rows, :] += down
        y = acc_s[0:rows, :]
        o_ref[...] = x + y * _rms(y) * (gpost_ref[...] * (0.5 * (1.0 + gate)))

    group = jnp.minimum(s, n_tiles - 1) // tiles_per_group
    group_mod = lambda j: modp_ref[j, pl.ds(group, 1), :]

    @pl.when(s == 0)
    def _():
        for copy in chunk_copies(0):
            copy.start()
        half_step(xp_ref, op_ref, group_mod(1), group_mod(0), group_mod(2), True)

    @pl.when((s > 0) & (s < n_tiles))
    def _():
        half_step(xp_ref, op_ref, group_mod(1), group_mod(0), group_mod(2), False)

    @pl.when(s == n_tiles)
    def _():
        half_step(xs_ref, os_ref, mods_ref[1], mods_ref[0], mods_ref[2], False)


def _ffn(xp, xs, mod_p, mod_s, g_pre, g_post, wg, wu, wd, layer, half):
    groups, seq, _ = xp.shape
    n_sample = xs.shape[0]
    rows = FFN_ROWS
    tpg = seq // rows
    n_tiles = groups * tpg
    sub = FFN_SUB
    tile = lambda s: jnp.minimum(s, n_tiles - 1)
    xp_spec = pl.BlockSpec((None, rows, D_MODEL), lambda s: (tile(s) // tpg, tile(s) % tpg, 0))
    xs_spec = pl.BlockSpec((n_sample, D_MODEL), lambda s: (0, 0))
    gain = pl.BlockSpec((None, None, 1, D_MODEL), lambda s: (layer, 2 * half, 0, 0))
    hbm = pl.BlockSpec(memory_space=pl.ANY)
    return pl.pallas_call(
        functools.partial(_ffn_kernel, layer=layer, half=half, n_tiles=n_tiles, tiles_per_group=tpg, sub=sub),
        grid=(n_tiles + 1,),
        in_specs=[
            xp_spec,
            xs_spec,
            pl.BlockSpec((None, 3, groups, D_MODEL), lambda s: (layer, 2 * half, 0, 0)),
            pl.BlockSpec((None, 3, n_sample, D_MODEL), lambda s: (layer, 2 * half, 0, 0)),
            gain,
            gain,
            hbm,
            hbm,
            hbm,
        ],
        out_specs=[xp_spec, xs_spec],
        out_shape=[jax.ShapeDtypeStruct(xp.shape, F32), jax.ShapeDtypeStruct(xs.shape, F32)],
        scratch_shapes=[
            pltpu.VMEM((D_MODEL, D_FF), BF16),
            pltpu.VMEM((D_MODEL, D_FF), BF16),
            pltpu.VMEM((D_FF, D_MODEL), BF16),
            pltpu.VMEM((2, 2, D_MODEL, sub), F32),
            pltpu.VMEM((2, sub, D_MODEL), F32),
            pltpu.SemaphoreType.DMA((2, 3)),
            pltpu.VMEM((max(rows, n_sample), D_MODEL), BF16),
            pltpu.VMEM((max(rows, n_sample), D_MODEL), F32),
        ],
        compiler_params=_params("arbitrary"),
        name="ffn",
    )(xp, xs, mod_p, mod_s, g_pre, g_post, wg, wu, wd)


def _block_diag(a, b):
    za = jnp.zeros((a.shape[0], b.shape[1]), a.dtype)
    zb = jnp.zeros((b.shape[0], a.shape[1]), a.dtype)
    return jnp.concatenate([jnp.concatenate([a, za], axis=1), jnp.concatenate([zb, b], axis=1)], axis=0)


def _mlstm_kernel(x_ref, mod_ref, gpre_ref, gpost_ref, wt_ref, bif_ref, ght_ref, wout_ref,
                  o_ref, c_out_ref, n_out_ref, m_out_ref,
                  st_s, m_s, yt_s, *, chunk):
    t = pl.program_id(1)

    @pl.when(t == 0)
    def _():
        st_s[...] = jnp.zeros_like(st_s)
        m_s[...] = jnp.zeros_like(m_s)

    x = x_ref[...]
    rows = x.shape[0]
    h_in = _norm_mod(x, gpre_ref[...], _group_mod(mod_ref, 1), _group_mod(mod_ref, 0)).astype(BF16)
    proj_q = _dot_nt(wt_ref[0:D_HQ], h_in)
    proj_t = _dot_nt(wt_ref[2 * D_HQ:], h_in)
    k_all = _dot_nt(h_in, wt_ref[D_HQ:2 * D_HQ])
    r_v, r_o, r_i, r_f = 0, D_HV, 2 * D_HV, 2 * D_HV + N_HEADS

    s_idx = lax.broadcasted_iota(jnp.int32, (chunk, chunk), 0)
    t_idx = lax.broadcasted_iota(jnp.int32, (chunk, chunk), 1)
    causal = s_idx <= t_idx
    tri = jnp.where(causal, 1.0, 0.0).astype(BF16)
    lane = lax.broadcasted_iota(jnp.int32, (1, LANES), 1)
    low = lane < D_QK
    ones_rows = jnp.ones((BF16_ROWS, 2 * chunk), BF16)
    zero_rows = jnp.zeros((LANES - N_HEADS, chunk), F32)

    ig_all = GATE_SOFTCAP * jnp.tanh((proj_t[r_i:r_i + N_HEADS, :] + bif_ref[0:N_HEADS, :]) / GATE_SOFTCAP)
    lf_all = _log_sigmoid(proj_t[r_f:r_f + N_HEADS, :] + bif_ref[N_HEADS:, :])

    for c in range(rows // chunk):
        cs = slice(c * chunk, (c + 1) * chunk)
        lf = lf_all[:, cs]
        lf_hi = lf.astype(BF16).astype(F32)
        lf_mid = (lf - lf_hi).astype(BF16).astype(F32)
        lf_lo = lf - lf_hi - lf_mid
        b3 = _dot(jnp.concatenate([lf_hi, lf_mid, lf_lo, jnp.zeros_like(lf)], axis=0).astype(BF16), tri)
        b = b3[0:N_HEADS] + b3[N_HEADS:2 * N_HEADS] + b3[2 * N_HEADS:3 * N_HEADS]
        col = ig_all[:, cs] - b
        b_last = jnp.broadcast_to(b[:, chunk - 1:chunk], b.shape)
        col_s = jnp.concatenate([col, zero_rows], axis=0).T

        for p in range(N_HEADS // 2):
            h0, h1 = 2 * p, 2 * p + 1
            qt = [(proj_q[h * D_QK:(h + 1) * D_QK, cs] * (D_QK ** -0.5)).astype(BF16) for h in (h0, h1)]
            vt = [proj_t[r_v + h * D_V:r_v + (h + 1) * D_V, cs] for h in (h0, h1)]
            k_pair = k_all[cs, p * LANES:(p + 1) * LANES]
            state = st_s[p]
            lhs1 = jnp.concatenate([k_pair.astype(BF16), state.astype(BF16)], axis=0)
            r1 = _dot(lhs1, _block_diag(qt[0], qt[1]))
            probs, w_state, inv_floor, w_k, w_decay = [], [], [], [], []
            for i, h in enumerate((h0, h1)):
                m_prev = m_s[h:h + 1, :]
                col_m = jnp.where(causal, col_s[:, h:h + 1], -jnp.inf)
                g = jnp.maximum(m_prev, jnp.max(col_m, axis=0, keepdims=True))
                probs.append((jnp.exp(col_m - g) * r1[:chunk, i * chunk:(i + 1) * chunk]).astype(BF16))
                m_t = b[h:h + 1, :] + g
                m_new = jnp.broadcast_to(m_t[:, chunk - 1:chunk], m_t.shape)
                w_state.append(jnp.exp(m_prev - g))
                inv_floor.append(jnp.exp(-m_t))
                w_decay.append(jnp.exp(b_last[h:h + 1, :] + m_prev - m_new))
                w_k.append(jnp.exp(b_last[h:h + 1, :] - m_new + col[h:h + 1, :]))
                m_s[h:h + 1, :] = m_new
            lhs2 = jnp.concatenate([jnp.concatenate([vt[0], vt[1]], axis=1).astype(BF16), ones_rows], axis=0)
            r2 = _dot(lhs2, _block_diag(probs[0], probs[1]))
            for i, h in enumerate((h0, h1)):
                ls = slice(i * chunk, (i + 1) * chunk)
                den = w_state[i] * r1[chunk + D_V:chunk + D_V + 1, ls] + r2[D_V:D_V + 1, ls]
                inv = 1.0 / jnp.maximum(jnp.abs(den), inv_floor[i])
                ht = (w_state[i] * r1[chunk:chunk + D_V, ls] + r2[:D_V, ls]) * inv
                rn = lax.rsqrt(jnp.mean(ht * ht, axis=0, keepdims=True) + EPS)
                og = proj_t[r_o + h * D_V:r_o + (h + 1) * D_V, cs]
                yt_s[h * D_V:(h + 1) * D_V, cs] = (_sigmoid(og) * (ht * rn * ght_ref[h])).astype(BF16)
            lhs3 = jnp.concatenate(
                [jnp.concatenate([vt[i] * w_k[i] for i in range(2)], axis=1),
                 jnp.concatenate([jnp.broadcast_to(w_k[i], (BF16_ROWS, chunk)) for i in range(2)], axis=1)],
                axis=0).astype(BF16)
            rhs3 = jnp.concatenate([jnp.where(low, k_pair, 0.0), jnp.where(low, 0.0, k_pair)], axis=0).astype(BF16)
            decay = jnp.where(low, w_decay[0], w_decay[1])
            st_s[p] = state * decay + _dot(lhs3, rhs3)

    y = _dot_tn(yt_s[...], wout_ref[...])
    o_ref[...] = x + y * _rms(y) * (gpost_ref[...] * (1.0 + _group_mod(mod_ref, 2)))

    @pl.when(t == pl.num_programs(1) - 1)
    def _():
        for p in range(N_HEADS // 2):
            state = st_s[p]
            c_out_ref[p * LANES:(p + 1) * LANES, :] = state[:D_V, :].T
            n_out_ref[:, p * LANES:(p + 1) * LANES] = state[D_V:D_V + 1, :]
        m_out_ref[...] = m_s[...]


def _mixer_mod_specs(layer, batch):
    gain = pl.BlockSpec((None, None, 1, D_MODEL), lambda b, t: (layer, 1, 0, 0))
    return [pl.BlockSpec((None, 3, batch, D_MODEL), lambda b, t: (layer, 1, 0, 0)), gain, gain]


def _mlstm_prompt(x, mod, g_pre, g_post, layer, w_t, b_if, g_head_t, w_out):
    batch, seq, _ = x.shape
    rows = MIX_ROWS
    return pl.pallas_call(
        functools.partial(_mlstm_kernel, chunk=ML_CHUNK),
        grid=(batch, seq // rows),
        in_specs=[
            pl.BlockSpec((None, rows, D_MODEL), lambda b, t: (b, t, 0)),
            *_mixer_mod_specs(layer, batch),
            _resident(w_t.shape),
            _resident(b_if.shape),
            _resident(g_head_t.shape),
            _resident(w_out.shape),
        ],
        out_specs=[
            pl.BlockSpec((None, rows, D_MODEL), lambda b, t: (b, t, 0)),
            pl.BlockSpec((None, D_HQ, D_V), lambda b, t: (b, 0, 0)),
            pl.BlockSpec((None, 1, D_HQ), lambda b, t: (b, 0, 0)),
            pl.BlockSpec((None, N_HEADS, LANES), lambda b, t: (b, 0, 0)),
        ],
        out_shape=[
            jax.ShapeDtypeStruct(x.shape, F32),
            jax.ShapeDtypeStruct((batch, D_HQ, D_V), F32),
            jax.ShapeDtypeStruct((batch, 1, D_HQ), F32),
            jax.ShapeDtypeStruct((batch, N_HEADS, LANES), F32),
        ],
        scratch_shapes=[
            pltpu.VMEM((N_HEADS // 2, ML_STATE_ROWS, LANES), F32),
            pltpu.VMEM((N_HEADS, LANES), F32),
            pltpu.VMEM((D_HV, rows), BF16),
        ],
        compiler_params=_params("arbitrary", "arbitrary"),
        name="mlstm_prompt",
    )(x, mod, g_pre, g_post, w_t, b_if, g_head_t, w_out)


def _pair_expand(cols):
    lane = lax.broadcasted_iota(jnp.int32, (cols[0].shape[0], LANES), 1)
    return jnp.concatenate(
        [jnp.where(lane < D_QK, cols[2 * p], cols[2 * p + 1]) for p in range(N_HEADS // 2)], axis=1)


def _mls_proj_kernel(x_ref, mod_ref, gpre_ref, wt_ref, bif_ref, n_ref, m_ref,
                     qs_ref, kw_ref, ws_ref, v_ref, og_ref, a_ref, bc_ref, n_out_ref, m_out_ref):
    x = x_ref[...]
    rows = x.shape[0]
    h_in = _norm_mod(x, gpre_ref[...], mod_ref[1], mod_ref[0]).astype(BF16)
    proj = _dot_nt(h_in, wt_ref[...])
    gates = proj[:, 2 * D_HQ + 2 * D_HV:] + bif_ref[...]
    ig = (GATE_SOFTCAP * jnp.tanh(gates / GATE_SOFTCAP))[:, :N_HEADS]
    lf = _log_sigmoid(gates)[:, N_HEADS:]
    m_prev = m_ref[...]
    st = lf + m_prev
    m_t = jnp.maximum(st, ig)
    w_i = jnp.exp(ig - m_t)
    w_s = jnp.exp(st - m_t)
    q = proj[:, :D_HQ] * (D_QK ** -0.5)
    k = proj[:, D_HQ:2 * D_HQ]
    n_prev = n_ref[...]
    qk = q * k
    qn = q * n_prev
    lane = lax.broadcasted_iota(jnp.int32, (rows, LANES), 1)
    lo = lane < D_QK

    def head_sums(z):
        out = []
        for p in range(N_HEADS // 2):
            zp = z[:, p * LANES:(p + 1) * LANES]
            out.append(jnp.sum(jnp.where(lo, zp, 0.0), axis=-1, keepdims=True))
            out.append(jnp.sum(jnp.where(lo, 0.0, zp), axis=-1, keepdims=True))
        return out

    qk_h = head_sums(qk)
    qn_h = head_sums(qn)
    a_all = jnp.zeros((rows, LANES), F32)
    b_all = jnp.zeros((rows, LANES), F32)
    ws_cols, wi_cols = [], []
    for hd in range(N_HEADS):
        ws_h = w_s[:, hd:hd + 1]
        wi_h = w_i[:, hd:hd + 1]
        s_h = qk_h[hd] * wi_h
        den = ws_h * qn_h[hd] + s_h
        inv = 1.0 / jnp.maximum(jnp.abs(den), jnp.exp(-m_t[:, hd:hd + 1]))
        a_all = jnp.where(lane == hd, ws_h * inv, a_all)
        b_all = jnp.where(lane == hd, s_h * inv, b_all)
        ws_cols.append(ws_h)
        wi_cols.append(wi_h)
    ws_x = _pair_expand(ws_cols)
    wi_x = _pair_expand(wi_cols)
    kw = k * wi_x
    qs_ref[...] = q
    kw_ref[...] = kw
    ws_ref[...] = ws_x
    v_ref[...] = proj[:, 2 * D_HQ:2 * D_HQ + D_HV]
    og_ref[...] = proj[:, 2 * D_HQ + D_HV:2 * D_HQ + 2 * D_HV]
    a_ref[...] = a_all
    bc_ref[...] = b_all
    n_out_ref[...] = ws_x * n_prev + kw
    m_out_ref[...] = m_t


def _mls_state_kernel(c_ref, qt_ref, kwt_ref, wst_ref, v_ref, og_ref, a_ref, bc_ref, ghead_ref,
                      c_out_ref, y_ref):
    for i in range(c_ref.shape[0]):
        nst = []
        v_i = v_ref[i]
        for hd in range(N_HEADS):
            rs = slice(hd * D_QK, (hd + 1) * D_QK)
            c_h = c_ref[i, hd]
            qc = qt_ref[rs, i:i + 1]
            nst.append(jnp.sum(c_h * qc, axis=0, keepdims=True))
            c_out_ref[i, hd] = c_h * wst_ref[rs, i:i + 1] + kwt_ref[rs, i:i + 1] * v_i[hd:hd + 1, :]
        nst = jnp.concatenate(nst, axis=0)
        hh = a_ref[i] * nst + bc_ref[i] * v_i
        hn = hh * _rms(hh) * ghead_ref[...]
        y_ref[i] = _sigmoid(og_ref[i]) * hn


def _mls_out_kernel(x_ref, y_ref, mod_ref, gpost_ref, wout_ref, o_ref):
    y = _dot(y_ref[...].astype(BF16), wout_ref[...])
    o_ref[...] = x_ref[...] + (1.0 + mod_ref[2]) * (y * _rms(y) * gpost_ref[...])


def _mlstm_sample(x, mod, g_pre, g_post, w_t, b_if, g_head, w_out, c0, n0, m0):
    rows = x.shape[0]
    blk = SAMPLE_BLOCK
    nblk = rows // blk
    full = lambda shape: pl.BlockSpec(shape, lambda: tuple(0 for _ in shape))
    rq = jax.ShapeDtypeStruct((rows, D_HQ), F32)
    rv = jax.ShapeDtypeStruct((rows, D_HV), F32)
    rl = jax.ShapeDtypeStruct((rows, LANES), F32)
    rm = jax.ShapeDtypeStruct((rows, N_HEADS), F32)
    ins = (x, mod, g_pre.reshape(1, D_MODEL), w_t, b_if, n0.reshape(rows, D_HQ), m0)
    qs, kw, ws, v, og, a, bc, n_new, m_new = pl.pallas_call(
        _mls_proj_kernel,
        in_specs=[full(z.shape) for z in ins],
        out_specs=[full(s.shape) for s in (rq, rq, rq, rv, rv, rl, rl, rq, rm)],
        out_shape=[rq, rq, rq, rv, rv, rl, rl, rq, rm],
        compiler_params=_params(),
        name="mlstm_sample_proj",
    )(*ins)

    blocked_t = lambda z: z.reshape(nblk, blk, D_HQ).transpose(0, 2, 1)
    per_head = lambda z: z.reshape(rows, N_HEADS, D_V)
    scal = lambda z: z[:, :N_HEADS].reshape(rows, N_HEADS, 1)
    col_spec = pl.BlockSpec((None, D_HQ, blk), lambda i: (i, 0, 0))
    hv_spec = pl.BlockSpec((blk, N_HEADS, D_V), lambda i: (i, 0, 0))
    sc_spec = pl.BlockSpec((blk, N_HEADS, 1), lambda i: (i, 0, 0))
    c_spec = pl.BlockSpec((blk, N_HEADS, D_QK, D_V), lambda i: (i, 0, 0, 0))
    c_new, y3 = pl.pallas_call(
        _mls_state_kernel,
        grid=(nblk,),
        in_specs=[c_spec, col_spec, col_spec, col_spec, hv_spec, hv_spec, sc_spec, sc_spec,
                  pl.BlockSpec((N_HEADS, D_V), lambda i: (0, 0))],
        out_specs=[c_spec, hv_spec],
        out_shape=[jax.ShapeDtypeStruct(c0.shape, F32), jax.ShapeDtypeStruct((rows, N_HEADS, D_V), F32)],
        compiler_params=_params("arbitrary"),
        name="mlstm_sample_state",
    )(c0, blocked_t(qs), blocked_t(kw), blocked_t(ws), per_head(v), per_head(og), scal(a), scal(bc), g_head)

    outs = (x, y3.reshape(rows, D_HV), mod, g_post.reshape(1, D_MODEL), w_out)
    y = pl.pallas_call(
        _mls_out_kernel,
        in_specs=[full(z.shape) for z in outs],
        out_specs=full(x.shape),
        out_shape=jax.ShapeDtypeStruct(x.shape, F32),
        compiler_params=_params(),
        name="mlstm_sample_out",
    )(*outs)
    return y, c_new, n_new.reshape(rows, N_HEADS, D_QK), m_new


def _conv_kernel(x_ref, mod_ref, gpre_ref, gpost_ref, win_ref, cw_ref, wout_ref, o_ref, buf_out_ref, tail_s):
    t = pl.program_id(1)

    @pl.when(t == 0)
    def _():
        tail_s[...] = jnp.zeros_like(tail_s)

    x = x_ref[...]
    rows = x.shape[0]
    h_in = _norm_mod(x, gpre_ref[...], _group_mod(mod_ref, 1), _group_mod(mod_ref, 0)).astype(BF16)
    p = _dot(h_in, win_ref[...])
    bg = p[:, :D_MODEL]
    u = p[:, D_MODEL:2 * D_MODEL] * p[:, 2 * D_MODEL:]
    prev1 = tail_s[7:8, :]
    prev2 = tail_s[6:7, :]
    ridx = lax.broadcasted_iota(jnp.int32, u.shape, 0)
    u1 = jnp.where(ridx == 0, prev1, pltpu.roll(u, 1, 0))
    u2 = jnp.where(ridx == 0, prev2, jnp.where(ridx == 1, prev1, pltpu.roll(u, 2, 0)))
    conv = cw_ref[0:1, :] * u2 + cw_ref[1:2, :] * u1 + cw_ref[2:3, :] * u
    tail_s[...] = u[rows - 8:, :]
    y = _dot((bg * conv).astype(BF16), wout_ref[...])
    o_ref[...] = x + y * _rms(y) * (gpost_ref[...] * (1.0 + _group_mod(mod_ref, 2)))

    @pl.when(t == pl.num_programs(1) - 1)
    def _():
        buf_out_ref[...] = u[rows - (CONV_W - 1):, :]


def _conv_prompt(x, mod, g_pre, g_post, layer, w_in, conv_w, w_out):
    batch, seq, _ = x.shape
    rows = MIX_ROWS
    return pl.pallas_call(
        _conv_kernel,
        grid=(batch, seq // rows),
        in_specs=[
            pl.BlockSpec((None, rows, D_MODEL), lambda b, t: (b, t, 0)),
            *_mixer_mod_specs(layer, batch),
            _resident(w_in.shape),
            _resident(conv_w.shape),
            _resident(w_out.shape),
        ],
        out_specs=[
            pl.BlockSpec((None, rows, D_MODEL), lambda b, t: (b, t, 0)),
            pl.BlockSpec((None, CONV_W - 1, D_MODEL), lambda b, t: (b, 0, 0)),
        ],
        out_shape=[jax.ShapeDtypeStruct(x.shape, F32),
                   jax.ShapeDtypeStruct((batch, CONV_W - 1, D_MODEL), F32)],
        scratch_shapes=[pltpu.VMEM((8, D_MODEL), F32)],
        compiler_params=_params("arbitrary", "arbitrary"),
        name="conv_prompt",
    )(x, mod, g_pre, g_post, w_in, conv_w, w_out)


def _conv_sample_kernel(x_ref, mod_ref, gpre_ref, gpost_ref, win_ref, cw_ref, wout_ref, buf_ref,
                        o_ref, buf_out_ref):
    x = x_ref[...]
    h_in = _norm_mod(x, gpre_ref[...], mod_ref[1], mod_ref[0]).astype(BF16)
    p = _dot(h_in, win_ref[...])
    bg = p[:, :D_MODEL]
    u = p[:, D_MODEL:2 * D_MODEL] * p[:, 2 * D_MODEL:]
    conv = cw_ref[0:1, :] * buf_ref[0] + cw_ref[1:2, :] * buf_ref[1] + cw_ref[2:3, :] * u
    y = _dot((bg * conv).astype(BF16), wout_ref[...])
    o_ref[...] = x + (1.0 + mod_ref[2]) * (y * _rms(y) * gpost_ref[...])
    buf_out_ref[0] = buf_ref[1]
    buf_out_ref[1] = u


def _conv_sample(x, mod, g_pre, g_post, w_in, conv_w, w_out, buf):
    full = lambda shape: pl.BlockSpec(shape, lambda: tuple(0 for _ in shape))
    ins = (x, mod, g_pre.reshape(1, D_MODEL), g_post.reshape(1, D_MODEL), w_in, conv_w, w_out, buf)
    return pl.pallas_call(
        _conv_sample_kernel,
        in_specs=[full(z.shape) for z in ins],
        out_specs=[full(x.shape), full(buf.shape)],
        out_shape=[jax.ShapeDtypeStruct(x.shape, F32), jax.ShapeDtypeStruct(buf.shape, F32)],
        compiler_params=_params(),
        name="conv_sample",
    )(*ins)


def kernel(x_prompt, x_sample, c_prompt, c_sample, state_mlstm_C, state_mlstm_n, state_mlstm_m, state_conv,
           w_ada, b_ada, g_pre, g_post, ffn_wg, ffn_wu, ffn_wd,
           ml_w_in, ml_b_i, ml_b_f, ml_g_head, ml_w_out, cv_w_in, cv_conv_w, cv_w_out):
    depth = w_ada.shape[0]
    batch, seq, _ = x_prompt.shape
    n_sample = x_sample.shape[0]
    n_ml = ml_w_in.shape[0]
    assert x_sample.shape[1] == 1 and seq % ML_CHUNK == 0

    mod_p, mod_s = _ada(c_prompt, c_sample, w_ada, b_ada)
    gp = g_pre.reshape(depth, 3, 1, D_MODEL)
    gq = g_post.reshape(depth, 3, 1, D_MODEL)

    ml_t = ml_w_in.transpose(0, 2, 1).astype(BF16)
    ml_bif = jnp.concatenate([ml_b_i, ml_b_f], axis=-1)
    ml_bif_t = jnp.broadcast_to(ml_bif[:, :, None], (n_ml, 2 * N_HEADS, MIX_ROWS))
    ml_ghead_t = jnp.broadcast_to(ml_g_head[:, :, :, None], (n_ml, N_HEADS, D_V, LANES))
    ml_out = ml_w_out.astype(BF16)
    cv_in = cv_w_in.astype(BF16)
    cv_out = cv_w_out.astype(BF16)

    xp = x_prompt
    xs = x_sample.reshape(n_sample, D_MODEL)
    p_c, p_n, p_m, p_buf, s_c, s_n, s_m, s_buf = [], [], [], [], [], [], [], []
    for l in range(depth):
        ms = lambda s: mod_s[l, 3 * s:3 * s + 3]
        xp, xs = _ffn(xp, xs, mod_p, mod_s, gp, gq, ffn_wg, ffn_wu, ffn_wd, l, 0)
        j = l // 2
        if l % 2 == 0:
            xp, c_j, n_j, m_j = _mlstm_prompt(xp, mod_p, gp, gq, l, ml_t[j], ml_bif_t[j], ml_ghead_t[j], ml_out[j])
            p_c.append(c_j.reshape(batch, N_HEADS, D_QK, D_V))
            p_n.append(n_j.reshape(batch, N_HEADS, D_QK))
            p_m.append(m_j[:, :, 0])
            ys, c_j, n_j, m_j = _mlstm_sample(xs, ms(1), g_pre[l, 1], g_post[l, 1], ml_t[j], ml_bif[j][None],
                                              ml_g_head[j], ml_out[j],
                                              state_mlstm_C[j], state_mlstm_n[j], state_mlstm_m[j])
            xs = ys
            s_c.append(c_j)
            s_n.append(n_j)
            s_m.append(m_j)
        else:
            xp, buf_j = _conv_prompt(xp, mod_p, gp, gq, l, cv_in[j], cv_conv_w[j], cv_out[j])
            p_buf.append(buf_j)
            ys, buf_j = _conv_sample(xs, ms(1), g_pre[l, 1], g_post[l, 1], cv_in[j], cv_conv_w[j], cv_out[j],
                                     state_conv[j].transpose(1, 0, 2))
            xs = ys
            s_buf.append(buf_j.transpose(1, 0, 2))
        xp, xs = _ffn(xp, xs, mod_p, mod_s, gp, gq, ffn_wg, ffn_wu, ffn_wd, l, 1)

    return (xp, xs.reshape(n_sample, 1, D_MODEL),
            jnp.stack(p_c), jnp.stack(p_n), jnp.stack(p_m), jnp.stack(p_buf),
            jnp.stack(s_c), jnp.stack(s_n), jnp.stack(s_m), jnp.stack(s_buf))
```

```python
import functools

import jax
import jax.numpy as jnp
from jax import lax
from jax.experimental import pallas as pl
from jax.experimental.pallas import tpu as pltpu

F32 = jnp.float32
BF16 = jnp.bfloat16

D_MODEL = 1024
N_HEADS = 8
D_QK = 64
D_V = 128
D_HQ = N_HEADS * D_QK
D_HV = N_HEADS * D_V
D_FF = 2816
N_ADA = 9
CONV_W = 3
GATE_SOFTCAP = 15.0
EPS = 1e-6

LANES = 128
BF16_ROWS = 16
MXU_COLS = 256
VMEM_LIMIT_BYTES = 56 * 1024 * 1024

FFN_ROWS = 512
FFN_SUB = MXU_COLS
ADA_COLS = 3 * D_MODEL
MIX_ROWS = 1024
ML_CHUNK = LANES
ML_STATE_ROWS = D_V + BF16_ROWS
SAMPLE_BLOCK = 8


def _params(*sem):
    return pltpu.CompilerParams(dimension_semantics=sem, vmem_limit_bytes=VMEM_LIMIT_BYTES)


def _sigmoid(x):
    return 1.0 / (1.0 + jnp.exp(-x))


def _rms(x):
    return lax.rsqrt(jnp.mean(x * x, axis=-1, keepdims=True) + EPS)


def _norm_mod(x, gain, scale, shift):
    return x * _rms(x) * (gain * (1.0 + scale)) + shift


def _dot(a, b):
    return jnp.dot(a, b, preferred_element_type=F32)


def _dot_nt(a, b):
    return lax.dot_general(a, b, (((1,), (1,)), ((), ())), preferred_element_type=F32)


def _dot_tn(a, b):
    return lax.dot_general(a, b, (((0,), (0,)), ((), ())), preferred_element_type=F32)


def _log_sigmoid(x):
    return -(jnp.maximum(-x, 0.0) + jnp.log1p(jnp.exp(-jnp.abs(x))))


def _resident(shape):
    return pl.BlockSpec(shape, lambda *_: tuple(0 for _ in shape), pipeline_mode=pl.Buffered(1))


def _ada_kernel(cp_ref, cs_ref, w_ref, b_ref, op_ref, os_ref):
    w = w_ref[...].astype(BF16)
    for c_ref, o_ref in ((cp_ref, op_ref), (cs_ref, os_ref)):
        c = c_ref[...]
        res = _dot((c * _sigmoid(c)).astype(BF16), w)
        for k in range(o_ref.shape[0]):
            o_ref[k] = res[:, k * D_MODEL:(k + 1) * D_MODEL] + b_ref[k]


def _ada(c_prompt, c_sample, w_ada, b_ada):
    depth = w_ada.shape[0]
    per_step = ADA_COLS // D_MODEL
    out = lambda c: (pl.BlockSpec((None, per_step, c.shape[0], D_MODEL), lambda l, j: (l, j, 0, 0)),
                     jax.ShapeDtypeStruct((depth, N_ADA, c.shape[0], D_MODEL), F32))
    (spec_p, shape_p), (spec_s, shape_s) = out(c_prompt), out(c_sample)
    return pl.pallas_call(
        _ada_kernel,
        grid=(depth, N_ADA // per_step),
        in_specs=[
            pl.BlockSpec(c_prompt.shape, lambda l, j: (0, 0)),
            pl.BlockSpec(c_sample.shape, lambda l, j: (0, 0)),
            pl.BlockSpec((None, D_MODEL, ADA_COLS), lambda l, j: (l, 0, j)),
            pl.BlockSpec((None, per_step, 1, D_MODEL), lambda l, j: (l, j, 0, 0)),
        ],
        out_specs=[spec_p, spec_s],
        out_shape=[shape_p, shape_s],
        compiler_params=_params("arbitrary", "arbitrary"),
        name="ada",
    )(c_prompt, c_sample, w_ada, b_ada.reshape(depth, N_ADA, 1, D_MODEL))


def _group_mod(mod_ref, j):
    return mod_ref[j, pl.ds(pl.program_id(0), 1), :]


def _swiglu_chunk(hn, wg_ref, wu_ref, wd_ref, c, sub):
    cols = slice(c * sub, (c + 1) * sub)
    g = _dot(hn, wg_ref[:, cols])
    u = _dot(hn, wu_ref[:, cols])
    return _dot((g * _sigmoid(g) * u).astype(BF16), wd_ref[cols, :])


def _ffn_kernel(xp_ref, xs_ref, modp_ref, mods_ref, gpre_ref, gpost_ref, wg_hbm, wu_hbm, wd_hbm,
                op_ref, os_ref,
                wg_s, wu_s, wd_s, up_stage, dn_stage, sem, hn_s, acc_s, *,
                layer, half, n_tiles, tiles_per_group, sub):
    s = pl.program_id(0)
    n_chunks = wg_s.shape[1] // sub

    def chunk_copies(c):
        slot = c % 2
        cols = pl.ds(c * sub, sub)
        return (pltpu.make_async_copy(wg_hbm.at[layer, half, :, cols], up_stage.at[slot, 0], sem.at[slot, 0]),
                pltpu.make_async_copy(wu_hbm.at[layer, half, :, cols], up_stage.at[slot, 1], sem.at[slot, 1]),
                pltpu.make_async_copy(wd_hbm.at[layer, half, cols, :], dn_stage.at[slot], sem.at[slot, 2]))

    def stage_weights():
        for copy in chunk_copies(0):
            copy.start()
        for c in range(n_chunks):
            if c + 1 < n_chunks:
                for copy in chunk_copies(c + 1):
                    copy.start()
            for copy in chunk_copies(c):
                copy.wait()
            cols = slice(c * sub, (c + 1) * sub)
            wg_s[:, cols] = up_stage[c % 2, 0].astype(BF16)
            wu_s[:, cols] = up_stage[c % 2, 1].astype(BF16)
            wd_s[cols, :] = dn_stage[c % 2].astype(BF16)

    def half_step(x_ref, o_ref, scale, shift, gate):
        rows = x_ref.shape[0]
        x = x_ref[...]
        hn_s[0:rows, :] = _norm_mod(x, gpre_ref[...], scale, shift).astype(BF16)
        hn = hn_s[0:rows, :]
        for c in range(n_chunks):
            down = _swiglu_chunk(hn, wg_s, wu_s, wd_s, c, sub)
            if c == 0:
                acc_s[0:rows, :] = down
            else:
                acc_s[0:rows, :] += down
        y = acc_s[0:rows, :]
        o_ref[...] = x + y * _rms(y) * (gpost_ref[...] * (0.5 * (1.0 + gate)))

    group = jnp.minimum(s, n_tiles - 1) // tiles_per_group
    group_mod = lambda j: modp_ref[j, pl.ds(group, 1), :]

    @pl.when(s == 0)
    def _():
        stage_weights()

    @pl.when(s < n_tiles)
    def _():
        half_step(xp_ref, op_ref, group_mod(1), group_mod(0), group_mod(2))

    @pl.when(s == n_tiles)
    def _():
        half_step(xs_ref, os_ref, mods_ref[1], mods_ref[0], mods_ref[2])


def _ffn(xp, xs, mod_p, mod_s, g_pre, g_post, wg, wu, wd, layer, half):
    groups, seq, _ = xp.shape
    n_sample = xs.shape[0]
    rows = FFN_ROWS
    tpg = seq // rows
    n_tiles = groups * tpg
    sub = FFN_SUB
    tile = lambda s: jnp.minimum(s, n_tiles - 1)
    xp_spec = pl.BlockSpec((None, rows, D_MODEL), lambda s: (tile(s) // tpg, tile(s) % tpg, 0))
    xs_spec = pl.BlockSpec((n_sample, D_MODEL), lambda s: (0, 0))
    gain = pl.BlockSpec((None, None, 1, D_MODEL), lambda s: (layer, 2 * half, 0, 0))
    hbm = pl.BlockSpec(memory_space=pl.ANY)
    return pl.pallas_call(
        functools.partial(_ffn_kernel, layer=layer, half=half, n_tiles=n_tiles, tiles_per_group=tpg, sub=sub),
        grid=(n_tiles + 1,),
        in_specs=[
            xp_spec,
            xs_spec,
            pl.BlockSpec((None, 3, groups, D_MODEL), lambda s: (layer, 2 * half, 0, 0)),
            pl.BlockSpec((None, 3, n_sample, D_MODEL), lambda s: (layer, 2 * half, 0, 0)),
            gain,
            gain,
            hbm,
            hbm,
            hbm,
        ],
        out_specs=[xp_spec, xs_spec],
        out_shape=[jax.ShapeDtypeStruct(xp.shape, F32), jax.ShapeDtypeStruct(xs.shape, F32)],
        scratch_shapes=[
            pltpu.VMEM((D_MODEL, D_FF), BF16),
            pltpu.VMEM((D_MODEL, D_FF), BF16),
            pltpu.VMEM((D_FF, D_MODEL), BF16),
            pltpu.VMEM((2, 2, D_MODEL, sub), F32),
            pltpu.VMEM((2, sub, D_MODEL), F32),
            pltpu.SemaphoreType.DMA((2, 3)),
            pltpu.VMEM((max(rows, n_sample), D_MODEL), BF16),
            pltpu.VMEM((max(rows, n_sample), D_MODEL), F32),
        ],
        compiler_params=_params("arbitrary"),
        name="ffn",
    )(xp, xs, mod_p, mod_s, g_pre, g_post, wg, wu, wd)


def _block_diag(a, b):
    za = jnp.zeros((a.shape[0], b.shape[1]), a.dtype)
    zb = jnp.zeros((b.shape[0], a.shape[1]), a.dtype)
    return jnp.concatenate([jnp.concatenate([a, za], axis=1), jnp.concatenate([zb, b], axis=1)], axis=0)


def _mlstm_kernel(x_ref, mod_ref, gpre_ref, gpost_ref, wt_ref, bif_ref, ght_ref, wout_ref,
                  o_ref, c_out_ref, n_out_ref, m_out_ref,
                  st_s, m_s, yt_s, *, chunk):
    t = pl.program_id(1)

    @pl.when(t == 0)
    def _():
        st_s[...] = jnp.zeros_like(st_s)
        m_s[...] = jnp.zeros_like(m_s)

    x = x_ref[...]
    rows = x.shape[0]
    h_in = _norm_mod(x, gpre_ref[...], _group_mod(mod_ref, 1), _group_mod(mod_ref, 0)).astype(BF16)
    proj_q = _dot_nt(wt_ref[0:D_HQ], h_in)
    proj_t = _dot_nt(wt_ref[2 * D_HQ:], h_in)
    k_all = _dot_nt(h_in, wt_ref[D_HQ:2 * D_HQ])
    r_v, r_o, r_i, r_f = 0, D_HV, 2 * D_HV, 2 * D_HV + N_HEADS

    s_idx = lax.broadcasted_iota(jnp.int32, (chunk, chunk), 0)
    t_idx = lax.broadcasted_iota(jnp.int32, (chunk, chunk), 1)
    causal = s_idx <= t_idx
    tri = jnp.where(causal, 1.0, 0.0).astype(BF16)
    lane = lax.broadcasted_iota(jnp.int32, (1, LANES), 1)
    low = lane < D_QK
    ones_rows = jnp.ones((BF16_ROWS, 2 * chunk), BF16)
    zero_rows = jnp.zeros((LANES - N_HEADS, chunk), F32)

    ig_all = GATE_SOFTCAP * jnp.tanh((proj_t[r_i:r_i + N_HEADS, :] + bif_ref[0:N_HEADS, :]) / GATE_SOFTCAP)
    lf_all = _log_sigmoid(proj_t[r_f:r_f + N_HEADS, :] + bif_ref[N_HEADS:, :])

    for c in range(rows // chunk):
        cs = slice(c * chunk, (c + 1) * chunk)
        lf = lf_all[:, cs]
        lf_hi = lf.astype(BF16).astype(F32)
        lf_mid = (lf - lf_hi).astype(BF16).astype(F32)
        lf_lo = lf - lf_hi - lf_mid
        b3 = _dot(jnp.concatenate([lf_hi, lf_mid, lf_lo, jnp.zeros_like(lf)], axis=0).astype(BF16), tri)
        b = b3[0:N_HEADS] + b3[N_HEADS:2 * N_HEADS] + b3[2 * N_HEADS:3 * N_HEADS]
        col = ig_all[:, cs] - b
        b_last = jnp.broadcast_to(b[:, chunk - 1:chunk], b.shape)
        col_s = jnp.concatenate([col, zero_rows], axis=0).T

        for p in range(N_HEADS // 2):
            h0, h1 = 2 * p, 2 * p + 1
            qt = [(proj_q[h * D_QK:(h + 1) * D_QK, cs] * (D_QK ** -0.5)).astype(BF16) for h in (h0, h1)]
            vt = [proj_t[r_v + h * D_V:r_v + (h + 1) * D_V, cs] for h in (h0, h1)]
            k_pair = k_all[cs, p * LANES:(p + 1) * LANES]
            state = st_s[p]
            lhs1 = jnp.concatenate([k_pair.astype(BF16), state.astype(BF16)], axis=0)
            r1 = _dot(lhs1, _block_diag(qt[0], qt[1]))
            probs, w_state, inv_floor, w_k, w_decay = [], [], [], [], []
            for i, h in enumerate((h0, h1)):
                m_prev = m_s[h:h + 1, :]
                col_m = jnp.where(causal, col_s[:, h:h + 1], -jnp.inf)
                g = jnp.maximum(m_prev, jnp.max(col_m, axis=0, keepdims=True))
                probs.append((jnp.exp(col_m - g) * r1[:chunk, i * chunk:(i + 1) * chunk]).astype(BF16))
                m_t = b[h:h + 1, :] + g
                m_new = jnp.broadcast_to(m_t[:, chunk - 1:chunk], m_t.shape)
                w_state.append(jnp.exp(m_prev - g))
                inv_floor.append(jnp.exp(-m_t))
                w_decay.append(jnp.exp(b_last[h:h + 1, :] + m_prev - m_new))
                w_k.append(jnp.exp(b_last[h:h + 1, :] - m_new + col[h:h + 1, :]))
                m_s[h:h + 1, :] = m_new
            lhs2 = jnp.concatenate([jnp.concatenate([vt[0], vt[1]], axis=1).astype(BF16), ones_rows], axis=0)
            r2 = _dot(lhs2, _block_diag(probs[0], probs[1]))
            for i, h in enumerate((h0, h1)):
                ls = slice(i * chunk, (i + 1) * chunk)
                den = w_state[i] * r1[chunk + D_V:chunk + D_V + 1, ls] + r2[D_V:D_V + 1, ls]
                inv = 1.0 / jnp.maximum(jnp.abs(den), inv_floor[i])
                ht = (w_state[i] * r1[chunk:chunk + D_V, ls] + r2[:D_V, ls]) * inv
                rn = lax.rsqrt(jnp.mean(ht * ht, axis=0, keepdims=True) + EPS)
                og = proj_t[r_o + h * D_V:r_o + (h + 1) * D_V, cs]
                yt_s[h * D_V:(h + 1) * D_V, cs] = (_sigmoid(og) * (ht * rn * ght_ref[h])).astype(BF16)
            lhs3 = jnp.concatenate(
                [jnp.concatenate([vt[i] * w_k[i] for i in range(2)], axis=1),
                 jnp.concatenate([jnp.broadcast_to(w_k[i], (BF16_ROWS, chunk)) for i in range(2)], axis=1)],
                axis=0).astype(BF16)
            rhs3 = jnp.concatenate([jnp.where(low, k_pair, 0.0), jnp.where(low, 0.0, k_pair)], axis=0).astype(BF16)
            decay = jnp.where(low, w_decay[0], w_decay[1])
            st_s[p] = state * decay + _dot(lhs3, rhs3)

    y = _dot_tn(yt_s[...], wout_ref[...])
    o_ref[...] = x + y * _rms(y) * (gpost_ref[...] * (1.0 + _group_mod(mod_ref, 2)))

    @pl.when(t == pl.num_programs(1) - 1)
    def _():
        for p in range(N_HEADS // 2):
            state = st_s[p]
            c_out_ref[p * LANES:(p + 1) * LANES, :] = state[:D_V, :].T
            n_out_ref[:, p * LANES:(p + 1) * LANES] = state[D_V:D_V + 1, :]
        m_out_ref[...] = m_s[...]


def _mixer_mod_specs(layer, batch):
    gain = pl.BlockSpec((None, None, 1, D_MODEL), lambda b, t: (layer, 1, 0, 0))
    return [pl.BlockSpec((None, 3, batch, D_MODEL), lambda b, t: (layer, 1, 0, 0)), gain, gain]


def _mlstm_prompt(x, mod, g_pre, g_post, layer, w_t, b_if, g_head_t, w_out):
    batch, seq, _ = x.shape
    rows = MIX_ROWS
    return pl.pallas_call(
        functools.partial(_mlstm_kernel, chunk=ML_CHUNK),
        grid=(batch, seq // rows),
        in_specs=[
            pl.BlockSpec((None, rows, D_MODEL), lambda b, t: (b, t, 0)),
            *_mixer_mod_specs(layer, batch),
            _resident(w_t.shape),
            _resident(b_if.shape),
            _resident(g_head_t.shape),
            _resident(w_out.shape),
        ],
        out_specs=[
            pl.BlockSpec((None, rows, D_MODEL), lambda b, t: (b, t, 0)),
            pl.BlockSpec((None, D_HQ, D_V), lambda b, t: (b, 0, 0)),
            pl.BlockSpec((None, 1, D_HQ), lambda b, t: (b, 0, 0)),
            pl.BlockSpec((None, N_HEADS, LANES), lambda b, t: (b, 0, 0)),
        ],
        out_shape=[
            jax.ShapeDtypeStruct(x.shape, F32),
            jax.ShapeDtypeStruct((batch, D_HQ, D_V), F32),
            jax.ShapeDtypeStruct((batch, 1, D_HQ), F32),
            jax.ShapeDtypeStruct((batch, N_HEADS, LANES), F32),
        ],
        scratch_shapes=[
            pltpu.VMEM((N_HEADS // 2, ML_STATE_ROWS, LANES), F32),
            pltpu.VMEM((N_HEADS, LANES), F32),
            pltpu.VMEM((D_HV, rows), BF16),
        ],
        compiler_params=_params("arbitrary", "arbitrary"),
        name="mlstm_prompt",
    )(x, mod, g_pre, g_post, w_t, b_if, g_head_t, w_out)


def _pair_expand(cols):
    lane = lax.broadcasted_iota(jnp.int32, (cols[0].shape[0], LANES), 1)
    return jnp.concatenate(
        [jnp.where(lane < D_QK, cols[2 * p], cols[2 * p + 1]) for p in range(N_HEADS // 2)], axis=1)


def _mls_proj_kernel(x_ref, mod_ref, gpre_ref, wt_ref, bif_ref, n_ref, m_ref,
                     qs_ref, kw_ref, ws_ref, v_ref, og_ref, a_ref, bc_ref, n_out_ref, m_out_ref):
    x = x_ref[...]
    rows = x.shape[0]
    h_in = _norm_mod(x, gpre_ref[...], mod_ref[1], mod_ref[0]).astype(BF16)
    proj = _dot_nt(h_in, wt_ref[...])
    gates = proj[:, 2 * D_HQ + 2 * D_HV:] + bif_ref[...]
    ig = (GATE_SOFTCAP * jnp.tanh(gates / GATE_SOFTCAP))[:, :N_HEADS]
    lf = _log_sigmoid(gates)[:, N_HEADS:]
    m_prev = m_ref[...]
    st = lf + m_prev
    m_t = jnp.maximum(st, ig)
    w_i = jnp.exp(ig - m_t)
    w_s = jnp.exp(st - m_t)
    q = proj[:, :D_HQ] * (D_QK ** -0.5)
    k = proj[:, D_HQ:2 * D_HQ]
    n_prev = n_ref[...]
    qk = q * k
    qn = q * n_prev
    lane = lax.broadcasted_iota(jnp.int32, (rows, LANES), 1)
    lo = lane < D_QK

    def head_sums(z):
        out = []
        for p in range(N_HEADS // 2):
            zp = z[:, p * LANES:(p + 1) * LANES]
            out.append(jnp.sum(jnp.where(lo, zp, 0.0), axis=-1, keepdims=True))
            out.append(jnp.sum(jnp.where(lo, 0.0, zp), axis=-1, keepdims=True))
        return out

    qk_h = head_sums(qk)
    qn_h = head_sums(qn)
    a_all = jnp.zeros((rows, LANES), F32)
    b_all = jnp.zeros((rows, LANES), F32)
    ws_cols, wi_cols = [], []
    for hd in range(N_HEADS):
        ws_h = w_s[:, hd:hd + 1]
        wi_h = w_i[:, hd:hd + 1]
        s_h = qk_h[hd] * wi_h
        den = ws_h * qn_h[hd] + s_h
        inv = 1.0 / jnp.maximum(jnp.abs(den), jnp.exp(-m_t[:, hd:hd + 1]))
        a_all = jnp.where(lane == hd, ws_h * inv, a_all)
        b_all = jnp.where(lane == hd, s_h * inv, b_all)
        ws_cols.append(ws_h)
        wi_cols.append(wi_h)
    ws_x = _pair_expand(ws_cols)
    wi_x = _pair_expand(wi_cols)
    kw = k * wi_x
    qs_ref[...] = q
    kw_ref[...] = kw
    ws_ref[...] = ws_x
    v_ref[...] = proj[:, 2 * D_HQ:2 * D_HQ + D_HV]
    og_ref[...] = proj[:, 2 * D_HQ + D_HV:2 * D_HQ + 2 * D_HV]
    a_ref[...] = a_all
    bc_ref[...] = b_all
    n_out_ref[...] = ws_x * n_prev + kw
    m_out_ref[...] = m_t


def _mls_state_kernel(c_ref, qt_ref, kwt_ref, wst_ref, v_ref, og_ref, a_ref, bc_ref, ghead_ref,
                      c_out_ref, y_ref):
    for i in range(c_ref.shape[0]):
        nst = []
        v_i = v_ref[i]
        for hd in range(N_HEADS):
            rs = slice(hd * D_QK, (hd + 1) * D_QK)
            c_h = c_ref[i, hd]
            qc = qt_ref[rs, i:i + 1]
            nst.append(jnp.sum(c_h * qc, axis=0, keepdims=True))
            c_out_ref[i, hd] = c_h * wst_ref[rs, i:i + 1] + kwt_ref[rs, i:i + 1] * v_i[hd:hd + 1, :]
        nst = jnp.concatenate(nst, axis=0)
        hh = a_ref[i] * nst + bc_ref[i] * v_i
        hn = hh * _rms(hh) * ghead_ref[...]
        y_ref[i] = _sigmoid(og_ref[i]) * hn


def _mls_out_kernel(x_ref, y_ref, mod_ref, gpost_ref, wout_ref, o_ref):
    y = _dot(y_ref[...].astype(BF16), wout_ref[...])
    o_ref[...] = x_ref[...] + (1.0 + mod_ref[2]) * (y * _rms(y) * gpost_ref[...])


def _mlstm_sample(x, mod, g_pre, g_post, w_t, b_if, g_head, w_out, c0, n0, m0):
    rows = x.shape[0]
    blk = SAMPLE_BLOCK
    nblk = rows // blk
    full = lambda shape: pl.BlockSpec(shape, lambda: tuple(0 for _ in shape))
    rq = jax.ShapeDtypeStruct((rows, D_HQ), F32)
    rv = jax.ShapeDtypeStruct((rows, D_HV), F32)
    rl = jax.ShapeDtypeStruct((rows, LANES), F32)
    rm = jax.ShapeDtypeStruct((rows, N_HEADS), F32)
    ins = (x, mod, g_pre.reshape(1, D_MODEL), w_t, b_if, n0.reshape(rows, D_HQ), m0)
    qs, kw, ws, v, og, a, bc, n_new, m_new = pl.pallas_call(
        _mls_proj_kernel,
        in_specs=[full(z.shape) for z in ins],
        out_specs=[full(s.shape) for s in (rq, rq, rq, rv, rv, rl, rl, rq, rm)],
        out_shape=[rq, rq, rq, rv, rv, rl, rl, rq, rm],
        compiler_params=_params(),
        name="mlstm_sample_proj",
    )(*ins)

    blocked_t = lambda z: z.reshape(nblk, blk, D_HQ).transpose(0, 2, 1)
    per_head = lambda z: z.reshape(rows, N_HEADS, D_V)
    scal = lambda z: z[:, :N_HEADS].reshape(rows, N_HEADS, 1)
    col_spec = pl.BlockSpec((None, D_HQ, blk), lambda i: (i, 0, 0))
    hv_spec = pl.BlockSpec((blk, N_HEADS, D_V), lambda i: (i, 0, 0))
    sc_spec = pl.BlockSpec((blk, N_HEADS, 1), lambda i: (i, 0, 0))
    c_spec = pl.BlockSpec((blk, N_HEADS, D_QK, D_V), lambda i: (i, 0, 0, 0))
    c_new, y3 = pl.pallas_call(
        _mls_state_kernel,
        grid=(nblk,),
        in_specs=[c_spec, col_spec, col_spec, col_spec, hv_spec, hv_spec, sc_spec, sc_spec,
                  pl.BlockSpec((N_HEADS, D_V), lambda i: (0, 0))],
        out_specs=[c_spec, hv_spec],
        out_shape=[jax.ShapeDtypeStruct(c0.shape, F32), jax.ShapeDtypeStruct((rows, N_HEADS, D_V), F32)],
        compiler_params=_params("arbitrary"),
        name="mlstm_sample_state",
    )(c0, blocked_t(qs), blocked_t(kw), blocked_t(ws), per_head(v), per_head(og), scal(a), scal(bc), g_head)

    outs = (x, y3.reshape(rows, D_HV), mod, g_post.reshape(1, D_MODEL), w_out)
    y = pl.pallas_call(
        _mls_out_kernel,
        in_specs=[full(z.shape) for z in outs],
        out_specs=full(x.shape),
        out_shape=jax.ShapeDtypeStruct(x.shape, F32),
        compiler_params=_params(),
        name="mlstm_sample_out",
    )(*outs)
    return y, c_new, n_new.reshape(rows, N_HEADS, D_QK), m_new


def _conv_kernel(x_ref, mod_ref, gpre_ref, gpost_ref, win_ref, cw_ref, wout_ref, o_ref, buf_out_ref, tail_s):
    t = pl.program_id(1)

    @pl.when(t == 0)
    def _():
        tail_s[...] = jnp.zeros_like(tail_s)

    x = x_ref[...]
    rows = x.shape[0]
    h_in = _norm_mod(x, gpre_ref[...], _group_mod(mod_ref, 1), _group_mod(mod_ref, 0)).astype(BF16)
    p = _dot(h_in, win_ref[...])
    bg = p[:, :D_MODEL]
    u = p[:, D_MODEL:2 * D_MODEL] * p[:, 2 * D_MODEL:]
    prev1 = tail_s[7:8, :]
    prev2 = tail_s[6:7, :]
    ridx = lax.broadcasted_iota(jnp.int32, u.shape, 0)
    u1 = jnp.where(ridx == 0, prev1, pltpu.roll(u, 1, 0))
    u2 = jnp.where(ridx == 0, prev2, jnp.where(ridx == 1, prev1, pltpu.roll(u, 2, 0)))
    conv = cw_ref[0:1, :] * u2 + cw_ref[1:2, :] * u1 + cw_ref[2:3, :] * u
    tail_s[...] = u[rows - 8:, :]
    y = _dot((bg * conv).astype(BF16), wout_ref[...])
    o_ref[...] = x + y * _rms(y) * (gpost_ref[...] * (1.0 + _group_mod(mod_ref, 2)))

    @pl.when(t == pl.num_programs(1) - 1)
    def _():
        buf_out_ref[...] = u[rows - (CONV_W - 1):, :]


def _conv_prompt(x, mod, g_pre, g_post, layer, w_in, conv_w, w_out):
    batch, seq, _ = x.shape
    rows = MIX_ROWS
    return pl.pallas_call(
        _conv_kernel,
        grid=(batch, seq // rows),
        in_specs=[
            pl.BlockSpec((None, rows, D_MODEL), lambda b, t: (b, t, 0)),
            *_mixer_mod_specs(layer, batch),
            _resident(w_in.shape),
            _resident(conv_w.shape),
            _resident(w_out.shape),
        ],
        out_specs=[
            pl.BlockSpec((None, rows, D_MODEL), lambda b, t: (b, t, 0)),
            pl.BlockSpec((None, CONV_W - 1, D_MODEL), lambda b, t: (b, 0, 0)),
        ],
        out_shape=[jax.ShapeDtypeStruct(x.shape, F32),
                   jax.ShapeDtypeStruct((batch, CONV_W - 1, D_MODEL), F32)],
        scratch_shapes=[pltpu.VMEM((8, D_MODEL), F32)],
        compiler_params=_params("arbitrary", "arbitrary"),
        name="conv_prompt",
    )(x, mod, g_pre, g_post, w_in, conv_w, w_out)


def _conv_sample_kernel(x_ref, mod_ref, gpre_ref, gpost_ref, win_ref, cw_ref, wout_ref, buf_ref,
                        o_ref, buf_out_ref):
    x = x_ref[...]
    h_in = _norm_mod(x, gpre_ref[...], mod_ref[1], mod_ref[0]).astype(BF16)
    p = _dot(h_in, win_ref[...])
    bg = p[:, :D_MODEL]
    u = p[:, D_MODEL:2 * D_MODEL] * p[:, 2 * D_MODEL:]
    conv = cw_ref[0:1, :] * buf_ref[0] + cw_ref[1:2, :] * buf_ref[1] + cw_ref[2:3, :] * u
    y = _dot((bg * conv).astype(BF16), wout_ref[...])
    o_ref[...] = x + (1.0 + mod_ref[2]) * (y * _rms(y) * gpost_ref[...])
    buf_out_ref[0] = buf_ref[1]
    buf_out_ref[1] = u


def _conv_sample(x, mod, g_pre, g_post, w_in, conv_w, w_out, buf):
    full = lambda shape: pl.BlockSpec(shape, lambda: tuple(0 for _ in shape))
    ins = (x, mod, g_pre.reshape(1, D_MODEL), g_post.reshape(1, D_MODEL), w_in, conv_w, w_out, buf)
    return pl.pallas_call(
        _conv_sample_kernel,
        in_specs=[full(z.shape) for z in ins],
        out_specs=[full(x.shape), full(buf.shape)],
        out_shape=[jax.ShapeDtypeStruct(x.shape, F32), jax.ShapeDtypeStruct(buf.shape, F32)],
        compiler_params=_params(),
        name="conv_sample",
    )(*ins)


def kernel(x_prompt, x_sample, c_prompt, c_sample, state_mlstm_C, state_mlstm_n, state_mlstm_m, state_conv,
           w_ada, b_ada, g_pre, g_post, ffn_wg, ffn_wu, ffn_wd,
           ml_w_in, ml_b_i, ml_b_f, ml_g_head, ml_w_out, cv_w_in, cv_conv_w, cv_w_out):
    depth = w_ada.shape[0]
    batch, seq, _ = x_prompt.shape
    n_sample = x_sample.shape[0]
    n_ml = ml_w_in.shape[0]
    assert x_sample.shape[1] == 1 and seq % ML_CHUNK == 0

    mod_p, mod_s = _ada(c_prompt, c_sample, w_ada, b_ada)
    gp = g_pre.reshape(depth, 3, 1, D_MODEL)
    gq = g_post.reshape(depth, 3, 1, D_MODEL)

    ml_t = ml_w_in.transpose(0, 2, 1).astype(BF16)
    ml_bif = jnp.concatenate([ml_b_i, ml_b_f], axis=-1)
    ml_bif_t = jnp.broadcast_to(ml_bif[:, :, None], (n_ml, 2 * N_HEADS, MIX_ROWS))
    ml_ghead_t = jnp.broadcast_to(ml_g_head[:, :, :, None], (n_ml, N_HEADS, D_V, LANES))
    ml_out = ml_w_out.astype(BF16)
    cv_in = cv_w_in.astype(BF16)
    cv_out = cv_w_out.astype(BF16)

    xp = x_prompt
    xs = x_sample.reshape(n_sample, D_MODEL)
    p_c, p_n, p_m, p_buf, s_c, s_n, s_m, s_buf = [], [], [], [], [], [], [], []
    for l in range(depth):
        ms = lambda s: mod_s[l, 3 * s:3 * s + 3]
        xp, xs = _ffn(xp, xs, mod_p, mod_s, gp, gq, ffn_wg, ffn_wu, ffn_wd, l, 0)
        j = l // 2
        if l % 2 == 0:
            xp, c_j, n_j, m_j = _mlstm_prompt(xp, mod_p, gp, gq, l, ml_t[j], ml_bif_t[j], ml_ghead_t[j], ml_out[j])
            p_c.append(c_j.reshape(batch, N_HEADS, D_QK, D_V))
            p_n.append(n_j.reshape(batch, N_HEADS, D_QK))
            p_m.append(m_j[:, :, 0])
            ys, c_j, n_j, m_j = _mlstm_sample(xs, ms(1), g_pre[l, 1], g_post[l, 1], ml_t[j], ml_bif[j][None],
                                              ml_g_head[j], ml_out[j],
                                              state_mlstm_C[j], state_mlstm_n[j], state_mlstm_m[j])
            xs = ys
            s_c.append(c_j)
            s_n.append(n_j)
            s_m.append(m_j)
        else:
            xp, buf_j = _conv_prompt(xp, mod_p, gp, gq, l, cv_in[j], cv_conv_w[j], cv_out[j])
            p_buf.append(buf_j)
            ys, buf_j = _conv_sample(xs, ms(1), g_pre[l, 1], g_post[l, 1], cv_in[j], cv_conv_w[j], cv_out[j],
                                     state_conv[j].transpose(1, 0, 2))
            xs = ys
            s_buf.append(buf_j.transpose(1, 0, 2))
        xp, xs = _ffn(xp, xs, mod_p, mod_s, gp, gq, ffn_wg, ffn_wu, ffn_wd, l, 1)

    return (xp, xs.reshape(n_sample, 1, D_MODEL),
            jnp.stack(p_c), jnp.stack(p_n), jnp.stack(p_m), jnp.stack(p_buf),
            jnp.stack(s_c), jnp.stack(s_n), jnp.stack(s_m), jnp.stack(s_buf))
```

```python
import functools

import jax
import jax.numpy as jnp
from jax import lax
from jax.experimental import pallas as pl
from jax.experimental.pallas import tpu as pltpu

F32 = jnp.float32
BF16 = jnp.bfloat16

D_MODEL = 1024
N_HEADS = 8
D_QK = 64
D_V = 128
D_HQ = N_HEADS * D_QK
D_HV = N_HEADS * D_V
D_FF = 2816
N_ADA = 9
CONV_W = 3
GATE_SOFTCAP = 15.0
EPS = 1e-6

LANES = 128
BF16_ROWS = 16
MXU_COLS = 256
VMEM_LIMIT_BYTES = 56 * 1024 * 1024

FFN_ROWS = 512
FFN_SUB = MXU_COLS
ADA_COLS = 3 * D_MODEL
MIX_ROWS = 1024
ML_CHUNK = LANES
ML_STATE_ROWS = D_V + BF16_ROWS
SAMPLE_BLOCK = 8


def _params(*sem):
    return pltpu.CompilerParams(dimension_semantics=sem, vmem_limit_bytes=VMEM_LIMIT_BYTES)


def _sigmoid(x):
    return 1.0 / (1.0 + jnp.exp(-x))


def _rms(x):
    return lax.rsqrt(jnp.mean(x * x, axis=-1, keepdims=True) + EPS)


def _norm_mod(x, gain, scale, shift):
    return x * _rms(x) * (gain * (1.0 + scale)) + shift


def _dot(a, b):
    return jnp.dot(a, b, preferred_element_type=F32)


def _dot_nt(a, b):
    return lax.dot_general(a, b, (((1,), (1,)), ((), ())), preferred_element_type=F32)


def _dot_tn(a, b):
    return lax.dot_general(a, b, (((0,), (0,)), ((), ())), preferred_element_type=F32)


def _log_sigmoid(x):
    return -(jnp.maximum(-x, 0.0) + jnp.log1p(jnp.exp(-jnp.abs(x))))


def _resident(shape):
    return pl.BlockSpec(shape, lambda *_: tuple(0 for _ in shape), pipeline_mode=pl.Buffered(1))


def _ada_kernel(cp_ref, cs_ref, w_ref, b_ref, op_ref, os_ref):
    w = w_ref[...].astype(BF16)
    for c_ref, o_ref in ((cp_ref, op_ref), (cs_ref, os_ref)):
        c = c_ref[...]
        res = _dot((c * _sigmoid(c)).astype(BF16), w)
        for k in range(o_ref.shape[0]):
            o_ref[k] = res[:, k * D_MODEL:(k + 1) * D_MODEL] + b_ref[k]


def _ada(c_prompt, c_sample, w_ada, b_ada):
    depth = w_ada.shape[0]
    per_step = ADA_COLS // D_MODEL
    out = lambda c: (pl.BlockSpec((None, per_step, c.shape[0], D_MODEL), lambda l, j: (l, j, 0, 0)),
                     jax.ShapeDtypeStruct((depth, N_ADA, c.shape[0], D_MODEL), F32))
    (spec_p, shape_p), (spec_s, shape_s) = out(c_prompt), out(c_sample)
    return pl.pallas_call(
        _ada_kernel,
        grid=(depth, N_ADA // per_step),
        in_specs=[
            pl.BlockSpec(c_prompt.shape, lambda l, j: (0, 0)),
            pl.BlockSpec(c_sample.shape, lambda l, j: (0, 0)),
            pl.BlockSpec((None, D_MODEL, ADA_COLS), lambda l, j: (l, 0, j)),
            pl.BlockSpec((None, per_step, 1, D_MODEL), lambda l, j: (l, j, 0, 0)),
        ],
        out_specs=[spec_p, spec_s],
        out_shape=[shape_p, shape_s],
        compiler_params=_params("arbitrary", "arbitrary"),
        name="ada",
    )(c_prompt, c_sample, w_ada, b_ada.reshape(depth, N_ADA, 1, D_MODEL))


def _group_mod(mod_ref, j):
    return mod_ref[j, pl.ds(pl.program_id(0), 1), :]


def _swiglu_chunk(hn, wg_ref, wu_ref, wd_ref, c, sub):
    cols = slice(c * sub, (c + 1) * sub)
    g = _dot(hn, wg_ref[:, cols])
    u = _dot(hn, wu_ref[:, cols])
    return _dot((g * _sigmoid(g) * u).astype(BF16), wd_ref[cols, :])


def _ffn_kernel(xp_ref, xs_ref, modp_ref, mods_ref, gpre_ref, gpost_ref, wg_hbm, wu_hbm, wd_hbm,
                op_ref, os_ref,
                wg_s, wu_s, wd_s, up_stage, dn_stage, sem, hn_s, acc_s, *,
                layer, half, n_tiles, tiles_per_group, sub):
    s = pl.program_id(0)
    n_chunks = wg_s.shape[1] // sub

    def chunk_copies(c):
        slot = c % 2
        cols = pl.ds(c * sub, sub)
        return (pltpu.make_async_copy(wg_hbm.at[layer, half, :, cols], up_stage.at[slot, 0], sem.at[slot, 0]),
                pltpu.make_async_copy(wu_hbm.at[layer, half, :, cols], up_stage.at[slot, 1], sem.at[slot, 1]),
                pltpu.make_async_copy(wd_hbm.at[layer, half, cols, :], dn_stage.at[slot], sem.at[slot, 2]))

    def stage_weights():
        for copy in chunk_copies(0):
            copy.start()
        for c in range(n_chunks):
            if c + 1 < n_chunks:
                for copy in chunk_copies(c + 1):
                    copy.start()
            for copy in chunk_copies(c):
                copy.wait()
            cols = slice(c * sub, (c + 1) * sub)
            wg_s[:, cols] = up_stage[c % 2, 0].astype(BF16)
            wu_s[:, cols] = up_stage[c % 2, 1].astype(BF16)
            wd_s[cols, :] = dn_stage[c % 2].astype(BF16)

    def half_step(x_ref, o_ref, scale, shift, gate):
        rows = x_ref.shape[0]
        x = x_ref[...]
        hn_s[0:rows, :] = _norm_mod(x, gpre_ref[...], scale, shift).astype(BF16)
        hn = hn_s[0:rows, :]
        for c in range(n_chunks):
            down = _swiglu_chunk(hn, wg_s, wu_s, wd_s, c, sub)
            if c == 0:
                acc_s[0:rows, :] = down
            else:
                acc_s[0:rows, :] += down
        y = acc_s[0:rows, :]
        o_ref[...] = x + y * _rms(y) * (gpost_ref[...] * (0.5 * (1.0 + gate)))

    group = jnp.minimum(s, n_tiles - 1) // tiles_per_group
    group_mod = lambda j: modp_ref[j, pl.ds(group, 1), :]

    @pl.when(s == 0)
    def _():
        stage_weights()

    @pl.when(s < n_tiles)
    def _():
        half_step(xp_ref, op_ref, group_mod(1), group_mod(0), group_mod(2))

    @pl.when(s == n_tiles)
    def _():
        half_step(xs_ref, os_ref, mods_ref[1], mods_ref[0], mods_ref[2])


def _ffn(xp, xs, mod_p, mod_s, g_pre, g_post, wg, wu, wd, layer, half):
    groups, seq, _ = xp.shape
    n_sample = xs.shape[0]
    rows = FFN_ROWS
    tpg = seq // rows
    n_tiles = groups * tpg
    sub = FFN_SUB
    tile = lambda s: jnp.minimum(s, n_tiles - 1)
    xp_spec = pl.BlockSpec((None, rows, D_MODEL), lambda s: (tile(s) // tpg, tile(s) % tpg, 0))
    xs_spec = pl.BlockSpec((n_sample, D_MODEL), lambda s: (0, 0))
    gain = pl.BlockSpec((None, None, 1, D_MODEL), lambda s: (layer, 2 * half, 0, 0))
    hbm = pl.BlockSpec(memory_space=pl.ANY)
    return pl.pallas_call(
        functools.partial(_ffn_kernel, layer=layer, half=half, n_tiles=n_tiles, tiles_per_group=tpg, sub=sub),
        grid=(n_tiles + 1,),
        in_specs=[
            xp_spec,
            xs_spec,
            pl.BlockSpec((None, 3, groups, D_MODEL), lambda s: (layer, 2 * half, 0, 0)),
            pl.BlockSpec((None, 3, n_sample, D_MODEL), lambda s: (layer, 2 * half, 0, 0)),
            gain,
            gain,
            hbm,
            hbm,
            hbm,
        ],
        out_specs=[xp_spec, xs_spec],
        out_shape=[jax.ShapeDtypeStruct(xp.shape, F32), jax.ShapeDtypeStruct(xs.shape, F32)],
        scratch_shapes=[
            pltpu.VMEM((D_MODEL, D_FF), BF16),
            pltpu.VMEM((D_MODEL, D_FF), BF16),
            pltpu.VMEM((D_FF, D_MODEL), BF16),
            pltpu.VMEM((2, 2, D_MODEL, sub), F32),
            pltpu.VMEM((2, sub, D_MODEL), F32),
            pltpu.SemaphoreType.DMA((2, 3)),
            pltpu.VMEM((max(rows, n_sample), D_MODEL), BF16),
            pltpu.VMEM((max(rows, n_sample), D_MODEL), F32),
        ],
        compiler_params=_params("arbitrary"),
        name="ffn",
    )(xp, xs, mod_p, mod_s, g_pre, g_post, wg, wu, wd)


def _block_diag(a, b):
    za = jnp.zeros((a.shape[0], b.shape[1]), a.dtype)
    zb = jnp.zeros((b.shape[0], a.shape[1]), a.dtype)
    return jnp.concatenate([jnp.concatenate([a, za], axis=1), jnp.concatenate([zb, b], axis=1)], axis=0)


def _mlstm_kernel(x_ref, mod_ref, gpre_ref, gpost_ref, wt_ref, bif_ref, ght_ref, wout_ref,
                  o_ref, c_out_ref, n_out_ref, m_out_ref,
                  st_s, m_s, yt_s, *, chunk):
    t = pl.program_id(1)

    @pl.when(t == 0)
    def _():
        st_s[...] = jnp.zeros_like(st_s)
        m_s[...] = jnp.zeros_like(m_s)

    x = x_ref[...]
    rows = x.shape[0]
    h_in = _norm_mod(x, gpre_ref[...], _group_mod(mod_ref, 1), _group_mod(mod_ref, 0)).astype(BF16)
    proj_q = _dot_nt(wt_ref[0:D_HQ], h_in)
    proj_t = _dot_nt(wt_ref[2 * D_HQ:], h_in)
    k_all = _dot_nt(h_in, wt_ref[D_HQ:2 * D_HQ])
    r_v, r_o, r_i, r_f = 0, D_HV, 2 * D_HV, 2 * D_HV + N_HEADS

    s_idx = lax.broadcasted_iota(jnp.int32, (chunk, chunk), 0)
    t_idx = lax.broadcasted_iota(jnp.int32, (chunk, chunk), 1)
    causal = s_idx <= t_idx
    tri = jnp.where(causal, 1.0, 0.0).astype(BF16)
    lane = lax.broadcasted_iota(jnp.int32, (1, LANES), 1)
    low = lane < D_QK
    ones_rows = jnp.ones((BF16_ROWS, 2 * chunk), BF16)
    zero_rows = jnp.zeros((LANES - N_HEADS, chunk), F32)

    ig_all = GATE_SOFTCAP * jnp.tanh((proj_t[r_i:r_i + N_HEADS, :] + bif_ref[0:N_HEADS, :]) / GATE_SOFTCAP)
    lf_all = _log_sigmoid(proj_t[r_f:r_f + N_HEADS, :] + bif_ref[N_HEADS:, :])

    for c in range(rows // chunk):
        cs = slice(c * chunk, (c + 1) * chunk)
        lf = lf_all[:, cs]
        lf_hi = lf.astype(BF16).astype(F32)
        lf_mid = (lf - lf_hi).astype(BF16).astype(F32)
        lf_lo = lf - lf_hi - lf_mid
        b3 = _dot(jnp.concatenate([lf_hi, lf_mid, lf_lo, jnp.zeros_like(lf)], axis=0).astype(BF16), tri)
        b = b3[0:N_HEADS] + b3[N_HEADS:2 * N_HEADS] + b3[2 * N_HEADS:3 * N_HEADS]
        col = ig_all[:, cs] - b
        b_last = jnp.broadcast_to(b[:, chunk - 1:chunk], b.shape)
        col_s = jnp.concatenate([col, zero_rows], axis=0).T

        for p in range(N_HEADS // 2):
            h0, h1 = 2 * p, 2 * p + 1
            qt = [(proj_q[h * D_QK:(h + 1) * D_QK, cs] * (D_QK ** -0.5)).astype(BF16) for h in (h0, h1)]
            vt = [proj_t[r_v + h * D_V:r_v + (h + 1) * D_V, cs] for h in (h0, h1)]
            k_pair = k_all[cs, p * LANES:(p + 1) * LANES]
            state = st_s[p]
            lhs1 = jnp.concatenate([k_pair.astype(BF16), state.astype(BF16)], axis=0)
            r1 = _dot(lhs1, _block_diag(qt[0], qt[1]))
            probs, w_state, inv_floor, w_k, w_decay = [], [], [], [], []
            for i, h in enumerate((h0, h1)):
                m_prev = m_s[h:h + 1, :]
                col_m = jnp.where(causal, col_s[:, h:h + 1], -jnp.inf)
                g = jnp.maximum(m_prev, jnp.max(col_m, axis=0, keepdims=True))
                probs.append((jnp.exp(col_m - g) * r1[:chunk, i * chunk:(i + 1) * chunk]).astype(BF16))
                m_t = b[h:h + 1, :] + g
                m_new = jnp.broadcast_to(m_t[:, chunk - 1:chunk], m_t.shape)
                w_state.append(jnp.exp(m_prev - g))
                inv_floor.append(jnp.exp(-m_t))
                w_decay.append(jnp.exp(b_last[h:h + 1, :] + m_prev - m_new))
                w_k.append(jnp.exp(b_last[h:h + 1, :] - m_new + col[h:h + 1, :]))
                m_s[h:h + 1, :] = m_new
            lhs2 = jnp.concatenate([jnp.concatenate([vt[0], vt[1]], axis=1).astype(BF16), ones_rows], axis=0)
            r2 = _dot(lhs2, _block_diag(probs[0], probs[1]))
            for i, h in enumerate((h0, h1)):
                ls = slice(i * chunk, (i + 1) * chunk)
                den = w_state[i] * r1[chunk + D_V:chunk + D_V + 1, ls] + r2[D_V:D_V + 1, ls]
                inv = 1.0 / jnp.maximum(jnp.abs(den), inv_floor[i])
                ht = (w_state[i] * r1[chunk:chunk + D_V, ls] + r2[:D_V, ls]) * inv
                rn = lax.rsqrt(jnp.mean(ht * ht, axis=0, keepdims=True) + EPS)
                og = proj_t[r_o + h * D_V:r_o + (h + 1) * D_V, cs]
                yt_s[h * D_V:(h + 1) * D_V, cs] = (_sigmoid(og) * (ht * rn * ght_ref[h])).astype(BF16)
            lhs3 = jnp.concatenate(
                [jnp.concatenate([vt[i] * w_k[i] for i in range(2)], axis=1),
                 jnp.concatenate([jnp.broadcast_to(w_k[i], (BF16_ROWS, chunk)) for i in range(2)], axis=1)],
                axis=0).astype(BF16)
            rhs3 = jnp.concatenate([jnp.where(low, k_pair, 0.0), jnp.where(low, 0.0, k_pair)], axis=0).astype(BF16)
            decay = jnp.where(low, w_decay[0], w_decay[1])
            st_s[p] = state * decay + _dot(lhs3, rhs3)

    y = _dot_tn(yt_s[...], wout_ref[...])
    o_ref[...] = x + y * _rms(y) * (gpost_ref[...] * (1.0 + _group_mod(mod_ref, 2)))

    @pl.when(t == pl.num_programs(1) - 1)
    def _():
        for p in range(N_HEADS // 2):
            state = st_s[p]
            c_out_ref[p * LANES:(p + 1) * LANES, :] = state[:D_V, :].T
            n_out_ref[:, p * LANES:(p + 1) * LANES] = state[D_V:D_V + 1, :]
        m_out_ref[...] = m_s[...]


def _mixer_mod_specs(layer, batch):
    gain = pl.BlockSpec((None, None, 1, D_MODEL), lambda b, t: (layer, 1, 0, 0))
    return [pl.BlockSpec((None, 3, batch, D_MODEL), lambda b, t: (layer, 1, 0, 0)), gain, gain]


def _mlstm_prompt(x, mod, g_pre, g_post, layer, w_t, b_if, g_head_t, w_out):
    batch, seq, _ = x.shape
    rows = MIX_ROWS
    return pl.pallas_call(
        functools.partial(_mlstm_kernel, chunk=ML_CHUNK),
        grid=(batch, seq // rows),
        in_specs=[
            pl.BlockSpec((None, rows, D_MODEL), lambda b, t: (b, t, 0)),
            *_mixer_mod_specs(layer, batch),
            _resident(w_t.shape),
            _resident(b_if.shape),
            _resident(g_head_t.shape),
            _resident(w_out.shape),
        ],
        out_specs=[
            pl.BlockSpec((None, rows, D_MODEL), lambda b, t: (b, t, 0)),
            pl.BlockSpec((None, D_HQ, D_V), lambda b, t: (b, 0, 0)),
            pl.BlockSpec((None, 1, D_HQ), lambda b, t: (b, 0, 0)),
            pl.BlockSpec((None, N_HEADS, LANES), lambda b, t: (b, 0, 0)),
        ],
        out_shape=[
            jax.ShapeDtypeStruct(x.shape, F32),
            jax.ShapeDtypeStruct((batch, D_HQ, D_V), F32),
            jax.ShapeDtypeStruct((batch, 1, D_HQ), F32),
            jax.ShapeDtypeStruct((batch, N_HEADS, LANES), F32),
        ],
        scratch_shapes=[
            pltpu.VMEM((N_HEADS // 2, ML_STATE_ROWS, LANES), F32),
            pltpu.VMEM((N_HEADS, LANES), F32),
            pltpu.VMEM((D_HV, rows), BF16),
        ],
        compiler_params=_params("arbitrary", "arbitrary"),
        name="mlstm_prompt",
    )(x, mod, g_pre, g_post, w_t, b_if, g_head_t, w_out)


def _pair_expand(cols):
    lane = lax.broadcasted_iota(jnp.int32, (cols[0].shape[0], LANES), 1)
    return jnp.concatenate(
        [jnp.where(lane < D_QK, cols[2 * p], cols[2 * p + 1]) for p in range(N_HEADS // 2)], axis=1)


def _row_to_head_tile(rows, b):
    return jnp.concatenate([rows[b:b + 1, h * D_V:(h + 1) * D_V] for h in range(N_HEADS)], axis=0)


def _mlstm_step_kernel(x_ref, mod_ref, gpre_ref, gpost_ref, wt_ref, bif_ref, ghead_ref, wout_ref,
                       c_ref, n_ref, m_ref,
                       o_ref, c_out_ref, n_out_ref, m_out_ref,
                       v_s, og_s, a_s, b_s, y_s, qt_s, kwt_s, wst_s):
    i = pl.program_id(0)
    blk = c_ref.shape[0]
    rows = x_ref.shape[0]

    @pl.when(i == 0)
    def _():
        x = x_ref[...]
        h_in = _norm_mod(x, gpre_ref[...], mod_ref[1], mod_ref[0]).astype(BF16)
        proj = _dot_nt(h_in, wt_ref[...])
        gates = proj[:, 2 * D_HQ + 2 * D_HV:] + bif_ref[...]
        ig = (GATE_SOFTCAP * jnp.tanh(gates / GATE_SOFTCAP))[:, :N_HEADS]
        lf = _log_sigmoid(gates)[:, N_HEADS:]
        m_prev = m_ref[...]
        st = lf + m_prev
        m_t = jnp.maximum(st, ig)
        w_i = jnp.exp(ig - m_t)
        w_s = jnp.exp(st - m_t)
        floor = jnp.exp(-m_t)
        m_out_ref[...] = m_t
        q = proj[:, :D_HQ] * (D_QK ** -0.5)
        k = proj[:, D_HQ:2 * D_HQ]
        n_prev = n_ref[...]
        lane = lax.broadcasted_iota(jnp.int32, (rows, LANES), 1)
        lo = lane < D_QK

        def head_sums(z):
            out = []
            for p in range(N_HEADS // 2):
                zp = z[:, p * LANES:(p + 1) * LANES]
                out.append(jnp.sum(jnp.where(lo, zp, 0.0), axis=-1, keepdims=True))
                out.append(jnp.sum(jnp.where(lo, 0.0, zp), axis=-1, keepdims=True))
            return out

        qk_h = head_sums(q * k)
        qn_h = head_sums(q * n_prev)
        ws_cols, wi_cols = [], []
        for hd in range(N_HEADS):
            ws_h = w_s[:, hd:hd + 1]
            wi_h = w_i[:, hd:hd + 1]
            s_h = qk_h[hd] * wi_h
            den = ws_h * qn_h[hd] + s_h
            inv = 1.0 / jnp.maximum(jnp.abs(den), floor[:, hd:hd + 1])
            a_s[:, hd * D_V:(hd + 1) * D_V] = jnp.broadcast_to(ws_h * inv, (rows, D_V))
            b_s[:, hd * D_V:(hd + 1) * D_V] = jnp.broadcast_to(s_h * inv, (rows, D_V))
            ws_cols.append(ws_h)
            wi_cols.append(wi_h)
        ws_x = _pair_expand(ws_cols)
        kw = k * _pair_expand(wi_cols)
        n_out_ref[...] = ws_x * n_prev + kw
        v_s[...] = proj[:, 2 * D_HQ:2 * D_HQ + D_HV]
        og_s[...] = proj[:, 2 * D_HQ + D_HV:2 * D_HQ + 2 * D_HV]
        for dst, src in ((qt_s, q), (kwt_s, kw), (wst_s, ws_x)):
            src_t = src.T
            for j in range(rows // blk):
                dst[j] = src_t[:, j * blk:(j + 1) * blk]

    block_rows = pl.ds(pl.multiple_of(i * blk, blk), blk)
    v_blk, og_blk, a_blk, b_blk = v_s[block_rows, :], og_s[block_rows, :], a_s[block_rows, :], b_s[block_rows, :]
    y_tiles = []
    for b in range(blk):
        v_i = _row_to_head_tile(v_blk, b)
        readout = []
        for hd in range(N_HEADS):
            rs = slice(hd * D_QK, (hd + 1) * D_QK)
            c_h = c_ref[b, hd]
            readout.append(jnp.sum(c_h * qt_s[i, rs, b:b + 1], axis=0, keepdims=True))
            c_out_ref[b, hd] = c_h * wst_s[i, rs, b:b + 1] + kwt_s[i, rs, b:b + 1] * v_i[hd:hd + 1, :]
        hh = _row_to_head_tile(a_blk, b) * jnp.concatenate(readout, axis=0) + _row_to_head_tile(b_blk, b) * v_i
        y_tiles.append(_sigmoid(_row_to_head_tile(og_blk, b)) * (hh * _rms(hh) * ghead_ref[...]))
    y_s[block_rows, :] = jnp.concatenate(
        [jnp.concatenate([y_tiles[b][hd:hd + 1, :] for b in range(blk)], axis=0) for hd in range(N_HEADS)], axis=1)

    @pl.when(i == pl.num_programs(0) - 1)
    def _():
        y = _dot(y_s[...].astype(BF16), wout_ref[...])
        o_ref[...] = x_ref[...] + y * _rms(y) * (gpost_ref[...] * (1.0 + mod_ref[2]))


def _mlstm_step(x, mod, g_pre, g_post, layer, w_t, b_if, g_head, w_out, c0, n0, m0):
    rows = x.shape[0]
    blk = SAMPLE_BLOCK
    nblk = rows // blk
    const = lambda shape: pl.BlockSpec(shape, lambda i: tuple(0 for _ in shape))
    gain = pl.BlockSpec((None, None, 1, D_MODEL), lambda i: (layer, 1, 0, 0))
    c_spec = pl.BlockSpec((blk, N_HEADS, D_QK, D_V), lambda i: (i, 0, 0, 0))
    n_flat = n0.reshape(rows, D_HQ)
    y, c_new, n_new, m_new = pl.pallas_call(
        _mlstm_step_kernel,
        grid=(nblk,),
        in_specs=[
            const(x.shape),
            pl.BlockSpec((None, 3, rows, D_MODEL), lambda i: (layer, 1, 0, 0)),
            gain,
            gain,
            _resident(w_t.shape),
            const(b_if.shape),
            const(g_head.shape),
            _resident(w_out.shape),
            c_spec,
            const(n_flat.shape),
            const(m0.shape),
        ],
        out_specs=[const(x.shape), c_spec, const(n_flat.shape), const(m0.shape)],
        out_shape=[jax.ShapeDtypeStruct(x.shape, F32), jax.ShapeDtypeStruct(c0.shape, F32),
                   jax.ShapeDtypeStruct(n_flat.shape, F32), jax.ShapeDtypeStruct(m0.shape, F32)],
        scratch_shapes=[pltpu.VMEM((rows, D_HV), F32)] * 5 + [pltpu.VMEM((nblk, D_HQ, blk), F32)] * 3,
        compiler_params=_params("arbitrary"),
        name="mlstm_step",
    )(x, mod, g_pre, g_post, w_t, b_if, g_head, w_out, c0, n_flat, m0)
    return y, c_new, n_new.reshape(rows, N_HEADS, D_QK), m_new


def _conv_kernel(x_ref, mod_ref, gpre_ref, gpost_ref, win_ref, cw_ref, wout_ref, o_ref, buf_out_ref, tail_s):
    t = pl.program_id(1)

    @pl.when(t == 0)
    def _():
        tail_s[...] = jnp.zeros_like(tail_s)

    x = x_ref[...]
    rows = x.shape[0]
    h_in = _norm_mod(x, gpre_ref[...], _group_mod(mod_ref, 1), _group_mod(mod_ref, 0)).astype(BF16)
    p = _dot(h_in, win_ref[...])
    bg = p[:, :D_MODEL]
    u = p[:, D_MODEL:2 * D_MODEL] * p[:, 2 * D_MODEL:]
    prev1 = tail_s[7:8, :]
    prev2 = tail_s[6:7, :]
    ridx = lax.broadcasted_iota(jnp.int32, u.shape, 0)
    u1 = jnp.where(ridx == 0, prev1, pltpu.roll(u, 1, 0))
    u2 = jnp.where(ridx == 0, prev2, jnp.where(ridx == 1, prev1, pltpu.roll(u, 2, 0)))
    conv = cw_ref[0:1, :] * u2 + cw_ref[1:2, :] * u1 + cw_ref[2:3, :] * u
    tail_s[...] = u[rows - 8:, :]
    y = _dot((bg * conv).astype(BF16), wout_ref[...])
    o_ref[...] = x + y * _rms(y) * (gpost_ref[...] * (1.0 + _group_mod(mod_ref, 2)))

    @pl.when(t == pl.num_programs(1) - 1)
    def _():
        buf_out_ref[...] = u[rows - (CONV_W - 1):, :]


def _conv_prompt(x, mod, g_pre, g_post, layer, w_in, conv_w, w_out):
    batch, seq, _ = x.shape
    rows = MIX_ROWS
    return pl.pallas_call(
        _conv_kernel,
        grid=(batch, seq // rows),
        in_specs=[
            pl.BlockSpec((None, rows, D_MODEL), lambda b, t: (b, t, 0)),
            *_mixer_mod_specs(layer, batch),
            _resident(w_in.shape),
            _resident(conv_w.shape),
            _resident(w_out.shape),
        ],
        out_specs=[
            pl.BlockSpec((None, rows, D_MODEL), lambda b, t: (b, t, 0)),
            pl.BlockSpec((None, CONV_W - 1, D_MODEL), lambda b, t: (b, 0, 0)),
        ],
        out_shape=[jax.ShapeDtypeStruct(x.shape, F32),
                   jax.ShapeDtypeStruct((batch, CONV_W - 1, D_MODEL), F32)],
        scratch_shapes=[pltpu.VMEM((8, D_MODEL), F32)],
        compiler_params=_params("arbitrary", "arbitrary"),
        name="conv_prompt",
    )(x, mod, g_pre, g_post, w_in, conv_w, w_out)


def _conv_sample_kernel(x_ref, mod_ref, gpre_ref, gpost_ref, win_ref, cw_ref, wout_ref, buf_ref,
                        o_ref, buf_out_ref):
    x = x_ref[...]
    h_in = _norm_mod(x, gpre_ref[...], mod_ref[1], mod_ref[0]).astype(BF16)
    p = _dot(h_in, win_ref[...])
    bg = p[:, :D_MODEL]
    u = p[:, D_MODEL:2 * D_MODEL] * p[:, 2 * D_MODEL:]
    conv = cw_ref[0:1, :] * buf_ref[0] + cw_ref[1:2, :] * buf_ref[1] + cw_ref[2:3, :] * u
    y = _dot((bg * conv).astype(BF16), wout_ref[...])
    o_ref[...] = x + (1.0 + mod_ref[2]) * (y * _rms(y) * gpost_ref[...])
    buf_out_ref[0] = buf_ref[1]
    buf_out_ref[1] = u


def _conv_sample(x, mod, g_pre, g_post, w_in, conv_w, w_out, buf):
    full = lambda shape: pl.BlockSpec(shape, lambda: tuple(0 for _ in shape))
    ins = (x, mod, g_pre.reshape(1, D_MODEL), g_post.reshape(1, D_MODEL), w_in, conv_w, w_out, buf)
    return pl.pallas_call(
        _conv_sample_kernel,
        in_specs=[full(z.shape) for z in ins],
        out_specs=[full(x.shape), full(buf.shape)],
        out_shape=[jax.ShapeDtypeStruct(x.shape, F32), jax.ShapeDtypeStruct(buf.shape, F32)],
        compiler_params=_params(),
        name="conv_sample",
    )(*ins)


def kernel(x_prompt, x_sample, c_prompt, c_sample, state_mlstm_C, state_mlstm_n, state_mlstm_m, state_conv,
           w_ada, b_ada, g_pre, g_post, ffn_wg, ffn_wu, ffn_wd,
           ml_w_in, ml_b_i, ml_b_f, ml_g_head, ml_w_out, cv_w_in, cv_conv_w, cv_w_out):
    depth = w_ada.shape[0]
    batch, seq, _ = x_prompt.shape
    n_sample = x_sample.shape[0]
    n_ml = ml_w_in.shape[0]
    assert x_sample.shape[1] == 1 and seq % ML_CHUNK == 0

    mod_p, mod_s = _ada(c_prompt, c_sample, w_ada, b_ada)
    gp = g_pre.reshape(depth, 3, 1, D_MODEL)
    gq = g_post.reshape(depth, 3, 1, D_MODEL)

    ml_t = ml_w_in.transpose(0, 2, 1).astype(BF16)
    ml_bif = jnp.concatenate([ml_b_i, ml_b_f], axis=-1)
    ml_bif_t = jnp.broadcast_to(ml_bif[:, :, None], (n_ml, 2 * N_HEADS, MIX_ROWS))
    ml_ghead_t = jnp.broadcast_to(ml_g_head[:, :, :, None], (n_ml, N_HEADS, D_V, LANES))
    ml_out = ml_w_out.astype(BF16)
    cv_in = cv_w_in.astype(BF16)
    cv_out = cv_w_out.astype(BF16)

    xp = x_prompt
    xs = x_sample.reshape(n_sample, D_MODEL)
    p_c, p_n, p_m, p_buf, s_c, s_n, s_m, s_buf = [], [], [], [], [], [], [], []
    for l in range(depth):
        ms = lambda s: mod_s[l, 3 * s:3 * s + 3]
        xp, xs = _ffn(xp, xs, mod_p, mod_s, gp, gq, ffn_wg, ffn_wu, ffn_wd, l, 0)
        j = l // 2
        if l % 2 == 0:
            xp, c_j, n_j, m_j = _mlstm_prompt(xp, mod_p, gp, gq, l, ml_t[j], ml_bif_t[j], ml_ghead_t[j], ml_out[j])
            p_c.append(c_j.reshape(batch, N_HEADS, D_QK, D_V))
            p_n.append(n_j.reshape(batch, N_HEADS, D_QK))
            p_m.append(m_j[:, :, 0])
            ys, c_j, n_j, m_j = _mlstm_step(xs, mod_s, gp, gq, l, ml_t[j], ml_bif[j][None], ml_g_head[j], ml_out[j],
                                            state_mlstm_C[j], state_mlstm_n[j], state_mlstm_m[j])
            xs = ys
            s_c.append(c_j)
            s_n.append(n_j)
            s_m.append(m_j)
        else:
            xp, buf_j = _conv_prompt(xp, mod_p, gp, gq, l, cv_in[j], cv_conv_w[j], cv_out[j])
            p_buf.append(buf_j)
            ys, buf_j = _conv_sample(xs, ms(1), g_pre[l, 1], g_post[l, 1], cv_in[j], cv_conv_w[j], cv_out[j],
                                     state_conv[j].transpose(1, 0, 2))
            xs = ys
            s_buf.append(buf_j.transpose(1, 0, 2))
        xp, xs = _ffn(xp, xs, mod_p, mod_s, gp, gq, ffn_wg, ffn_wu, ffn_wd, l, 1)

    return (xp, xs.reshape(n_sample, 1, D_MODEL),
            jnp.stack(p_c), jnp.stack(p_n), jnp.stack(p_m), jnp.stack(p_buf),
            jnp.stack(s_c), jnp.stack(s_n), jnp.stack(s_m), jnp.stack(s_buf))
```

```python
import functools

import jax
import jax.numpy as jnp
from jax import lax
from jax.experimental import pallas as pl
from jax.experimental.pallas import tpu as pltpu

F32 = jnp.float32
BF16 = jnp.bfloat16

D_MODEL = 1024
N_HEADS = 8
D_QK = 64
D_V = 128
D_HQ = N_HEADS * D_QK
D_HV = N_HEADS * D_V
D_FF = 2816
N_ADA = 9
CONV_W = 3
GATE_SOFTCAP = 15.0
EPS = 1e-6

LANES = 128
BF16_ROWS = 16
MXU_COLS = 256
VMEM_LIMIT_BYTES = 56 * 1024 * 1024

FFN_ROWS = 512
FFN_SUB = MXU_COLS
FFN_STAGE_PIECES = 8
ADA_COLS = 3 * D_MODEL
MIX_ROWS = 1024
ML_CHUNK = LANES
ML_STATE_ROWS = D_V + BF16_ROWS
SAMPLE_BLOCK = 8


def _params(*sem):
    return pltpu.CompilerParams(dimension_semantics=sem, vmem_limit_bytes=VMEM_LIMIT_BYTES)


def _sigmoid(x):
    return 1.0 / (1.0 + jnp.exp(-x))


def _rms(x):
    return lax.rsqrt(jnp.mean(x * x, axis=-1, keepdims=True) + EPS)


def _norm_mod(x, gain, scale, shift):
    return x * _rms(x) * (gain * (1.0 + scale)) + shift


def _dot(a, b):
    return jnp.dot(a, b, preferred_element_type=F32)


def _dot_nt(a, b):
    return lax.dot_general(a, b, (((1,), (1,)), ((), ())), preferred_element_type=F32)


def _dot_tn(a, b):
    return lax.dot_general(a, b, (((0,), (0,)), ((), ())), preferred_element_type=F32)


def _log_sigmoid(x):
    return -(jnp.maximum(-x, 0.0) + jnp.log1p(jnp.exp(-jnp.abs(x))))


def _resident(shape):
    return pl.BlockSpec(shape, lambda *_: tuple(0 for _ in shape), pipeline_mode=pl.Buffered(1))


def _ada_kernel(cp_ref, cs_ref, w_ref, b_ref, op_ref, os_ref):
    w = w_ref[...].astype(BF16)
    for c_ref, o_ref in ((cp_ref, op_ref), (cs_ref, os_ref)):
        c = c_ref[...]
        res = _dot((c * _sigmoid(c)).astype(BF16), w)
        for k in range(o_ref.shape[0]):
            o_ref[k] = res[:, k * D_MODEL:(k + 1) * D_MODEL] + b_ref[k]


def _ada(c_prompt, c_sample, w_ada, b_ada):
    depth = w_ada.shape[0]
    per_step = ADA_COLS // D_MODEL
    out = lambda c: (pl.BlockSpec((None, per_step, c.shape[0], D_MODEL), lambda l, j: (l, j, 0, 0)),
                     jax.ShapeDtypeStruct((depth, N_ADA, c.shape[0], D_MODEL), F32))
    (spec_p, shape_p), (spec_s, shape_s) = out(c_prompt), out(c_sample)
    return pl.pallas_call(
        _ada_kernel,
        grid=(depth, N_ADA // per_step),
        in_specs=[
            pl.BlockSpec(c_prompt.shape, lambda l, j: (0, 0)),
            pl.BlockSpec(c_sample.shape, lambda l, j: (0, 0)),
            pl.BlockSpec((None, D_MODEL, ADA_COLS), lambda l, j: (l, 0, j)),
            pl.BlockSpec((None, per_step, 1, D_MODEL), lambda l, j: (l, j, 0, 0)),
        ],
        out_specs=[spec_p, spec_s],
        out_shape=[shape_p, shape_s],
        compiler_params=_params("arbitrary", "arbitrary"),
        name="ada",
    )(c_prompt, c_sample, w_ada, b_ada.reshape(depth, N_ADA, 1, D_MODEL))


def _group_mod(mod_ref, j):
    return mod_ref[j, pl.ds(pl.program_id(0), 1), :]


def _swiglu_chunk(hn, wg_ref, wu_ref, wd_ref, c, sub):
    cols = slice(c * sub, (c + 1) * sub)
    g = _dot(hn, wg_ref[:, cols])
    u = _dot(hn, wu_ref[:, cols])
    return _dot((g * _sigmoid(g) * u).astype(BF16), wd_ref[cols, :])


def _ffn_kernel(xp_ref, xs_ref, modp_ref, mods_ref, gpre_ref, gpost_ref, wg_hbm, wu_hbm, wd_hbm,
                op_ref, os_ref,
                wg_s, wu_s, wd_s, up_stage, dn_stage, sem, hn_s, acc_s, *,
                layer, half, n_tiles, tiles_per_group, sub):
    s = pl.program_id(0)
    n_chunks = wg_s.shape[1] // sub

    up_rows = up_stage.shape[2]
    dn_rows = dn_stage.shape[1]
    n_pieces = wg_s.shape[0] // up_rows

    def piece_copies(c):
        slot = c % 2
        up = pl.ds(c * up_rows, up_rows)
        dn = pl.ds(c * dn_rows, dn_rows)
        return (pltpu.make_async_copy(wg_hbm.at[layer, half, up, :], up_stage.at[slot, 0], sem.at[slot, 0]),
                pltpu.make_async_copy(wu_hbm.at[layer, half, up, :], up_stage.at[slot, 1], sem.at[slot, 1]),
                pltpu.make_async_copy(wd_hbm.at[layer, half, dn, :], dn_stage.at[slot], sem.at[slot, 2]))

    def stage_weights():
        for copy in piece_copies(0):
            copy.start()
        for c in range(n_pieces):
            if c + 1 < n_pieces:
                for copy in piece_copies(c + 1):
                    copy.start()
            for copy in piece_copies(c):
                copy.wait()
            wg_s[c * up_rows:(c + 1) * up_rows, :] = up_stage[c % 2, 0].astype(BF16)
            wu_s[c * up_rows:(c + 1) * up_rows, :] = up_stage[c % 2, 1].astype(BF16)
            wd_s[c * dn_rows:(c + 1) * dn_rows, :] = dn_stage[c % 2].astype(BF16)

    def half_step(x_ref, o_ref, scale, shift, gate):
        rows = x_ref.shape[0]
        x = x_ref[...]
        hn_s[0:rows, :] = _norm_mod(x, gpre_ref[...], scale, shift).astype(BF16)
        hn = hn_s[0:rows, :]
        for c in range(n_chunks):
            down = _swiglu_chunk(hn, wg_s, wu_s, wd_s, c, sub)
            if c == 0:
                acc_s[0:rows, :] = down
            else:
                acc_s[0:rows, :] += down
        y = acc_s[0:rows, :]
        o_ref[...] = x + y * _rms(y) * (gpost_ref[...] * (0.5 * (1.0 + gate)))

    group = jnp.minimum(s, n_tiles - 1) // tiles_per_group
    group_mod = lambda j: modp_ref[j, pl.ds(group, 1), :]

    @pl.when(s == 0)
    def _():
        stage_weights()

    @pl.when(s < n_tiles)
    def _():
        half_step(xp_ref, op_ref, group_mod(1), group_mod(0), group_mod(2))

    @pl.when(s == n_tiles)
    def _():
        half_step(xs_ref, os_ref, mods_ref[1], mods_ref[0], mods_ref[2])


def _ffn(xp, xs, mod_p, mod_s, g_pre, g_post, wg, wu, wd, layer, half):
    groups, seq, _ = xp.shape
    n_sample = xs.shape[0]
    rows = FFN_ROWS
    tpg = seq // rows
    n_tiles = groups * tpg
    sub = FFN_SUB
    tile = lambda s: jnp.minimum(s, n_tiles - 1)
    xp_spec = pl.BlockSpec((None, rows, D_MODEL), lambda s: (tile(s) // tpg, tile(s) % tpg, 0))
    xs_spec = pl.BlockSpec((n_sample, D_MODEL), lambda s: (0, 0))
    gain = pl.BlockSpec((None, None, 1, D_MODEL), lambda s: (layer, 2 * half, 0, 0))
    hbm = pl.BlockSpec(memory_space=pl.ANY)
    return pl.pallas_call(
        functools.partial(_ffn_kernel, layer=layer, half=half, n_tiles=n_tiles, tiles_per_group=tpg, sub=sub),
        grid=(n_tiles + 1,),
        in_specs=[
            xp_spec,
            xs_spec,
            pl.BlockSpec((None, 3, groups, D_MODEL), lambda s: (layer, 2 * half, 0, 0)),
            pl.BlockSpec((None, 3, n_sample, D_MODEL), lambda s: (layer, 2 * half, 0, 0)),
            gain,
            gain,
            hbm,
            hbm,
            hbm,
        ],
        out_specs=[xp_spec, xs_spec],
        out_shape=[jax.ShapeDtypeStruct(xp.shape, F32), jax.ShapeDtypeStruct(xs.shape, F32)],
        scratch_shapes=[
            pltpu.VMEM((D_MODEL, D_FF), BF16),
            pltpu.VMEM((D_MODEL, D_FF), BF16),
            pltpu.VMEM((D_FF, D_MODEL), BF16),
            pltpu.VMEM((2, 2, D_MODEL // FFN_STAGE_PIECES, D_FF), F32),
            pltpu.VMEM((2, D_FF // FFN_STAGE_PIECES, D_MODEL), F32),
            pltpu.SemaphoreType.DMA((2, 3)),
            pltpu.VMEM((max(rows, n_sample), D_MODEL), BF16),
            pltpu.VMEM((max(rows, n_sample), D_MODEL), F32),
        ],
        compiler_params=_params("arbitrary"),
        name="ffn",
    )(xp, xs, mod_p, mod_s, g_pre, g_post, wg, wu, wd)


def _block_diag(a, b):
    za = jnp.zeros((a.shape[0], b.shape[1]), a.dtype)
    zb = jnp.zeros((b.shape[0], a.shape[1]), a.dtype)
    return jnp.concatenate([jnp.concatenate([a, za], axis=1), jnp.concatenate([zb, b], axis=1)], axis=0)


def _mlstm_kernel(x_ref, mod_ref, gpre_ref, gpost_ref, wt_ref, bif_ref, ght_ref, wout_ref,
                  o_ref, c_out_ref, n_out_ref, m_out_ref,
                  st_s, m_s, yt_s, *, chunk):
    t = pl.program_id(1)

    @pl.when(t == 0)
    def _():
        st_s[...] = jnp.zeros_like(st_s)
        m_s[...] = jnp.zeros_like(m_s)

    x = x_ref[...]
    rows = x.shape[0]
    h_in = _norm_mod(x, gpre_ref[...], _group_mod(mod_ref, 1), _group_mod(mod_ref, 0)).astype(BF16)
    proj_q = _dot_nt(wt_ref[0:D_HQ], h_in)
    proj_t = _dot_nt(wt_ref[2 * D_HQ:], h_in)
    k_all = _dot_nt(h_in, wt_ref[D_HQ:2 * D_HQ])
    r_v, r_o, r_i, r_f = 0, D_HV, 2 * D_HV, 2 * D_HV + N_HEADS

    s_idx = lax.broadcasted_iota(jnp.int32, (chunk, chunk), 0)
    t_idx = lax.broadcasted_iota(jnp.int32, (chunk, chunk), 1)
    causal = s_idx <= t_idx
    tri = jnp.where(causal, 1.0, 0.0).astype(BF16)
    lane = lax.broadcasted_iota(jnp.int32, (1, LANES), 1)
    low = lane < D_QK
    ones_rows = jnp.ones((BF16_ROWS, 2 * chunk), BF16)
    zero_rows = jnp.zeros((LANES - N_HEADS, chunk), F32)

    ig_all = GATE_SOFTCAP * jnp.tanh((proj_t[r_i:r_i + N_HEADS, :] + bif_ref[0:N_HEADS, :]) / GATE_SOFTCAP)
    lf_all = _log_sigmoid(proj_t[r_f:r_f + N_HEADS, :] + bif_ref[N_HEADS:, :])

    for c in range(rows // chunk):
        cs = slice(c * chunk, (c + 1) * chunk)
        lf = lf_all[:, cs]
        lf_hi = lf.astype(BF16).astype(F32)
        lf_mid = (lf - lf_hi).astype(BF16).astype(F32)
        lf_lo = lf - lf_hi - lf_mid
        b3 = _dot(jnp.concatenate([lf_hi, lf_mid, lf_lo, jnp.zeros_like(lf)], axis=0).astype(BF16), tri)
        b = b3[0:N_HEADS] + b3[N_HEADS:2 * N_HEADS] + b3[2 * N_HEADS:3 * N_HEADS]
        col = ig_all[:, cs] - b
        b_last = jnp.broadcast_to(b[:, chunk - 1:chunk], b.shape)
        col_s = jnp.concatenate([col, zero_rows], axis=0).T

        for p in range(N_HEADS // 2):
            h0, h1 = 2 * p, 2 * p + 1
            qt = [(proj_q[h * D_QK:(h + 1) * D_QK, cs] * (D_QK ** -0.5)).astype(BF16) for h in (h0, h1)]
            vt = [proj_t[r_v + h * D_V:r_v + (h + 1) * D_V, cs] for h in (h0, h1)]
            k_pair = k_all[cs, p * LANES:(p + 1) * LANES]
            state = st_s[p]
            lhs1 = jnp.concatenate([k_pair.astype(BF16), state.astype(BF16)], axis=0)
            r1 = _dot(lhs1, _block_diag(qt[0], qt[1]))
            probs, w_state, inv_floor, w_k, w_decay = [], [], [], [], []
            for i, h in enumerate((h0, h1)):
                m_prev = m_s[h:h + 1, :]
                col_m = jnp.where(causal, col_s[:, h:h + 1], -jnp.inf)
                g = jnp.maximum(m_prev, jnp.max(col_m, axis=0, keepdims=True))
                probs.append((jnp.exp(col_m - g) * r1[:chunk, i * chunk:(i + 1) * chunk]).astype(BF16))
                m_t = b[h:h + 1, :] + g
                m_new = jnp.broadcast_to(m_t[:, chunk - 1:chunk], m_t.shape)
                w_state.append(jnp.exp(m_prev - g))
                inv_floor.append(jnp.exp(-m_t))
                w_decay.append(jnp.exp(b_last[h:h + 1, :] + m_prev - m_new))
                w_k.append(jnp.exp(b_last[h:h + 1, :] - m_new + col[h:h + 1, :]))
                m_s[h:h + 1, :] = m_new
            lhs2 = jnp.concatenate([jnp.concatenate([vt[0], vt[1]], axis=1).astype(BF16), ones_rows], axis=0)
            r2 = _dot(lhs2, _block_diag(probs[0], probs[1]))
            for i, h in enumerate((h0, h1)):
                ls = slice(i * chunk, (i + 1) * chunk)
                den = w_state[i] * r1[chunk + D_V:chunk + D_V + 1, ls] + r2[D_V:D_V + 1, ls]
                inv = 1.0 / jnp.maximum(jnp.abs(den), inv_floor[i])
                ht = (w_state[i] * r1[chunk:chunk + D_V, ls] + r2[:D_V, ls]) * inv
                rn = lax.rsqrt(jnp.mean(ht * ht, axis=0, keepdims=True) + EPS)
                og = proj_t[r_o + h * D_V:r_o + (h + 1) * D_V, cs]
                yt_s[h * D_V:(h + 1) * D_V, cs] = (_sigmoid(og) * (ht * rn * ght_ref[h])).astype(BF16)
            lhs3 = jnp.concatenate(
                [jnp.concatenate([vt[i] * w_k[i] for i in range(2)], axis=1),
                 jnp.concatenate([jnp.broadcast_to(w_k[i], (BF16_ROWS, chunk)) for i in range(2)], axis=1)],
                axis=0).astype(BF16)
            rhs3 = jnp.concatenate([jnp.where(low, k_pair, 0.0), jnp.where(low, 0.0, k_pair)], axis=0).astype(BF16)
            decay = jnp.where(low, w_decay[0], w_decay[1])
            st_s[p] = state * decay + _dot(lhs3, rhs3)

    y = _dot_tn(yt_s[...], wout_ref[...])
    o_ref[...] = x + y * _rms(y) * (gpost_ref[...] * (1.0 + _group_mod(mod_ref, 2)))

    @pl.when(t == pl.num_programs(1) - 1)
    def _():
        for p in range(N_HEADS // 2):
            state = st_s[p]
            c_out_ref[p * LANES:(p + 1) * LANES, :] = state[:D_V, :].T
            n_out_ref[:, p * LANES:(p + 1) * LANES] = state[D_V:D_V + 1, :]
        m_out_ref[...] = m_s[...]


def _mixer_mod_specs(layer, batch):
    gain = pl.BlockSpec((None, None, 1, D_MODEL), lambda b, t: (layer, 1, 0, 0))
    return [pl.BlockSpec((None, 3, batch, D_MODEL), lambda b, t: (layer, 1, 0, 0)), gain, gain]


def _mlstm_prompt(x, mod, g_pre, g_post, layer, w_t, b_if, g_head_t, w_out):
    batch, seq, _ = x.shape
    rows = MIX_ROWS
    return pl.pallas_call(
        functools.partial(_mlstm_kernel, chunk=ML_CHUNK),
        grid=(batch, seq // rows),
        in_specs=[
            pl.BlockSpec((None, rows, D_MODEL), lambda b, t: (b, t, 0)),
            *_mixer_mod_specs(layer, batch),
            _resident(w_t.shape),
            _resident(b_if.shape),
            _resident(g_head_t.shape),
            _resident(w_out.shape),
        ],
        out_specs=[
            pl.BlockSpec((None, rows, D_MODEL), lambda b, t: (b, t, 0)),
            pl.BlockSpec((None, D_HQ, D_V), lambda b, t: (b, 0, 0)),
            pl.BlockSpec((None, 1, D_HQ), lambda b, t: (b, 0, 0)),
            pl.BlockSpec((None, N_HEADS, LANES), lambda b, t: (b, 0, 0)),
        ],
        out_shape=[
            jax.ShapeDtypeStruct(x.shape, F32),
            jax.ShapeDtypeStruct((batch, D_HQ, D_V), F32),
            jax.ShapeDtypeStruct((batch, 1, D_HQ), F32),
            jax.ShapeDtypeStruct((batch, N_HEADS, LANES), F32),
        ],
        scratch_shapes=[
            pltpu.VMEM((N_HEADS // 2, ML_STATE_ROWS, LANES), F32),
            pltpu.VMEM((N_HEADS, LANES), F32),
            pltpu.VMEM((D_HV, rows), BF16),
        ],
        compiler_params=_params("arbitrary", "arbitrary"),
        name="mlstm_prompt",
    )(x, mod, g_pre, g_post, w_t, b_if, g_head_t, w_out)


def _pair_expand(cols):
    lane = lax.broadcasted_iota(jnp.int32, (cols[0].shape[0], LANES), 1)
    return jnp.concatenate(
        [jnp.where(lane < D_QK, cols[2 * p], cols[2 * p + 1]) for p in range(N_HEADS // 2)], axis=1)


def _row_to_head_tile(rows, b):
    return jnp.concatenate([rows[b:b + 1, h * D_V:(h + 1) * D_V] for h in range(N_HEADS)], axis=0)


def _mlstm_step_kernel(x_ref, mod_ref, gpre_ref, gpost_ref, wt_ref, bif_ref, ghead_ref, wout_ref,
                       c_ref, n_ref, m_ref,
                       o_ref, c_out_ref, n_out_ref, m_out_ref,
                       v_s, og_s, a_s, b_s, y_s, qt_s, kwt_s, wst_s):
    i = pl.program_id(0)
    blk = c_ref.shape[0]
    rows = x_ref.shape[0]

    @pl.when(i == 0)
    def _():
        x = x_ref[...]
        h_in = _norm_mod(x, gpre_ref[...], mod_ref[1], mod_ref[0]).astype(BF16)
        proj = _dot_nt(h_in, wt_ref[...])
        gates = proj[:, 2 * D_HQ + 2 * D_HV:] + bif_ref[...]
        ig = (GATE_SOFTCAP * jnp.tanh(gates / GATE_SOFTCAP))[:, :N_HEADS]
        lf = _log_sigmoid(gates)[:, N_HEADS:]
        m_prev = m_ref[...]
        st = lf + m_prev
        m_t = jnp.maximum(st, ig)
        w_i = jnp.exp(ig - m_t)
        w_s = jnp.exp(st - m_t)
        floor = jnp.exp(-m_t)
        m_out_ref[...] = m_t
        q = proj[:, :D_HQ] * (D_QK ** -0.5)
        k = proj[:, D_HQ:2 * D_HQ]
        n_prev = n_ref[...]
        lane = lax.broadcasted_iota(jnp.int32, (rows, LANES), 1)
        lo = lane < D_QK

        def head_sums(z):
            out = []
            for p in range(N_HEADS // 2):
                zp = z[:, p * LANES:(p + 1) * LANES]
                out.append(jnp.sum(jnp.where(lo, zp, 0.0), axis=-1, keepdims=True))
                out.append(jnp.sum(jnp.where(lo, 0.0, zp), axis=-1, keepdims=True))
            return out

        qk_h = head_sums(q * k)
        qn_h = head_sums(q * n_prev)
        ws_cols, wi_cols = [], []
        for hd in range(N_HEADS):
            ws_h = w_s[:, hd:hd + 1]
            wi_h = w_i[:, hd:hd + 1]
            s_h = qk_h[hd] * wi_h
            den = ws_h * qn_h[hd] + s_h
            inv = 1.0 / jnp.maximum(jnp.abs(den), floor[:, hd:hd + 1])
            a_s[:, hd * D_V:(hd + 1) * D_V] = jnp.broadcast_to(ws_h * inv, (rows, D_V))
            b_s[:, hd * D_V:(hd + 1) * D_V] = jnp.broadcast_to(s_h * inv, (rows, D_V))
            ws_cols.append(ws_h)
            wi_cols.append(wi_h)
        ws_x = _pair_expand(ws_cols)
        kw = k * _pair_expand(wi_cols)
        n_out_ref[...] = ws_x * n_prev + kw
        v_s[...] = proj[:, 2 * D_HQ:2 * D_HQ + D_HV]
        og_s[...] = proj[:, 2 * D_HQ + D_HV:2 * D_HQ + 2 * D_HV]
        for dst, src in ((qt_s, q), (kwt_s, kw), (wst_s, ws_x)):
            src_t = src.T
            for j in range(rows // blk):
                dst[j] = src_t[:, j * blk:(j + 1) * blk]

    block_rows = pl.ds(pl.multiple_of(i * blk, blk), blk)
    v_blk, og_blk, a_blk, b_blk = v_s[block_rows, :], og_s[block_rows, :], a_s[block_rows, :], b_s[block_rows, :]
    y_tiles = []
    for b in range(blk):
        v_i = _row_to_head_tile(v_blk, b)
        readout = []
        for hd in range(N_HEADS):
            rs = slice(hd * D_QK, (hd + 1) * D_QK)
            c_h = c_ref[b, hd]
            readout.append(jnp.sum(c_h * qt_s[i, rs, b:b + 1], axis=0, keepdims=True))
            c_out_ref[b, hd] = c_h * wst_s[i, rs, b:b + 1] + kwt_s[i, rs, b:b + 1] * v_i[hd:hd + 1, :]
        hh = _row_to_head_tile(a_blk, b) * jnp.concatenate(readout, axis=0) + _row_to_head_tile(b_blk, b) * v_i
        y_tiles.append(_sigmoid(_row_to_head_tile(og_blk, b)) * (hh * _rms(hh) * ghead_ref[...]))
    y_s[block_rows, :] = jnp.concatenate(
        [jnp.concatenate([y_tiles[b][hd:hd + 1, :] for b in range(blk)], axis=0) for hd in range(N_HEADS)], axis=1)

    @pl.when(i == pl.num_programs(0) - 1)
    def _():
        y = _dot(y_s[...].astype(BF16), wout_ref[...])
        o_ref[...] = x_ref[...] + y * _rms(y) * (gpost_ref[...] * (1.0 + mod_ref[2]))


def _mlstm_step(x, mod, g_pre, g_post, layer, w_t, b_if, g_head, w_out, c0, n0, m0):
    rows = x.shape[0]
    blk = SAMPLE_BLOCK
    nblk = rows // blk
    const = lambda shape: pl.BlockSpec(shape, lambda i: tuple(0 for _ in shape))
    gain = pl.BlockSpec((None, None, 1, D_MODEL), lambda i: (layer, 1, 0, 0))
    c_spec = pl.BlockSpec((blk, N_HEADS, D_QK, D_V), lambda i: (i, 0, 0, 0))
    n_flat = n0.reshape(rows, D_HQ)
    y, c_new, n_new, m_new = pl.pallas_call(
        _mlstm_step_kernel,
        grid=(nblk,),
        in_specs=[
            const(x.shape),
            pl.BlockSpec((None, 3, rows, D_MODEL), lambda i: (layer, 1, 0, 0)),
            gain,
            gain,
            _resident(w_t.shape),
            const(b_if.shape),
            const(g_head.shape),
            _resident(w_out.shape),
            c_spec,
            const(n_flat.shape),
            const(m0.shape),
        ],
        out_specs=[const(x.shape), c_spec, const(n_flat.shape), const(m0.shape)],
        out_shape=[jax.ShapeDtypeStruct(x.shape, F32), jax.ShapeDtypeStruct(c0.shape, F32),
                   jax.ShapeDtypeStruct(n_flat.shape, F32), jax.ShapeDtypeStruct(m0.shape, F32)],
        scratch_shapes=[pltpu.VMEM((rows, D_HV), F32)] * 5 + [pltpu.VMEM((nblk, D_HQ, blk), F32)] * 3,
        compiler_params=_params("arbitrary"),
        name="mlstm_step",
    )(x, mod, g_pre, g_post, w_t, b_if, g_head, w_out, c0, n_flat, m0)
    return y, c_new, n_new.reshape(rows, N_HEADS, D_QK), m_new


def _conv_kernel(x_ref, mod_ref, gpre_ref, gpost_ref, win_ref, cw_ref, wout_ref, o_ref, buf_out_ref, tail_s):
    t = pl.program_id(1)

    @pl.when(t == 0)
    def _():
        tail_s[...] = jnp.zeros_like(tail_s)

    x = x_ref[...]
    rows = x.shape[0]
    h_in = _norm_mod(x, gpre_ref[...], _group_mod(mod_ref, 1), _group_mod(mod_ref, 0)).astype(BF16)
    p = _dot(h_in, win_ref[...])
    bg = p[:, :D_MODEL]
    u = p[:, D_MODEL:2 * D_MODEL] * p[:, 2 * D_MODEL:]
    prev1 = tail_s[7:8, :]
    prev2 = tail_s[6:7, :]
    ridx = lax.broadcasted_iota(jnp.int32, u.shape, 0)
    u1 = jnp.where(ridx == 0, prev1, pltpu.roll(u, 1, 0))
    u2 = jnp.where(ridx == 0, prev2, jnp.where(ridx == 1, prev1, pltpu.roll(u, 2, 0)))
    conv = cw_ref[0:1, :] * u2 + cw_ref[1:2, :] * u1 + cw_ref[2:3, :] * u
    tail_s[...] = u[rows - 8:, :]
    y = _dot((bg * conv).astype(BF16), wout_ref[...])
    o_ref[...] = x + y * _rms(y) * (gpost_ref[...] * (1.0 + _group_mod(mod_ref, 2)))

    @pl.when(t == pl.num_programs(1) - 1)
    def _():
        buf_out_ref[...] = u[rows - (CONV_W - 1):, :]


def _conv_prompt(x, mod, g_pre, g_post, layer, w_in, conv_w, w_out):
    batch, seq, _ = x.shape
    rows = MIX_ROWS
    return pl.pallas_call(
        _conv_kernel,
        grid=(batch, seq // rows),
        in_specs=[
            pl.BlockSpec((None, rows, D_MODEL), lambda b, t: (b, t, 0)),
            *_mixer_mod_specs(layer, batch),
            _resident(w_in.shape),
            _resident(conv_w.shape),
            _resident(w_out.shape),
        ],
        out_specs=[
            pl.BlockSpec((None, rows, D_MODEL), lambda b, t: (b, t, 0)),
            pl.BlockSpec((None, CONV_W - 1, D_MODEL), lambda b, t: (b, 0, 0)),
        ],
        out_shape=[jax.ShapeDtypeStruct(x.shape, F32),
                   jax.ShapeDtypeStruct((batch, CONV_W - 1, D_MODEL), F32)],
        scratch_shapes=[pltpu.VMEM((8, D_MODEL), F32)],
        compiler_params=_params("arbitrary", "arbitrary"),
        name="conv_prompt",
    )(x, mod, g_pre, g_post, w_in, conv_w, w_out)


def _conv_sample_kernel(x_ref, mod_ref, gpre_ref, gpost_ref, win_ref, cw_ref, wout_ref, buf_ref,
                        o_ref, buf_out_ref):
    x = x_ref[...]
    h_in = _norm_mod(x, gpre_ref[...], mod_ref[1], mod_ref[0]).astype(BF16)
    p = _dot(h_in, win_ref[...])
    bg = p[:, :D_MODEL]
    u = p[:, D_MODEL:2 * D_MODEL] * p[:, 2 * D_MODEL:]
    conv = cw_ref[0:1, :] * buf_ref[0] + cw_ref[1:2, :] * buf_ref[1] + cw_ref[2:3, :] * u
    y = _dot((bg * conv).astype(BF16), wout_ref[...])
    o_ref[...] = x + (1.0 + mod_ref[2]) * (y * _rms(y) * gpost_ref[...])
    buf_out_ref[0] = buf_ref[1]
    buf_out_ref[1] = u


def _conv_sample(x, mod, g_pre, g_post, w_in, conv_w, w_out, buf):
    full = lambda shape: pl.BlockSpec(shape, lambda: tuple(0 for _ in shape))
    ins = (x, mod, g_pre.reshape(1, D_MODEL), g_post.reshape(1, D_MODEL), w_in, conv_w, w_out, buf)
    return pl.pallas_call(
        _conv_sample_kernel,
        in_specs=[full(z.shape) for z in ins],
        out_specs=[full(x.shape), full(buf.shape)],
        out_shape=[jax.ShapeDtypeStruct(x.shape, F32), jax.ShapeDtypeStruct(buf.shape, F32)],
        compiler_params=_params(),
        name="conv_sample",
    )(*ins)


def kernel(x_prompt, x_sample, c_prompt, c_sample, state_mlstm_C, state_mlstm_n, state_mlstm_m, state_conv,
           w_ada, b_ada, g_pre, g_post, ffn_wg, ffn_wu, ffn_wd,
           ml_w_in, ml_b_i, ml_b_f, ml_g_head, ml_w_out, cv_w_in, cv_conv_w, cv_w_out):
    depth = w_ada.shape[0]
    batch, seq, _ = x_prompt.shape
    n_sample = x_sample.shape[0]
    n_ml = ml_w_in.shape[0]
    assert x_sample.shape[1] == 1 and seq % ML_CHUNK == 0

    mod_p, mod_s = _ada(c_prompt, c_sample, w_ada, b_ada)
    gp = g_pre.reshape(depth, 3, 1, D_MODEL)
    gq = g_post.reshape(depth, 3, 1, D_MODEL)

    ml_t = ml_w_in.transpose(0, 2, 1).astype(BF16)
    ml_bif = jnp.concatenate([ml_b_i, ml_b_f], axis=-1)
    ml_bif_t = jnp.broadcast_to(ml_bif[:, :, None], (n_ml, 2 * N_HEADS, MIX_ROWS))
    ml_ghead_t = jnp.broadcast_to(ml_g_head[:, :, :, None], (n_ml, N_HEADS, D_V, LANES))
    ml_out = ml_w_out.astype(BF16)
    cv_in = cv_w_in.astype(BF16)
    cv_out = cv_w_out.astype(BF16)

    xp = x_prompt
    xs = x_sample.reshape(n_sample, D_MODEL)
    p_c, p_n, p_m, p_buf, s_c, s_n, s_m, s_buf = [], [], [], [], [], [], [], []
    for l in range(depth):
        ms = lambda s: mod_s[l, 3 * s:3 * s + 3]
        xp, xs = _ffn(xp, xs, mod_p, mod_s, gp, gq, ffn_wg, ffn_wu, ffn_wd, l, 0)
        j = l // 2
        if l % 2 == 0:
            xp, c_j, n_j, m_j = _mlstm_prompt(xp, mod_p, gp, gq, l, ml_t[j], ml_bif_t[j], ml_ghead_t[j], ml_out[j])
            p_c.append(c_j.reshape(batch, N_HEADS, D_QK, D_V))
            p_n.append(n_j.reshape(batch, N_HEADS, D_QK))
            p_m.append(m_j[:, :, 0])
            ys, c_j, n_j, m_j = _mlstm_step(xs, mod_s, gp, gq, l, ml_t[j], ml_bif[j][None], ml_g_head[j], ml_out[j],
                                            state_mlstm_C[j], state_mlstm_n[j], state_mlstm_m[j])
            xs = ys
            s_c.append(c_j)
            s_n.append(n_j)
            s_m.append(m_j)
        else:
            xp, buf_j = _conv_prompt(xp, mod_p, gp, gq, l, cv_in[j], cv_conv_w[j], cv_out[j])
            p_buf.append(buf_j)
            ys, buf_j = _conv_sample(xs, ms(1), g_pre[l, 1], g_post[l, 1], cv_in[j], cv_conv_w[j], cv_out[j],
                                     state_conv[j].transpose(1, 0, 2))
            xs = ys
            s_buf.append(buf_j.transpose(1, 0, 2))
        xp, xs = _ffn(xp, xs, mod_p, mod_s, gp, gq, ffn_wg, ffn_wu, ffn_wd, l, 1)

    return (xp, xs.reshape(n_sample, 1, D_MODEL),
            jnp.stack(p_c), jnp.stack(p_n), jnp.stack(p_m), jnp.stack(p_buf),
            jnp.stack(s_c), jnp.stack(s_n), jnp.stack(s_m), jnp.stack(s_buf))
```

```python
import functools

import jax
import jax.numpy as jnp
from jax import lax
from jax.experimental import pallas as pl
from jax.experimental.pallas import tpu as pltpu

F32 = jnp.float32
BF16 = jnp.bfloat16

D_MODEL = 1024
N_HEADS = 8
D_QK = 64
D_V = 128
D_HQ = N_HEADS * D_QK
D_HV = N_HEADS * D_V
D_FF = 2816
N_ADA = 9
CONV_W = 3
GATE_SOFTCAP = 15.0
EPS = 1e-6

LANES = 128
BF16_ROWS = 16
MXU_COLS = 256
VMEM_LIMIT_BYTES = 56 * 1024 * 1024

FFN_ROWS = 1024
FFN_SUB = MXU_COLS
FFN_STAGE_PIECES = 8
ADA_COLS = 3 * D_MODEL
MIX_ROWS = 1024
ML_CHUNK = LANES
ML_STATE_ROWS = D_V + BF16_ROWS
SAMPLE_BLOCK = 8


def _params(*sem):
    return pltpu.CompilerParams(dimension_semantics=sem, vmem_limit_bytes=VMEM_LIMIT_BYTES)


def _sigmoid(x):
    return 1.0 / (1.0 + jnp.exp(-x))


def _rms(x):
    return lax.rsqrt(jnp.mean(x * x, axis=-1, keepdims=True) + EPS)


def _norm_mod(x, gain, scale, shift):
    return x * _rms(x) * (gain * (1.0 + scale)) + shift


def _dot(a, b):
    return jnp.dot(a, b, preferred_element_type=F32)


def _dot_nt(a, b):
    return lax.dot_general(a, b, (((1,), (1,)), ((), ())), preferred_element_type=F32)


def _dot_tn(a, b):
    return lax.dot_general(a, b, (((0,), (0,)), ((), ())), preferred_element_type=F32)


def _log_sigmoid(x):
    return -(jnp.maximum(-x, 0.0) + jnp.log1p(jnp.exp(-jnp.abs(x))))


def _resident(shape):
    return pl.BlockSpec(shape, lambda *_: tuple(0 for _ in shape), pipeline_mode=pl.Buffered(1))


def _ada_kernel(cp_ref, cs_ref, w_ref, b_ref, op_ref, os_ref):
    w = w_ref[...].astype(BF16)
    for c_ref, o_ref in ((cp_ref, op_ref), (cs_ref, os_ref)):
        c = c_ref[...]
        res = _dot((c * _sigmoid(c)).astype(BF16), w)
        for k in range(o_ref.shape[0]):
            o_ref[k] = res[:, k * D_MODEL:(k + 1) * D_MODEL] + b_ref[k]


def _ada(c_prompt, c_sample, w_ada, b_ada):
    depth = w_ada.shape[0]
    per_step = ADA_COLS // D_MODEL
    out = lambda c: (pl.BlockSpec((None, per_step, c.shape[0], D_MODEL), lambda l, j: (l, j, 0, 0)),
                     jax.ShapeDtypeStruct((depth, N_ADA, c.shape[0], D_MODEL), F32))
    (spec_p, shape_p), (spec_s, shape_s) = out(c_prompt), out(c_sample)
    return pl.pallas_call(
        _ada_kernel,
        grid=(depth, N_ADA // per_step),
        in_specs=[
            pl.BlockSpec(c_prompt.shape, lambda l, j: (0, 0)),
            pl.BlockSpec(c_sample.shape, lambda l, j: (0, 0)),
            pl.BlockSpec((None, D_MODEL, ADA_COLS), lambda l, j: (l, 0, j)),
            pl.BlockSpec((None, per_step, 1, D_MODEL), lambda l, j: (l, j, 0, 0)),
        ],
        out_specs=[spec_p, spec_s],
        out_shape=[shape_p, shape_s],
        compiler_params=_params("arbitrary", "arbitrary"),
        name="ada",
    )(c_prompt, c_sample, w_ada, b_ada.reshape(depth, N_ADA, 1, D_MODEL))


def _group_mod(mod_ref, j):
    return mod_ref[j, pl.ds(pl.program_id(0), 1), :]


def _swiglu_chunk(hn, wg_ref, wu_ref, wd_ref, c, sub):
    cols = slice(c * sub, (c + 1) * sub)
    g = _dot(hn, wg_ref[:, cols])
    u = _dot(hn, wu_ref[:, cols])
    return _dot((g * _sigmoid(g) * u).astype(BF16), wd_ref[cols, :])


def _ffn_kernel(xp_ref, xs_ref, modp_ref, mods_ref, gpre_ref, gpost_ref, wg_hbm, wu_hbm, wd_hbm,
                op_ref, os_ref,
                wg_s, wu_s, wd_s, up_stage, dn_stage, sem, hn_s, acc_s, *,
                layer, half, n_tiles, tiles_per_group, sub):
    s = pl.program_id(0)
    n_chunks = wg_s.shape[1] // sub

    up_rows = up_stage.shape[2]
    dn_rows = dn_stage.shape[1]
    n_pieces = wg_s.shape[0] // up_rows

    def piece_copies(c):
        slot = c % 2
        up = pl.ds(c * up_rows, up_rows)
        dn = pl.ds(c * dn_rows, dn_rows)
        return (pltpu.make_async_copy(wg_hbm.at[layer, half, up, :], up_stage.at[slot, 0], sem.at[slot, 0]),
                pltpu.make_async_copy(wu_hbm.at[layer, half, up, :], up_stage.at[slot, 1], sem.at[slot, 1]),
                pltpu.make_async_copy(wd_hbm.at[layer, half, dn, :], dn_stage.at[slot], sem.at[slot, 2]))

    def stage_weights():
        for copy in piece_copies(0):
            copy.start()
        for c in range(n_pieces):
            if c + 1 < n_pieces:
                for copy in piece_copies(c + 1):
                    copy.start()
            for copy in piece_copies(c):
                copy.wait()
            wg_s[c * up_rows:(c + 1) * up_rows, :] = up_stage[c % 2, 0].astype(BF16)
            wu_s[c * up_rows:(c + 1) * up_rows, :] = up_stage[c % 2, 1].astype(BF16)
            wd_s[c * dn_rows:(c + 1) * dn_rows, :] = dn_stage[c % 2].astype(BF16)

    def half_step(x_ref, o_ref, scale, shift, gate):
        rows = x_ref.shape[0]
        x = x_ref[...]
        hn_s[0:rows, :] = _norm_mod(x, gpre_ref[...], scale, shift).astype(BF16)
        hn = hn_s[0:rows, :]
        for c in range(n_chunks):
            down = _swiglu_chunk(hn, wg_s, wu_s, wd_s, c, sub)
            if c == 0:
                acc_s[0:rows, :] = down
            else:
                acc_s[0:rows, :] += down
        y = acc_s[0:rows, :]
        o_ref[...] = x + y * _rms(y) * (gpost_ref[...] * (0.5 * (1.0 + gate)))

    group = jnp.minimum(s, n_tiles - 1) // tiles_per_group
    group_mod = lambda j: modp_ref[j, pl.ds(group, 1), :]

    @pl.when(s == 0)
    def _():
        stage_weights()

    @pl.when(s < n_tiles)
    def _():
        half_step(xp_ref, op_ref, group_mod(1), group_mod(0), group_mod(2))

    @pl.when(s == n_tiles)
    def _():
        half_step(xs_ref, os_ref, mods_ref[1], mods_ref[0], mods_ref[2])


def _ffn(xp, xs, mod_p, mod_s, g_pre, g_post, wg, wu, wd, layer, half):
    groups, seq, _ = xp.shape
    n_sample = xs.shape[0]
    rows = FFN_ROWS
    tpg = seq // rows
    n_tiles = groups * tpg
    sub = FFN_SUB
    tile = lambda s: jnp.minimum(s, n_tiles - 1)
    xp_spec = pl.BlockSpec((None, rows, D_MODEL), lambda s: (tile(s) // tpg, tile(s) % tpg, 0))
    xs_spec = pl.BlockSpec((n_sample, D_MODEL), lambda s: (0, 0))
    gain = pl.BlockSpec((None, None, 1, D_MODEL), lambda s: (layer, 2 * half, 0, 0))
    hbm = pl.BlockSpec(memory_space=pl.ANY)
    return pl.pallas_call(
        functools.partial(_ffn_kernel, layer=layer, half=half, n_tiles=n_tiles, tiles_per_group=tpg, sub=sub),
        grid=(n_tiles + 1,),
        in_specs=[
            xp_spec,
            xs_spec,
            pl.BlockSpec((None, 3, groups, D_MODEL), lambda s: (layer, 2 * half, 0, 0)),
            pl.BlockSpec((None, 3, n_sample, D_MODEL), lambda s: (layer, 2 * half, 0, 0)),
            gain,
            gain,
            hbm,
            hbm,
            hbm,
        ],
        out_specs=[xp_spec, xs_spec],
        out_shape=[jax.ShapeDtypeStruct(xp.shape, F32), jax.ShapeDtypeStruct(xs.shape, F32)],
        scratch_shapes=[
            pltpu.VMEM((D_MODEL, D_FF), BF16),
            pltpu.VMEM((D_MODEL, D_FF), BF16),
            pltpu.VMEM((D_FF, D_MODEL), BF16),
            pltpu.VMEM((2, 2, D_MODEL // FFN_STAGE_PIECES, D_FF), F32),
            pltpu.VMEM((2, D_FF // FFN_STAGE_PIECES, D_MODEL), F32),
            pltpu.SemaphoreType.DMA((2, 3)),
            pltpu.VMEM((max(rows, n_sample), D_MODEL), BF16),
            pltpu.VMEM((max(rows, n_sample), D_MODEL), F32),
        ],
        compiler_params=_params("arbitrary"),
        name="ffn",
    )(xp, xs, mod_p, mod_s, g_pre, g_post, wg, wu, wd)


def _block_diag(a, b):
    za = jnp.zeros((a.shape[0], b.shape[1]), a.dtype)
    zb = jnp.zeros((b.shape[0], a.shape[1]), a.dtype)
    return jnp.concatenate([jnp.concatenate([a, za], axis=1), jnp.concatenate([zb, b], axis=1)], axis=0)


def _mlstm_kernel(x_ref, mod_ref, gpre_ref, gpost_ref, wt_ref, bif_ref, ght_ref, wout_ref,
                  o_ref, c_out_ref, n_out_ref, m_out_ref,
                  st_s, m_s, yt_s, *, chunk):
    t = pl.program_id(1)

    @pl.when(t == 0)
    def _():
        st_s[...] = jnp.zeros_like(st_s)
        m_s[...] = jnp.zeros_like(m_s)

    x = x_ref[...]
    rows = x.shape[0]
    h_in = _norm_mod(x, gpre_ref[...], _group_mod(mod_ref, 1), _group_mod(mod_ref, 0)).astype(BF16)
    proj_q = _dot_nt(wt_ref[0:D_HQ], h_in)
    proj_t = _dot_nt(wt_ref[2 * D_HQ:], h_in)
    k_all = _dot_nt(h_in, wt_ref[D_HQ:2 * D_HQ])
    r_v, r_o, r_i, r_f = 0, D_HV, 2 * D_HV, 2 * D_HV + N_HEADS

    s_idx = lax.broadcasted_iota(jnp.int32, (chunk, chunk), 0)
    t_idx = lax.broadcasted_iota(jnp.int32, (chunk, chunk), 1)
    causal = s_idx <= t_idx
    tri = jnp.where(causal, 1.0, 0.0).astype(BF16)
    lane = lax.broadcasted_iota(jnp.int32, (1, LANES), 1)
    low = lane < D_QK
    ones_rows = jnp.ones((BF16_ROWS, 2 * chunk), BF16)
    zero_rows = jnp.zeros((LANES - N_HEADS, chunk), F32)

    ig_all = GATE_SOFTCAP * jnp.tanh((proj_t[r_i:r_i + N_HEADS, :] + bif_ref[0:N_HEADS, :]) / GATE_SOFTCAP)
    lf_all = _log_sigmoid(proj_t[r_f:r_f + N_HEADS, :] + bif_ref[N_HEADS:, :])

    for c in range(rows // chunk):
        cs = slice(c * chunk, (c + 1) * chunk)
        lf = lf_all[:, cs]
        lf_hi = lf.astype(BF16).astype(F32)
        lf_mid = (lf - lf_hi).astype(BF16).astype(F32)
        lf_lo = lf - lf_hi - lf_mid
        b3 = _dot(jnp.concatenate([lf_hi, lf_mid, lf_lo, jnp.zeros_like(lf)], axis=0).astype(BF16), tri)
        b = b3[0:N_HEADS] + b3[N_HEADS:2 * N_HEADS] + b3[2 * N_HEADS:3 * N_HEADS]
        col = ig_all[:, cs] - b
        b_last = jnp.broadcast_to(b[:, chunk - 1:chunk], b.shape)
        col_s = jnp.concatenate([col, zero_rows], axis=0).T

        for p in range(N_HEADS // 2):
            h0, h1 = 2 * p, 2 * p + 1
            qt = [(proj_q[h * D_QK:(h + 1) * D_QK, cs] * (D_QK ** -0.5)).astype(BF16) for h in (h0, h1)]
            vt = [proj_t[r_v + h * D_V:r_v + (h + 1) * D_V, cs] for h in (h0, h1)]
            k_pair = k_all[cs, p * LANES:(p + 1) * LANES]
            state = st_s[p]
            lhs1 = jnp.concatenate([k_pair.astype(BF16), state.astype(BF16)], axis=0)
            r1 = _dot(lhs1, _block_diag(qt[0], qt[1]))
            probs, w_state, inv_floor, w_k, w_decay = [], [], [], [], []
            for i, h in enumerate((h0, h1)):
                m_prev = m_s[h:h + 1, :]
                col_m = jnp.where(causal, col_s[:, h:h + 1], -jnp.inf)
                g = jnp.maximum(m_prev, jnp.max(col_m, axis=0, keepdims=True))
                probs.append((jnp.exp(col_m - g) * r1[:chunk, i * chunk:(i + 1) * chunk]).astype(BF16))
                m_t = b[h:h + 1, :] + g
                m_new = jnp.broadcast_to(m_t[:, chunk - 1:chunk], m_t.shape)
                w_state.append(jnp.exp(m_prev - g))
                inv_floor.append(jnp.exp(-m_t))
                w_decay.append(jnp.exp(b_last[h:h + 1, :] + m_prev - m_new))
                w_k.append(jnp.exp(b_last[h:h + 1, :] - m_new + col[h:h + 1, :]))
                m_s[h:h + 1, :] = m_new
            lhs2 = jnp.concatenate([jnp.concatenate([vt[0], vt[1]], axis=1).astype(BF16), ones_rows], axis=0)
            r2 = _dot(lhs2, _block_diag(probs[0], probs[1]))
            for i, h in enumerate((h0, h1)):
                ls = slice(i * chunk, (i + 1) * chunk)
                den = w_state[i] * r1[chunk + D_V:chunk + D_V + 1, ls] + r2[D_V:D_V + 1, ls]
                inv = 1.0 / jnp.maximum(jnp.abs(den), inv_floor[i])
                ht = (w_state[i] * r1[chunk:chunk + D_V, ls] + r2[:D_V, ls]) * inv
                rn = lax.rsqrt(jnp.mean(ht * ht, axis=0, keepdims=True) + EPS)
                og = proj_t[r_o + h * D_V:r_o + (h + 1) * D_V, cs]
                yt_s[h * D_V:(h + 1) * D_V, cs] = (_sigmoid(og) * (ht * rn * ght_ref[h])).astype(BF16)
            lhs3 = jnp.concatenate(
                [jnp.concatenate([vt[i] * w_k[i] for i in range(2)], axis=1),
                 jnp.concatenate([jnp.broadcast_to(w_k[i], (BF16_ROWS, chunk)) for i in range(2)], axis=1)],
                axis=0).astype(BF16)
            rhs3 = jnp.concatenate([jnp.where(low, k_pair, 0.0), jnp.where(low, 0.0, k_pair)], axis=0).astype(BF16)
            decay = jnp.where(low, w_decay[0], w_decay[1])
            st_s[p] = state * decay + _dot(lhs3, rhs3)

    y = _dot_tn(yt_s[...], wout_ref[...])
    o_ref[...] = x + y * _rms(y) * (gpost_ref[...] * (1.0 + _group_mod(mod_ref, 2)))

    @pl.when(t == pl.num_programs(1) - 1)
    def _():
        for p in range(N_HEADS // 2):
            state = st_s[p]
            c_out_ref[p * LANES:(p + 1) * LANES, :] = state[:D_V, :].T
            n_out_ref[:, p * LANES:(p + 1) * LANES] = state[D_V:D_V + 1, :]
        m_out_ref[...] = m_s[...]


def _mixer_mod_specs(layer, batch):
    gain = pl.BlockSpec((None, None, 1, D_MODEL), lambda b, t: (layer, 1, 0, 0))
    return [pl.BlockSpec((None, 3, batch, D_MODEL), lambda b, t: (layer, 1, 0, 0)), gain, gain]


def _mlstm_prompt(x, mod, g_pre, g_post, layer, w_t, b_if, g_head_t, w_out):
    batch, seq, _ = x.shape
    rows = MIX_ROWS
    return pl.pallas_call(
        functools.partial(_mlstm_kernel, chunk=ML_CHUNK),
        grid=(batch, seq // rows),
        in_specs=[
            pl.BlockSpec((None, rows, D_MODEL), lambda b, t: (b, t, 0)),
            *_mixer_mod_specs(layer, batch),
            _resident(w_t.shape),
            _resident(b_if.shape),
            _resident(g_head_t.shape),
            _resident(w_out.shape),
        ],
        out_specs=[
            pl.BlockSpec((None, rows, D_MODEL), lambda b, t: (b, t, 0)),
            pl.BlockSpec((None, D_HQ, D_V), lambda b, t: (b, 0, 0)),
            pl.BlockSpec((None, 1, D_HQ), lambda b, t: (b, 0, 0)),
            pl.BlockSpec((None, N_HEADS, LANES), lambda b, t: (b, 0, 0)),
        ],
        out_shape=[
            jax.ShapeDtypeStruct(x.shape, F32),
            jax.ShapeDtypeStruct((batch, D_HQ, D_V), F32),
            jax.ShapeDtypeStruct((batch, 1, D_HQ), F32),
            jax.ShapeDtypeStruct((batch, N_HEADS, LANES), F32),
        ],
        scratch_shapes=[
            pltpu.VMEM((N_HEADS // 2, ML_STATE_ROWS, LANES), F32),
            pltpu.VMEM((N_HEADS, LANES), F32),
            pltpu.VMEM((D_HV, rows), BF16),
        ],
        compiler_params=_params("arbitrary", "arbitrary"),
        name="mlstm_prompt",
    )(x, mod, g_pre, g_post, w_t, b_if, g_head_t, w_out)


def _pair_expand(cols):
    lane = lax.broadcasted_iota(jnp.int32, (cols[0].shape[0], LANES), 1)
    return jnp.concatenate(
        [jnp.where(lane < D_QK, cols[2 * p], cols[2 * p + 1]) for p in range(N_HEADS // 2)], axis=1)


def _row_to_head_tile(rows, b):
    return jnp.concatenate([rows[b:b + 1, h * D_V:(h + 1) * D_V] for h in range(N_HEADS)], axis=0)


def _mlstm_step_kernel(x_ref, mod_ref, gpre_ref, gpost_ref, wt_ref, bif_ref, ghead_ref, wout_ref,
                       c_ref, n_ref, m_ref,
                       o_ref, c_out_ref, n_out_ref, m_out_ref,
                       v_s, og_s, a_s, b_s, y_s, qt_s, kwt_s, wst_s):
    i = pl.program_id(0)
    blk = c_ref.shape[0]
    rows = x_ref.shape[0]

    @pl.when(i == 0)
    def _():
        x = x_ref[...]
        h_in = _norm_mod(x, gpre_ref[...], mod_ref[1], mod_ref[0]).astype(BF16)
        proj = _dot_nt(h_in, wt_ref[...])
        gates = proj[:, 2 * D_HQ + 2 * D_HV:] + bif_ref[...]
        ig = (GATE_SOFTCAP * jnp.tanh(gates / GATE_SOFTCAP))[:, :N_HEADS]
        lf = _log_sigmoid(gates)[:, N_HEADS:]
        m_prev = m_ref[...]
        st = lf + m_prev
        m_t = jnp.maximum(st, ig)
        w_i = jnp.exp(ig - m_t)
        w_s = jnp.exp(st - m_t)
        floor = jnp.exp(-m_t)
        m_out_ref[...] = m_t
        q = proj[:, :D_HQ] * (D_QK ** -0.5)
        k = proj[:, D_HQ:2 * D_HQ]
        n_prev = n_ref[...]
        lane = lax.broadcasted_iota(jnp.int32, (rows, LANES), 1)
        lo = lane < D_QK

        def head_sums(z):
            out = []
            for p in range(N_HEADS // 2):
                zp = z[:, p * LANES:(p + 1) * LANES]
                out.append(jnp.sum(jnp.where(lo, zp, 0.0), axis=-1, keepdims=True))
                out.append(jnp.sum(jnp.where(lo, 0.0, zp), axis=-1, keepdims=True))
            return out

        qk_h = head_sums(q * k)
        qn_h = head_sums(q * n_prev)
        ws_cols, wi_cols = [], []
        for hd in range(N_HEADS):
            ws_h = w_s[:, hd:hd + 1]
            wi_h = w_i[:, hd:hd + 1]
            s_h = qk_h[hd] * wi_h
            den = ws_h * qn_h[hd] + s_h
            inv = 1.0 / jnp.maximum(jnp.abs(den), floor[:, hd:hd + 1])
            a_s[:, hd * D_V:(hd + 1) * D_V] = jnp.broadcast_to(ws_h * inv, (rows, D_V))
            b_s[:, hd * D_V:(hd + 1) * D_V] = jnp.broadcast_to(s_h * inv, (rows, D_V))
            ws_cols.append(ws_h)
            wi_cols.append(wi_h)
        ws_x = _pair_expand(ws_cols)
        kw = k * _pair_expand(wi_cols)
        n_out_ref[...] = ws_x * n_prev + kw
        v_s[...] = proj[:, 2 * D_HQ:2 * D_HQ + D_HV]
        og_s[...] = proj[:, 2 * D_HQ + D_HV:2 * D_HQ + 2 * D_HV]
        for dst, src in ((qt_s, q), (kwt_s, kw), (wst_s, ws_x)):
            src_t = src.T
            for j in range(rows // blk):
                dst[j] = src_t[:, j * blk:(j + 1) * blk]

    block_rows = pl.ds(pl.multiple_of(i * blk, blk), blk)
    v_blk, og_blk, a_blk, b_blk = v_s[block_rows, :], og_s[block_rows, :], a_s[block_rows, :], b_s[block_rows, :]
    y_tiles = []
    for b in range(blk):
        v_i = _row_to_head_tile(v_blk, b)
        readout = []
        for hd in range(N_HEADS):
            rs = slice(hd * D_QK, (hd + 1) * D_QK)
            c_h = c_ref[b, hd]
            readout.append(jnp.sum(c_h * qt_s[i, rs, b:b + 1], axis=0, keepdims=True))
            c_out_ref[b, hd] = c_h * wst_s[i, rs, b:b + 1] + kwt_s[i, rs, b:b + 1] * v_i[hd:hd + 1, :]
        hh = _row_to_head_tile(a_blk, b) * jnp.concatenate(readout, axis=0) + _row_to_head_tile(b_blk, b) * v_i
        y_tiles.append(_sigmoid(_row_to_head_tile(og_blk, b)) * (hh * _rms(hh) * ghead_ref[...]))
    y_s[block_rows, :] = jnp.concatenate(
        [jnp.concatenate([y_tiles[b][hd:hd + 1, :] for b in range(blk)], axis=0) for hd in range(N_HEADS)], axis=1)

    @pl.when(i == pl.num_programs(0) - 1)
    def _():
        y = _dot(y_s[...].astype(BF16), wout_ref[...])
        o_ref[...] = x_ref[...] + y * _rms(y) * (gpost_ref[...] * (1.0 + mod_ref[2]))


def _mlstm_step(x, mod, g_pre, g_post, layer, w_t, b_if, g_head, w_out, c0, n0, m0):
    rows = x.shape[0]
    blk = SAMPLE_BLOCK
    nblk = rows // blk
    const = lambda shape: pl.BlockSpec(shape, lambda i: tuple(0 for _ in shape))
    gain = pl.BlockSpec((None, None, 1, D_MODEL), lambda i: (layer, 1, 0, 0))
    c_spec = pl.BlockSpec((blk, N_HEADS, D_QK, D_V), lambda i: (i, 0, 0, 0))
    n_flat = n0.reshape(rows, D_HQ)
    y, c_new, n_new, m_new = pl.pallas_call(
        _mlstm_step_kernel,
        grid=(nblk,),
        in_specs=[
            const(x.shape),
            pl.BlockSpec((None, 3, rows, D_MODEL), lambda i: (layer, 1, 0, 0)),
            gain,
            gain,
            _resident(w_t.shape),
            const(b_if.shape),
            const(g_head.shape),
            _resident(w_out.shape),
            c_spec,
            const(n_flat.shape),
            const(m0.shape),
        ],
        out_specs=[const(x.shape), c_spec, const(n_flat.shape), const(m0.shape)],
        out_shape=[jax.ShapeDtypeStruct(x.shape, F32), jax.ShapeDtypeStruct(c0.shape, F32),
                   jax.ShapeDtypeStruct(n_flat.shape, F32), jax.ShapeDtypeStruct(m0.shape, F32)],
        scratch_shapes=[pltpu.VMEM((rows, D_HV), F32)] * 5 + [pltpu.VMEM((nblk, D_HQ, blk), F32)] * 3,
        compiler_params=_params("arbitrary"),
        name="mlstm_step",
    )(x, mod, g_pre, g_post, w_t, b_if, g_head, w_out, c0, n_flat, m0)
    return y, c_new, n_new.reshape(rows, N_HEADS, D_QK), m_new


def _conv_kernel(x_ref, mod_ref, gpre_ref, gpost_ref, win_ref, cw_ref, wout_ref, o_ref, buf_out_ref, tail_s):
    t = pl.program_id(1)

    @pl.when(t == 0)
    def _():
        tail_s[...] = jnp.zeros_like(tail_s)

    x = x_ref[...]
    rows = x.shape[0]
    h_in = _norm_mod(x, gpre_ref[...], _group_mod(mod_ref, 1), _group_mod(mod_ref, 0)).astype(BF16)
    p = _dot(h_in, win_ref[...])
    bg = p[:, :D_MODEL]
    u = p[:, D_MODEL:2 * D_MODEL] * p[:, 2 * D_MODEL:]
    prev1 = tail_s[7:8, :]
    prev2 = tail_s[6:7, :]
    ridx = lax.broadcasted_iota(jnp.int32, u.shape, 0)
    u1 = jnp.where(ridx == 0, prev1, pltpu.roll(u, 1, 0))
    u2 = jnp.where(ridx == 0, prev2, jnp.where(ridx == 1, prev1, pltpu.roll(u, 2, 0)))
    conv = cw_ref[0:1, :] * u2 + cw_ref[1:2, :] * u1 + cw_ref[2:3, :] * u
    tail_s[...] = u[rows - 8:, :]
    y = _dot((bg * conv).astype(BF16), wout_ref[...])
    o_ref[...] = x + y * _rms(y) * (gpost_ref[...] * (1.0 + _group_mod(mod_ref, 2)))

    @pl.when(t == pl.num_programs(1) - 1)
    def _():
        buf_out_ref[...] = u[rows - (CONV_W - 1):, :]


def _conv_prompt(x, mod, g_pre, g_post, layer, w_in, conv_w, w_out):
    batch, seq, _ = x.shape
    rows = MIX_ROWS
    return pl.pallas_call(
        _conv_kernel,
        grid=(batch, seq // rows),
        in_specs=[
            pl.BlockSpec((None, rows, D_MODEL), lambda b, t: (b, t, 0)),
            *_mixer_mod_specs(layer, batch),
            _resident(w_in.shape),
            _resident(conv_w.shape),
            _resident(w_out.shape),
        ],
        out_specs=[
            pl.BlockSpec((None, rows, D_MODEL), lambda b, t: (b, t, 0)),
            pl.BlockSpec((None, CONV_W - 1, D_MODEL), lambda b, t: (b, 0, 0)),
        ],
        out_shape=[jax.ShapeDtypeStruct(x.shape, F32),
                   jax.ShapeDtypeStruct((batch, CONV_W - 1, D_MODEL), F32)],
        scratch_shapes=[pltpu.VMEM((8, D_MODEL), F32)],
        compiler_params=_params("arbitrary", "arbitrary"),
        name="conv_prompt",
    )(x, mod, g_pre, g_post, w_in, conv_w, w_out)


def _conv_sample_kernel(x_ref, mod_ref, gpre_ref, gpost_ref, win_ref, cw_ref, wout_ref, buf_ref,
                        o_ref, buf_out_ref):
    x = x_ref[...]
    h_in = _norm_mod(x, gpre_ref[...], mod_ref[1], mod_ref[0]).astype(BF16)
    p = _dot(h_in, win_ref[...])
    bg = p[:, :D_MODEL]
    u = p[:, D_MODEL:2 * D_MODEL] * p[:, 2 * D_MODEL:]
    conv = cw_ref[0:1, :] * buf_ref[0] + cw_ref[1:2, :] * buf_ref[1] + cw_ref[2:3, :] * u
    y = _dot((bg * conv).astype(BF16), wout_ref[...])
    o_ref[...] = x + (1.0 + mod_ref[2]) * (y * _rms(y) * gpost_ref[...])
    buf_out_ref[0] = buf_ref[1]
    buf_out_ref[1] = u


def _conv_sample(x, mod, g_pre, g_post, w_in, conv_w, w_out, buf):
    full = lambda shape: pl.BlockSpec(shape, lambda: tuple(0 for _ in shape))
    ins = (x, mod, g_pre.reshape(1, D_MODEL), g_post.reshape(1, D_MODEL), w_in, conv_w, w_out, buf)
    return pl.pallas_call(
        _conv_sample_kernel,
        in_specs=[full(z.shape) for z in ins],
        out_specs=[full(x.shape), full(buf.shape)],
        out_shape=[jax.ShapeDtypeStruct(x.shape, F32), jax.ShapeDtypeStruct(buf.shape, F32)],
        compiler_params=_params(),
        name="conv_sample",
    )(*ins)


def kernel(x_prompt, x_sample, c_prompt, c_sample, state_mlstm_C, state_mlstm_n, state_mlstm_m, state_conv,
           w_ada, b_ada, g_pre, g_post, ffn_wg, ffn_wu, ffn_wd,
           ml_w_in, ml_b_i, ml_b_f, ml_g_head, ml_w_out, cv_w_in, cv_conv_w, cv_w_out):
    depth = w_ada.shape[0]
    batch, seq, _ = x_prompt.shape
    n_sample = x_sample.shape[0]
    n_ml = ml_w_in.shape[0]
    assert x_sample.shape[1] == 1 and seq % ML_CHUNK == 0

    mod_p, mod_s = _ada(c_prompt, c_sample, w_ada, b_ada)
    gp = g_pre.reshape(depth, 3, 1, D_MODEL)
    gq = g_post.reshape(depth, 3, 1, D_MODEL)

    ml_t = ml_w_in.transpose(0, 2, 1).astype(BF16)
    ml_bif = jnp.concatenate([ml_b_i, ml_b_f], axis=-1)
    ml_bif_t = jnp.broadcast_to(ml_bif[:, :, None], (n_ml, 2 * N_HEADS, MIX_ROWS))
    ml_ghead_t = jnp.broadcast_to(ml_g_head[:, :, :, None], (n_ml, N_HEADS, D_V, LANES))
    ml_out = ml_w_out.astype(BF16)
    cv_in = cv_w_in.astype(BF16)
    cv_out = cv_w_out.astype(BF16)

    xp = x_prompt
    xs = x_sample.reshape(n_sample, D_MODEL)
    p_c, p_n, p_m, p_buf, s_c, s_n, s_m, s_buf = [], [], [], [], [], [], [], []
    for l in range(depth):
        ms = lambda s: mod_s[l, 3 * s:3 * s + 3]
        xp, xs = _ffn(xp, xs, mod_p, mod_s, gp, gq, ffn_wg, ffn_wu, ffn_wd, l, 0)
        j = l // 2
        if l % 2 == 0:
            xp, c_j, n_j, m_j = _mlstm_prompt(xp, mod_p, gp, gq, l, ml_t[j], ml_bif_t[j], ml_ghead_t[j], ml_out[j])
            p_c.append(c_j.reshape(batch, N_HEADS, D_QK, D_V))
            p_n.append(n_j.reshape(batch, N_HEADS, D_QK))
            p_m.append(m_j[:, :, 0])
            ys, c_j, n_j, m_j = _mlstm_step(xs, mod_s, gp, gq, l, ml_t[j], ml_bif[j][None], ml_g_head[j], ml_out[j],
                                            state_mlstm_C[j], state_mlstm_n[j], state_mlstm_m[j])
            xs = ys
            s_c.append(c_j)
            s_n.append(n_j)
            s_m.append(m_j)
        else:
            xp, buf_j = _conv_prompt(xp, mod_p, gp, gq, l, cv_in[j], cv_conv_w[j], cv_out[j])
            p_buf.append(buf_j)
            ys, buf_j = _conv_sample(xs, ms(1), g_pre[l, 1], g_post[l, 1], cv_in[j], cv_conv_w[j], cv_out[j],
                                     state_conv[j].transpose(1, 0, 2))
            xs = ys
            s_buf.append(buf_j.transpose(1, 0, 2))
        xp, xs = _ffn(xp, xs, mod_p, mod_s, gp, gq, ffn_wg, ffn_wu, ffn_wd, l, 1)

    return (xp, xs.reshape(n_sample, 1, D_MODEL),
            jnp.stack(p_c), jnp.stack(p_n), jnp.stack(p_m), jnp.stack(p_buf),
            jnp.stack(s_c), jnp.stack(s_n), jnp.stack(s_m), jnp.stack(s_buf))
```

```python
import functools

import jax
import jax.numpy as jnp
from jax import lax
from jax.experimental import pallas as pl
from jax.experimental.pallas import tpu as pltpu

F32 = jnp.float32
BF16 = jnp.bfloat16

D_MODEL = 1024
N_HEADS = 8
D_QK = 64
D_V = 128
D_HQ = N_HEADS * D_QK
D_HV = N_HEADS * D_V
D_FF = 2816
N_ADA = 9
CONV_W = 3
GATE_SOFTCAP = 15.0
EPS = 1e-6

LANES = 128
BF16_ROWS = 16
MXU_COLS = 256
VMEM_LIMIT_BYTES = 56 * 1024 * 1024

FFN_ROWS = 512
FFN_SUB = 2 * MXU_COLS
FFN_STAGE_PIECES = 8
ADA_COLS = 3 * D_MODEL
MIX_ROWS = 1024
ML_CHUNK = LANES
ML_STATE_ROWS = D_V + BF16_ROWS
SAMPLE_BLOCK = 8


def _params(*sem):
    return pltpu.CompilerParams(dimension_semantics=sem, vmem_limit_bytes=VMEM_LIMIT_BYTES)


def _sigmoid(x):
    return 1.0 / (1.0 + jnp.exp(-x))


def _rms(x):
    return lax.rsqrt(jnp.mean(x * x, axis=-1, keepdims=True) + EPS)


def _norm_mod(x, gain, scale, shift):
    return x * _rms(x) * (gain * (1.0 + scale)) + shift


def _dot(a, b):
    return jnp.dot(a, b, preferred_element_type=F32)


def _dot_nt(a, b):
    return lax.dot_general(a, b, (((1,), (1,)), ((), ())), preferred_element_type=F32)


def _dot_tn(a, b):
    return lax.dot_general(a, b, (((0,), (0,)), ((), ())), preferred_element_type=F32)


def _log_sigmoid(x):
    return -(jnp.maximum(-x, 0.0) + jnp.log1p(jnp.exp(-jnp.abs(x))))


def _resident(shape):
    return pl.BlockSpec(shape, lambda *_: tuple(0 for _ in shape), pipeline_mode=pl.Buffered(1))


def _ada_kernel(cp_ref, cs_ref, w_ref, b_ref, op_ref, os_ref):
    w = w_ref[...].astype(BF16)
    for c_ref, o_ref in ((cp_ref, op_ref), (cs_ref, os_ref)):
        c = c_ref[...]
        res = _dot((c * _sigmoid(c)).astype(BF16), w)
        for k in range(o_ref.shape[0]):
            o_ref[k] = res[:, k * D_MODEL:(k + 1) * D_MODEL] + b_ref[k]


def _ada(c_prompt, c_sample, w_ada, b_ada):
    depth = w_ada.shape[0]
    per_step = ADA_COLS // D_MODEL
    out = lambda c: (pl.BlockSpec((None, per_step, c.shape[0], D_MODEL), lambda l, j: (l, j, 0, 0)),
                     jax.ShapeDtypeStruct((depth, N_ADA, c.shape[0], D_MODEL), F32))
    (spec_p, shape_p), (spec_s, shape_s) = out(c_prompt), out(c_sample)
    return pl.pallas_call(
        _ada_kernel,
        grid=(depth, N_ADA // per_step),
        in_specs=[
            pl.BlockSpec(c_prompt.shape, lambda l, j: (0, 0)),
            pl.BlockSpec(c_sample.shape, lambda l, j: (0, 0)),
            pl.BlockSpec((None, D_MODEL, ADA_COLS), lambda l, j: (l, 0, j)),
            pl.BlockSpec((None, per_step, 1, D_MODEL), lambda l, j: (l, j, 0, 0)),
        ],
        out_specs=[spec_p, spec_s],
        out_shape=[shape_p, shape_s],
        compiler_params=_params("arbitrary", "arbitrary"),
        name="ada",
    )(c_prompt, c_sample, w_ada, b_ada.reshape(depth, N_ADA, 1, D_MODEL))


def _group_mod(mod_ref, j):
    return mod_ref[j, pl.ds(pl.program_id(0), 1), :]


def _swiglu_chunk(hn, wg_ref, wu_ref, wd_ref, cols):
    g = _dot(hn, wg_ref[:, cols])
    u = _dot(hn, wu_ref[:, cols])
    return _dot((g * _sigmoid(g) * u).astype(BF16), wd_ref[cols, :])


def _ffn_kernel(xp_ref, xs_ref, modp_ref, mods_ref, gpre_ref, gpost_ref, wg_hbm, wu_hbm, wd_hbm,
                op_ref, os_ref,
                wg_s, wu_s, wd_s, up_stage, dn_stage, sem, hn_s, acc_s, *,
                layer, half, n_tiles, tiles_per_group, sub):
    s = pl.program_id(0)

    up_rows = up_stage.shape[2]
    dn_rows = dn_stage.shape[1]
    n_pieces = wg_s.shape[0] // up_rows

    def piece_copies(c):
        slot = c % 2
        up = pl.ds(c * up_rows, up_rows)
        dn = pl.ds(c * dn_rows, dn_rows)
        return (pltpu.make_async_copy(wg_hbm.at[layer, half, up, :], up_stage.at[slot, 0], sem.at[slot, 0]),
                pltpu.make_async_copy(wu_hbm.at[layer, half, up, :], up_stage.at[slot, 1], sem.at[slot, 1]),
                pltpu.make_async_copy(wd_hbm.at[layer, half, dn, :], dn_stage.at[slot], sem.at[slot, 2]))

    def stage_weights():
        for copy in piece_copies(0):
            copy.start()
        for c in range(n_pieces):
            if c + 1 < n_pieces:
                for copy in piece_copies(c + 1):
                    copy.start()
            for copy in piece_copies(c):
                copy.wait()
            wg_s[c * up_rows:(c + 1) * up_rows, :] = up_stage[c % 2, 0].astype(BF16)
            wu_s[c * up_rows:(c + 1) * up_rows, :] = up_stage[c % 2, 1].astype(BF16)
            wd_s[c * dn_rows:(c + 1) * dn_rows, :] = dn_stage[c % 2].astype(BF16)

    def half_step(x_ref, o_ref, scale, shift, gate):
        rows = x_ref.shape[0]
        x = x_ref[...]
        hn_s[0:rows, :] = _norm_mod(x, gpre_ref[...], scale, shift).astype(BF16)
        hn = hn_s[0:rows, :]
        d_ff = wg_s.shape[1]
        for start in range(0, d_ff, sub):
            down = _swiglu_chunk(hn, wg_s, wu_s, wd_s, slice(start, min(start + sub, d_ff)))
            if start == 0:
                acc_s[0:rows, :] = down
            else:
                acc_s[0:rows, :] += down
        y = acc_s[0:rows, :]
        o_ref[...] = x + y * _rms(y) * (gpost_ref[...] * (0.5 * (1.0 + gate)))

    group = jnp.minimum(s, n_tiles - 1) // tiles_per_group
    group_mod = lambda j: modp_ref[j, pl.ds(group, 1), :]

    @pl.when(s == 0)
    def _():
        stage_weights()

    @pl.when(s < n_tiles)
    def _():
        half_step(xp_ref, op_ref, group_mod(1), group_mod(0), group_mod(2))

    @pl.when(s == n_tiles)
    def _():
        half_step(xs_ref, os_ref, mods_ref[1], mods_ref[0], mods_ref[2])


def _ffn(xp, xs, mod_p, mod_s, g_pre, g_post, wg, wu, wd, layer, half, in_place):
    groups, seq, _ = xp.shape
    n_sample = xs.shape[0]
    rows = FFN_ROWS
    tpg = seq // rows
    n_tiles = groups * tpg
    sub = FFN_SUB
    tile = lambda s: jnp.minimum(s, n_tiles - 1)
    xp_spec = pl.BlockSpec((None, rows, D_MODEL), lambda s: (tile(s) // tpg, tile(s) % tpg, 0))
    xs_spec = pl.BlockSpec((n_sample, D_MODEL), lambda s: (0, 0))
    gain = pl.BlockSpec((None, None, 1, D_MODEL), lambda s: (layer, 2 * half, 0, 0))
    hbm = pl.BlockSpec(memory_space=pl.ANY)
    return pl.pallas_call(
        functools.partial(_ffn_kernel, layer=layer, half=half, n_tiles=n_tiles, tiles_per_group=tpg, sub=sub),
        grid=(n_tiles + 1,),
        in_specs=[
            xp_spec,
            xs_spec,
            pl.BlockSpec((None, 3, groups, D_MODEL), lambda s: (layer, 2 * half, 0, 0)),
            pl.BlockSpec((None, 3, n_sample, D_MODEL), lambda s: (layer, 2 * half, 0, 0)),
            gain,
            gain,
            hbm,
            hbm,
            hbm,
        ],
        out_specs=[xp_spec, xs_spec],
        out_shape=[jax.ShapeDtypeStruct(xp.shape, F32), jax.ShapeDtypeStruct(xs.shape, F32)],
        scratch_shapes=[
            pltpu.VMEM((D_MODEL, D_FF), BF16),
            pltpu.VMEM((D_MODEL, D_FF), BF16),
            pltpu.VMEM((D_FF, D_MODEL), BF16),
            pltpu.VMEM((2, 2, D_MODEL // FFN_STAGE_PIECES, D_FF), F32),
            pltpu.VMEM((2, D_FF // FFN_STAGE_PIECES, D_MODEL), F32),
            pltpu.SemaphoreType.DMA((2, 3)),
            pltpu.VMEM((max(rows, n_sample), D_MODEL), BF16),
            pltpu.VMEM((max(rows, n_sample), D_MODEL), F32),
        ],
        input_output_aliases={0: 0, 1: 1} if in_place else {},
        compiler_params=_params("arbitrary"),
        name="ffn",
    )(xp, xs, mod_p, mod_s, g_pre, g_post, wg, wu, wd)


def _block_diag(a, b):
    za = jnp.zeros((a.shape[0], b.shape[1]), a.dtype)
    zb = jnp.zeros((b.shape[0], a.shape[1]), a.dtype)
    return jnp.concatenate([jnp.concatenate([a, za], axis=1), jnp.concatenate([zb, b], axis=1)], axis=0)


def _mlstm_kernel(x_ref, mod_ref, gpre_ref, gpost_ref, wt_ref, bif_ref, ght_ref, wout_ref,
                  o_ref, c_out_ref, n_out_ref, m_out_ref,
                  st_s, m_s, yt_s, *, chunk):
    t = pl.program_id(1)

    @pl.when(t == 0)
    def _():
        st_s[...] = jnp.zeros_like(st_s)
        m_s[...] = jnp.zeros_like(m_s)

    x = x_ref[...]
    rows = x.shape[0]
    h_in = _norm_mod(x, gpre_ref[...], _group_mod(mod_ref, 1), _group_mod(mod_ref, 0)).astype(BF16)
    proj_q = _dot_nt(wt_ref[0:D_HQ], h_in)
    proj_t = _dot_nt(wt_ref[2 * D_HQ:], h_in)
    k_all = _dot_nt(h_in, wt_ref[D_HQ:2 * D_HQ])
    r_v, r_o, r_i, r_f = 0, D_HV, 2 * D_HV, 2 * D_HV + N_HEADS

    s_idx = lax.broadcasted_iota(jnp.int32, (chunk, chunk), 0)
    t_idx = lax.broadcasted_iota(jnp.int32, (chunk, chunk), 1)
    causal = s_idx <= t_idx
    tri = jnp.where(causal, 1.0, 0.0).astype(BF16)
    lane = lax.broadcasted_iota(jnp.int32, (1, LANES), 1)
    low = lane < D_QK
    ones_rows = jnp.ones((BF16_ROWS, 2 * chunk), BF16)
    zero_rows = jnp.zeros((LANES - N_HEADS, chunk), F32)

    ig_all = GATE_SOFTCAP * jnp.tanh((proj_t[r_i:r_i + N_HEADS, :] + bif_ref[0:N_HEADS, :]) / GATE_SOFTCAP)
    lf_all = _log_sigmoid(proj_t[r_f:r_f + N_HEADS, :] + bif_ref[N_HEADS:, :])

    for c in range(rows // chunk):
        cs = slice(c * chunk, (c + 1) * chunk)
        lf = lf_all[:, cs]
        lf_hi = lf.astype(BF16).astype(F32)
        lf_mid = (lf - lf_hi).astype(BF16).astype(F32)
        lf_lo = lf - lf_hi - lf_mid
        b3 = _dot(jnp.concatenate([lf_hi, lf_mid, lf_lo, jnp.zeros_like(lf)], axis=0).astype(BF16), tri)
        b = b3[0:N_HEADS] + b3[N_HEADS:2 * N_HEADS] + b3[2 * N_HEADS:3 * N_HEADS]
        col = ig_all[:, cs] - b
        b_last = jnp.broadcast_to(b[:, chunk - 1:chunk], b.shape)
        col_s = jnp.concatenate([col, zero_rows], axis=0).T

        for p in range(N_HEADS // 2):
            h0, h1 = 2 * p, 2 * p + 1
            qt = [(proj_q[h * D_QK:(h + 1) * D_QK, cs] * (D_QK ** -0.5)).astype(BF16) for h in (h0, h1)]
            vt = [proj_t[r_v + h * D_V:r_v + (h + 1) * D_V, cs] for h in (h0, h1)]
            k_pair = k_all[cs, p * LANES:(p + 1) * LANES]
            state = st_s[p]
            lhs1 = jnp.concatenate([k_pair.astype(BF16), state.astype(BF16)], axis=0)
            r1 = _dot(lhs1, _block_diag(qt[0], qt[1]))
            probs, w_state, inv_floor, w_k, w_decay = [], [], [], [], []
            for i, h in enumerate((h0, h1)):
                m_prev = m_s[h:h + 1, :]
                col_m = jnp.where(causal, col_s[:, h:h + 1], -jnp.inf)
                g = jnp.maximum(m_prev, jnp.max(col_m, axis=0, keepdims=True))
                probs.append((jnp.exp(col_m - g) * r1[:chunk, i * chunk:(i + 1) * chunk]).astype(BF16))
                m_t = b[h:h + 1, :] + g
                m_new = jnp.broadcast_to(m_t[:, chunk - 1:chunk], m_t.shape)
                w_state.append(jnp.exp(m_prev - g))
                inv_floor.append(jnp.exp(-m_t))
                w_decay.append(jnp.exp(b_last[h:h + 1, :] + m_prev - m_new))
                w_k.append(jnp.exp(b_last[h:h + 1, :] - m_new + col[h:h + 1, :]))
                m_s[h:h + 1, :] = m_new
            lhs2 = jnp.concatenate([jnp.concatenate([vt[0], vt[1]], axis=1).astype(BF16), ones_rows], axis=0)
            r2 = _dot(lhs2, _block_diag(probs[0], probs[1]))
            for i, h in enumerate((h0, h1)):
                ls = slice(i * chunk, (i + 1) * chunk)
                den = w_state[i] * r1[chunk + D_V:chunk + D_V + 1, ls] + r2[D_V:D_V + 1, ls]
                inv = 1.0 / jnp.maximum(jnp.abs(den), inv_floor[i])
                ht = (w_state[i] * r1[chunk:chunk + D_V, ls] + r2[:D_V, ls]) * inv
                rn = lax.rsqrt(jnp.mean(ht * ht, axis=0, keepdims=True) + EPS)
                og = proj_t[r_o + h * D_V:r_o + (h + 1) * D_V, cs]
                yt_s[h * D_V:(h + 1) * D_V, cs] = (_sigmoid(og) * (ht * rn * ght_ref[h])).astype(BF16)
            lhs3 = jnp.concatenate(
                [jnp.concatenate([vt[i] * w_k[i] for i in range(2)], axis=1),
                 jnp.concatenate([jnp.broadcast_to(w_k[i], (BF16_ROWS, chunk)) for i in range(2)], axis=1)],
                axis=0).astype(BF16)
            rhs3 = jnp.concatenate([jnp.where(low, k_pair, 0.0), jnp.where(low, 0.0, k_pair)], axis=0).astype(BF16)
            decay = jnp.where(low, w_decay[0], w_decay[1])
            st_s[p] = state * decay + _dot(lhs3, rhs3)

    y = _dot_tn(yt_s[...], wout_ref[...])
    o_ref[...] = x + y * _rms(y) * (gpost_ref[...] * (1.0 + _group_mod(mod_ref, 2)))

    @pl.when(t == pl.num_programs(1) - 1)
    def _():
        for p in range(N_HEADS // 2):
            state = st_s[p]
            c_out_ref[p * LANES:(p + 1) * LANES, :] = state[:D_V, :].T
            n_out_ref[:, p * LANES:(p + 1) * LANES] = state[D_V:D_V + 1, :]
        m_out_ref[...] = m_s[...]


def _mixer_mod_specs(layer, batch):
    gain = pl.BlockSpec((None, None, 1, D_MODEL), lambda b, t: (layer, 1, 0, 0))
    return [pl.BlockSpec((None, 3, batch, D_MODEL), lambda b, t: (layer, 1, 0, 0)), gain, gain]


def _mlstm_prompt(x, mod, g_pre, g_post, layer, w_t, b_if, g_head_t, w_out):
    batch, seq, _ = x.shape
    rows = MIX_ROWS
    return pl.pallas_call(
        functools.partial(_mlstm_kernel, chunk=ML_CHUNK),
        grid=(batch, seq // rows),
        in_specs=[
            pl.BlockSpec((None, rows, D_MODEL), lambda b, t: (b, t, 0)),
            *_mixer_mod_specs(layer, batch),
            _resident(w_t.shape),
            _resident(b_if.shape),
            _resident(g_head_t.shape),
            _resident(w_out.shape),
        ],
        out_specs=[
            pl.BlockSpec((None, rows, D_MODEL), lambda b, t: (b, t, 0)),
            pl.BlockSpec((None, D_HQ, D_V), lambda b, t: (b, 0, 0)),
            pl.BlockSpec((None, 1, D_HQ), lambda b, t: (b, 0, 0)),
            pl.BlockSpec((None, N_HEADS, LANES), lambda b, t: (b, 0, 0)),
        ],
        out_shape=[
            jax.ShapeDtypeStruct(x.shape, F32),
            jax.ShapeDtypeStruct((batch, D_HQ, D_V), F32),
            jax.ShapeDtypeStruct((batch, 1, D_HQ), F32),
            jax.ShapeDtypeStruct((batch, N_HEADS, LANES), F32),
        ],
        scratch_shapes=[
            pltpu.VMEM((N_HEADS // 2, ML_STATE_ROWS, LANES), F32),
            pltpu.VMEM((N_HEADS, LANES), F32),
            pltpu.VMEM((D_HV, rows), BF16),
        ],
        input_output_aliases={0: 0},
        compiler_params=_params("arbitrary", "arbitrary"),
        name="mlstm_prompt",
    )(x, mod, g_pre, g_post, w_t, b_if, g_head_t, w_out)


def _pair_expand(cols):
    lane = lax.broadcasted_iota(jnp.int32, (cols[0].shape[0], LANES), 1)
    return jnp.concatenate(
        [jnp.where(lane < D_QK, cols[2 * p], cols[2 * p + 1]) for p in range(N_HEADS // 2)], axis=1)


def _row_to_head_tile(rows, b):
    return jnp.concatenate([rows[b:b + 1, h * D_V:(h + 1) * D_V] for h in range(N_HEADS)], axis=0)


def _mlstm_step_kernel(x_ref, mod_ref, gpre_ref, gpost_ref, wt_ref, bif_ref, ghead_ref, wout_ref,
                       c_ref, n_ref, m_ref,
                       o_ref, c_out_ref, n_out_ref, m_out_ref,
                       v_s, og_s, a_s, b_s, y_s, qt_s, kwt_s, wst_s):
    i = pl.program_id(0)
    blk = c_ref.shape[0]
    rows = x_ref.shape[0]

    @pl.when(i == 0)
    def _():
        x = x_ref[...]
        h_in = _norm_mod(x, gpre_ref[...], mod_ref[1], mod_ref[0]).astype(BF16)
        proj = _dot_nt(h_in, wt_ref[...])
        gates = proj[:, 2 * D_HQ + 2 * D_HV:] + bif_ref[...]
        ig = (GATE_SOFTCAP * jnp.tanh(gates / GATE_SOFTCAP))[:, :N_HEADS]
        lf = _log_sigmoid(gates)[:, N_HEADS:]
        m_prev = m_ref[...]
        st = lf + m_prev
        m_t = jnp.maximum(st, ig)
        w_i = jnp.exp(ig - m_t)
        w_s = jnp.exp(st - m_t)
        floor = jnp.exp(-m_t)
        m_out_ref[...] = m_t
        q = proj[:, :D_HQ] * (D_QK ** -0.5)
        k = proj[:, D_HQ:2 * D_HQ]
        n_prev = n_ref[...]
        lane = lax.broadcasted_iota(jnp.int32, (rows, LANES), 1)
        lo = lane < D_QK

        def head_sums(z):
            out = []
            for p in range(N_HEADS // 2):
                zp = z[:, p * LANES:(p + 1) * LANES]
                out.append(jnp.sum(jnp.where(lo, zp, 0.0), axis=-1, keepdims=True))
                out.append(jnp.sum(jnp.where(lo, 0.0, zp), axis=-1, keepdims=True))
            return out

        qk_h = head_sums(q * k)
        qn_h = head_sums(q * n_prev)
        ws_cols, wi_cols = [], []
        for hd in range(N_HEADS):
            ws_h = w_s[:, hd:hd + 1]
            wi_h = w_i[:, hd:hd + 1]
            s_h = qk_h[hd] * wi_h
            den = ws_h * qn_h[hd] + s_h
            inv = 1.0 / jnp.maximum(jnp.abs(den), floor[:, hd:hd + 1])
            a_s[:, hd * D_V:(hd + 1) * D_V] = jnp.broadcast_to(ws_h * inv, (rows, D_V))
            b_s[:, hd * D_V:(hd + 1) * D_V] = jnp.broadcast_to(s_h * inv, (rows, D_V))
            ws_cols.append(ws_h)
            wi_cols.append(wi_h)
        ws_x = _pair_expand(ws_cols)
        kw = k * _pair_expand(wi_cols)
        n_out_ref[...] = ws_x * n_prev + kw
        v_s[...] = proj[:, 2 * D_HQ:2 * D_HQ + D_HV]
        og_s[...] = proj[:, 2 * D_HQ + D_HV:2 * D_HQ + 2 * D_HV]
        for dst, src in ((qt_s, q), (kwt_s, kw), (wst_s, ws_x)):
            src_t = src.T
            for j in range(rows // blk):
                dst[j] = src_t[:, j * blk:(j + 1) * blk]

    block_rows = pl.ds(pl.multiple_of(i * blk, blk), blk)
    v_blk, og_blk, a_blk, b_blk = v_s[block_rows, :], og_s[block_rows, :], a_s[block_rows, :], b_s[block_rows, :]
    y_tiles = []
    for b in range(blk):
        v_i = _row_to_head_tile(v_blk, b)
        readout = []
        for hd in range(N_HEADS):
            rs = slice(hd * D_QK, (hd + 1) * D_QK)
            c_h = c_ref[b, hd]
            readout.append(jnp.sum(c_h * qt_s[i, rs, b:b + 1], axis=0, keepdims=True))
            c_out_ref[b, hd] = c_h * wst_s[i, rs, b:b + 1] + kwt_s[i, rs, b:b + 1] * v_i[hd:hd + 1, :]
        hh = _row_to_head_tile(a_blk, b) * jnp.concatenate(readout, axis=0) + _row_to_head_tile(b_blk, b) * v_i
        y_tiles.append(_sigmoid(_row_to_head_tile(og_blk, b)) * (hh * _rms(hh) * ghead_ref[...]))
    y_s[block_rows, :] = jnp.concatenate(
        [jnp.concatenate([y_tiles[b][hd:hd + 1, :] for b in range(blk)], axis=0) for hd in range(N_HEADS)], axis=1)

    @pl.when(i == pl.num_programs(0) - 1)
    def _():
        y = _dot(y_s[...].astype(BF16), wout_ref[...])
        o_ref[...] = x_ref[...] + y * _rms(y) * (gpost_ref[...] * (1.0 + mod_ref[2]))


def _mlstm_step(x, mod, g_pre, g_post, layer, w_t, b_if, g_head, w_out, c0, n0, m0):
    rows = x.shape[0]
    blk = SAMPLE_BLOCK
    nblk = rows // blk
    const = lambda shape: pl.BlockSpec(shape, lambda i: tuple(0 for _ in shape))
    gain = pl.BlockSpec((None, None, 1, D_MODEL), lambda i: (layer, 1, 0, 0))
    c_spec = pl.BlockSpec((blk, N_HEADS, D_QK, D_V), lambda i: (i, 0, 0, 0))
    n_flat = n0.reshape(rows, D_HQ)
    y, c_new, n_new, m_new = pl.pallas_call(
        _mlstm_step_kernel,
        grid=(nblk,),
        in_specs=[
            const(x.shape),
            pl.BlockSpec((None, 3, rows, D_MODEL), lambda i: (layer, 1, 0, 0)),
            gain,
            gain,
            _resident(w_t.shape),
            const(b_if.shape),
            const(g_head.shape),
            _resident(w_out.shape),
            c_spec,
            const(n_flat.shape),
            const(m0.shape),
        ],
        out_specs=[const(x.shape), c_spec, const(n_flat.shape), const(m0.shape)],
        out_shape=[jax.ShapeDtypeStruct(x.shape, F32), jax.ShapeDtypeStruct(c0.shape, F32),
                   jax.ShapeDtypeStruct(n_flat.shape, F32), jax.ShapeDtypeStruct(m0.shape, F32)],
        scratch_shapes=[pltpu.VMEM((rows, D_HV), F32)] * 5 + [pltpu.VMEM((nblk, D_HQ, blk), F32)] * 3,
        compiler_params=_params("arbitrary"),
        name="mlstm_step",
    )(x, mod, g_pre, g_post, w_t, b_if, g_head, w_out, c0, n_flat, m0)
    return y, c_new, n_new.reshape(rows, N_HEADS, D_QK), m_new


def _conv_kernel(x_ref, mod_ref, gpre_ref, gpost_ref, win_ref, cw_ref, wout_ref, o_ref, buf_out_ref, tail_s):
    t = pl.program_id(1)

    @pl.when(t == 0)
    def _():
        tail_s[...] = jnp.zeros_like(tail_s)

    x = x_ref[...]
    rows = x.shape[0]
    h_in = _norm_mod(x, gpre_ref[...], _group_mod(mod_ref, 1), _group_mod(mod_ref, 0)).astype(BF16)
    p = _dot(h_in, win_ref[...])
    bg = p[:, :D_MODEL]
    u = p[:, D_MODEL:2 * D_MODEL] * p[:, 2 * D_MODEL:]
    prev1 = tail_s[7:8, :]
    prev2 = tail_s[6:7, :]
    ridx = lax.broadcasted_iota(jnp.int32, u.shape, 0)
    u1 = jnp.where(ridx == 0, prev1, pltpu.roll(u, 1, 0))
    u2 = jnp.where(ridx == 0, prev2, jnp.where(ridx == 1, prev1, pltpu.roll(u, 2, 0)))
    conv = cw_ref[0:1, :] * u2 + cw_ref[1:2, :] * u1 + cw_ref[2:3, :] * u
    tail_s[...] = u[rows - 8:, :]
    y = _dot((bg * conv).astype(BF16), wout_ref[...])
    o_ref[...] = x + y * _rms(y) * (gpost_ref[...] * (1.0 + _group_mod(mod_ref, 2)))

    @pl.when(t == pl.num_programs(1) - 1)
    def _():
        buf_out_ref[...] = u[rows - (CONV_W - 1):, :]


def _conv_prompt(x, mod, g_pre, g_post, layer, w_in, conv_w, w_out):
    batch, seq, _ = x.shape
    rows = MIX_ROWS
    return pl.pallas_call(
        _conv_kernel,
        grid=(batch, seq // rows),
        in_specs=[
            pl.BlockSpec((None, rows, D_MODEL), lambda b, t: (b, t, 0)),
            *_mixer_mod_specs(layer, batch),
            _resident(w_in.shape),
            _resident(conv_w.shape),
            _resident(w_out.shape),
        ],
        out_specs=[
            pl.BlockSpec((None, rows, D_MODEL), lambda b, t: (b, t, 0)),
            pl.BlockSpec((None, CONV_W - 1, D_MODEL), lambda b, t: (b, 0, 0)),
        ],
        out_shape=[jax.ShapeDtypeStruct(x.shape, F32),
                   jax.ShapeDtypeStruct((batch, CONV_W - 1, D_MODEL), F32)],
        scratch_shapes=[pltpu.VMEM((8, D_MODEL), F32)],
        input_output_aliases={0: 0},
        compiler_params=_params("arbitrary", "arbitrary"),
        name="conv_prompt",
    )(x, mod, g_pre, g_post, w_in, conv_w, w_out)


def _conv_sample_kernel(x_ref, mod_ref, gpre_ref, gpost_ref, win_ref, cw_ref, wout_ref, buf_ref,
                        o_ref, buf_out_ref):
    x = x_ref[...]
    h_in = _norm_mod(x, gpre_ref[...], mod_ref[1], mod_ref[0]).astype(BF16)
    p = _dot(h_in, win_ref[...])
    bg = p[:, :D_MODEL]
    u = p[:, D_MODEL:2 * D_MODEL] * p[:, 2 * D_MODEL:]
    conv = cw_ref[0:1, :] * buf_ref[0] + cw_ref[1:2, :] * buf_ref[1] + cw_ref[2:3, :] * u
    y = _dot((bg * conv).astype(BF16), wout_ref[...])
    o_ref[...] = x + (1.0 + mod_ref[2]) * (y * _rms(y) * gpost_ref[...])
    buf_out_ref[0] = buf_ref[1]
    buf_out_ref[1] = u


def _conv_sample(x, mod, g_pre, g_post, w_in, conv_w, w_out, buf):
    full = lambda shape: pl.BlockSpec(shape, lambda: tuple(0 for _ in shape))
    ins = (x, mod, g_pre.reshape(1, D_MODEL), g_post.reshape(1, D_MODEL), w_in, conv_w, w_out, buf)
    return pl.pallas_call(
        _conv_sample_kernel,
        in_specs=[full(z.shape) for z in ins],
        out_specs=[full(x.shape), full(buf.shape)],
        out_shape=[jax.ShapeDtypeStruct(x.shape, F32), jax.ShapeDtypeStruct(buf.shape, F32)],
        compiler_params=_params(),
        name="conv_sample",
    )(*ins)


def kernel(x_prompt, x_sample, c_prompt, c_sample, state_mlstm_C, state_mlstm_n, state_mlstm_m, state_conv,
           w_ada, b_ada, g_pre, g_post, ffn_wg, ffn_wu, ffn_wd,
           ml_w_in, ml_b_i, ml_b_f, ml_g_head, ml_w_out, cv_w_in, cv_conv_w, cv_w_out):
    depth = w_ada.shape[0]
    batch, seq, _ = x_prompt.shape
    n_sample = x_sample.shape[0]
    n_ml = ml_w_in.shape[0]
    assert x_sample.shape[1] == 1 and seq % ML_CHUNK == 0

    mod_p, mod_s = _ada(c_prompt, c_sample, w_ada, b_ada)
    gp = g_pre.reshape(depth, 3, 1, D_MODEL)
    gq = g_post.reshape(depth, 3, 1, D_MODEL)

    ml_t = ml_w_in.transpose(0, 2, 1).astype(BF16)
    ml_bif = jnp.concatenate([ml_b_i, ml_b_f], axis=-1)
    ml_bif_t = jnp.broadcast_to(ml_bif[:, :, None], (n_ml, 2 * N_HEADS, MIX_ROWS))
    ml_ghead_t = jnp.broadcast_to(ml_g_head[:, :, :, None], (n_ml, N_HEADS, D_V, LANES))
    ml_out = ml_w_out.astype(BF16)
    cv_in = cv_w_in.astype(BF16)
    cv_out = cv_w_out.astype(BF16)

    xp = x_prompt
    xs = x_sample.reshape(n_sample, D_MODEL)
    p_c, p_n, p_m, p_buf, s_c, s_n, s_m, s_buf = [], [], [], [], [], [], [], []
    for l in range(depth):
        ms = lambda s: mod_s[l, 3 * s:3 * s + 3]
        xp, xs = _ffn(xp, xs, mod_p, mod_s, gp, gq, ffn_wg, ffn_wu, ffn_wd, l, 0, in_place=l > 0)
        j = l // 2
        if l % 2 == 0:
            xp, c_j, n_j, m_j = _mlstm_prompt(xp, mod_p, gp, gq, l, ml_t[j], ml_bif_t[j], ml_ghead_t[j], ml_out[j])
            p_c.append(c_j.reshape(batch, N_HEADS, D_QK, D_V))
            p_n.append(n_j.reshape(batch, N_HEADS, D_QK))
            p_m.append(m_j[:, :, 0])
            ys, c_j, n_j, m_j = _mlstm_step(xs, mod_s, gp, gq, l, ml_t[j], ml_bif[j][None], ml_g_head[j], ml_out[j],
                                            state_mlstm_C[j], state_mlstm_n[j], state_mlstm_m[j])
            xs = ys
            s_c.append(c_j)
            s_n.append(n_j)
            s_m.append(m_j)
        else:
            xp, buf_j = _conv_prompt(xp, mod_p, gp, gq, l, cv_in[j], cv_conv_w[j], cv_out[j])
            p_buf.append(buf_j)
            ys, buf_j = _conv_sample(xs, ms(1), g_pre[l, 1], g_post[l, 1], cv_in[j], cv_conv_w[j], cv_out[j],
                                     state_conv[j].transpose(1, 0, 2))
            xs = ys
            s_buf.append(buf_j.transpose(1, 0, 2))
        xp, xs = _ffn(xp, xs, mod_p, mod_s, gp, gq, ffn_wg, ffn_wu, ffn_wd, l, 1, in_place=True)

    return (xp, xs.reshape(n_sample, 1, D_MODEL),
            jnp.stack(p_c), jnp.stack(p_n), jnp.stack(p_m), jnp.stack(p_buf),
            jnp.stack(s_c), jnp.stack(s_n), jnp.stack(s_m), jnp.stack(s_buf))
```

```python
import functools

import jax
import jax.numpy as jnp
from jax import lax
from jax.experimental import pallas as pl
from jax.experimental.pallas import tpu as pltpu

F32 = jnp.float32
BF16 = jnp.bfloat16

D_MODEL = 1024
N_HEADS = 8
D_QK = 64
D_V = 128
D_HQ = N_HEADS * D_QK
D_HV = N_HEADS * D_V
D_FF = 2816
N_ADA = 9
CONV_W = 3
GATE_SOFTCAP = 15.0
EPS = 1e-6

LANES = 128
BF16_ROWS = 16
MXU_COLS = 256
VMEM_LIMIT_BYTES = 56 * 1024 * 1024

FFN_ROWS = 512
FFN_SUB = MXU_COLS
ADA_COLS = 3 * D_MODEL
MIX_ROWS = 1024
ML_CHUNK = LANES
ML_STATE_ROWS = D_V + BF16_ROWS
SAMPLE_BLOCK = 8


def _params(*sem):
    return pltpu.CompilerParams(dimension_semantics=sem, vmem_limit_bytes=VMEM_LIMIT_BYTES)


def _sigmoid(x):
    return 1.0 / (1.0 + jnp.exp(-x))


def _rms(x):
    return lax.rsqrt(jnp.mean(x * x, axis=-1, keepdims=True) + EPS)


def _norm_mod(x, gain, scale, shift):
    return x * _rms(x) * (gain * (1.0 + scale)) + shift


def _dot(a, b):
    return jnp.dot(a, b, preferred_element_type=F32)


def _dot_nt(a, b):
    return lax.dot_general(a, b, (((1,), (1,)), ((), ())), preferred_element_type=F32)


def _dot_tn(a, b):
    return lax.dot_general(a, b, (((0,), (0,)), ((), ())), preferred_element_type=F32)


def _log_sigmoid(x):
    return -(jnp.maximum(-x, 0.0) + jnp.log1p(jnp.exp(-jnp.abs(x))))


def _resident(shape):
    return pl.BlockSpec(shape, lambda *_: tuple(0 for _ in shape), pipeline_mode=pl.Buffered(1))


def _ada_kernel(cp_ref, cs_ref, w_ref, b_ref, op_ref, os_ref):
    w = w_ref[...].astype(BF16)
    for c_ref, o_ref in ((cp_ref, op_ref), (cs_ref, os_ref)):
        c = c_ref[...]
        res = _dot((c * _sigmoid(c)).astype(BF16), w)
        for k in range(o_ref.shape[0]):
            o_ref[k] = res[:, k * D_MODEL:(k + 1) * D_MODEL] + b_ref[k]


def _ada(c_prompt, c_sample, w_ada, b_ada):
    depth = w_ada.shape[0]
    per_step = ADA_COLS // D_MODEL
    out = lambda c: (pl.BlockSpec((None, per_step, c.shape[0], D_MODEL), lambda l, j: (l, j, 0, 0)),
                     jax.ShapeDtypeStruct((depth, N_ADA, c.shape[0], D_MODEL), F32))
    (spec_p, shape_p), (spec_s, shape_s) = out(c_prompt), out(c_sample)
    return pl.pallas_call(
        _ada_kernel,
        grid=(depth, N_ADA // per_step),
        in_specs=[
            pl.BlockSpec(c_prompt.shape, lambda l, j: (0, 0)),
            pl.BlockSpec(c_sample.shape, lambda l, j: (0, 0)),
            pl.BlockSpec((None, D_MODEL, ADA_COLS), lambda l, j: (l, 0, j)),
            pl.BlockSpec((None, per_step, 1, D_MODEL), lambda l, j: (l, j, 0, 0)),
        ],
        out_specs=[spec_p, spec_s],
        out_shape=[shape_p, shape_s],
        compiler_params=_params("arbitrary", "arbitrary"),
        name="ada",
    )(c_prompt, c_sample, w_ada, b_ada.reshape(depth, N_ADA, 1, D_MODEL))


def _group_mod(mod_ref, j):
    return mod_ref[j, pl.ds(pl.program_id(0), 1), :]


def _swiglu_chunk(hn, wg_ref, wu_ref, wd_ref, c, sub):
    cols = slice(c * sub, (c + 1) * sub)
    g = _dot(hn, wg_ref[:, cols])
    u = _dot(hn, wu_ref[:, cols])
    return _dot((g * _sigmoid(g) * u).astype(BF16), wd_ref[cols, :])


def _ffn_kernel(xp_ref, xs_ref, modp_ref, mods_ref, gpre_ref, gpost_ref, wg_hbm, wu_hbm, wd_hbm,
                op_ref, os_ref,
                wg_s, wu_s, wd_s, up_stage, dn_stage, sem, hn_s, acc_s, *,
                layer, half, n_tiles, tiles_per_group, sub):
    s = pl.program_id(0)
    n_chunks = wg_s.shape[1] // sub

    def chunk_copies(c):
        slot = c % 2
        cols = pl.ds(c * sub, sub)
        return (pltpu.make_async_copy(wg_hbm.at[layer, half, :, cols], up_stage.at[slot, 0], sem.at[slot, 0]),
                pltpu.make_async_copy(wu_hbm.at[layer, half, :, cols], up_stage.at[slot, 1], sem.at[slot, 1]),
                pltpu.make_async_copy(wd_hbm.at[layer, half, cols, :], dn_stage.at[slot], sem.at[slot, 2]))

    def stage_chunk(c):
        if c + 1 < n_chunks:
            for copy in chunk_copies(c + 1):
                copy.start()
        for copy in chunk_copies(c):
            copy.wait()
        cols = slice(c * sub, (c + 1) * sub)
        wg_s[:, cols] = up_stage[c % 2, 0].astype(BF16)
        wu_s[:, cols] = up_stage[c % 2, 1].astype(BF16)
        wd_s[cols, :] = dn_stage[c % 2].astype(BF16)

    def half_step(x_ref, o_ref, scale, shift, gate, stage_weights=False):
        rows = x_ref.shape[0]
        x = x_ref[...]
        hn_s[0:rows, :] = _norm_mod(x, gpre_ref[...], scale, shift).astype(BF16)
        hn = hn_s[0:rows, :]
        for c in range(n_chunks):
            if stage_weights:
                stage_chunk(c)
            down = _swiglu_chunk(hn, wg_s, wu_s, wd_s, c, sub)
            if c == 0:
                acc_s[0:rows, :] = down
            else:
                acc_s[0:rows, :] += down
        y = acc_s[0:rows, :]
        o_ref[...] = x + y * _rms(y) * (gpost_ref[...] * (0.5 * (1.0 + gate)))

    group = jnp.minimum(s, n_tiles - 1) // tiles_per_group
    group_mod = lambda j: modp_ref[j, pl.ds(group, 1), :]

    @pl.when(s == 0)
    def _():
        for copy in chunk_copies(0):
            copy.start()
        half_step(xp_ref, op_ref, group_mod(1), group_mod(0), group_mod(2), stage_weights=True)

    @pl.when((s > 0) & (s < n_tiles))
    def _():
        half_step(xp_ref, op_ref, group_mod(1), group_mod(0), group_mod(2))

    @pl.when(s == n_tiles)
    def _():
        half_step(xs_ref, os_ref, mods_ref[1], mods_ref[0], mods_ref[2])


def _ffn(xp, xs, mod_p, mod_s, g_pre, g_post, wg, wu, wd, layer, half):
    groups, seq, _ = xp.shape
    n_sample = xs.shape[0]
    rows = FFN_ROWS
    tpg = seq // rows
    n_tiles = groups * tpg
    sub = FFN_SUB
    tile = lambda s: jnp.minimum(s, n_tiles - 1)
    xp_spec = pl.BlockSpec((None, rows, D_MODEL), lambda s: (tile(s) // tpg, tile(s) % tpg, 0))
    xs_spec = pl.BlockSpec((n_sample, D_MODEL), lambda s: (0, 0))
    gain = pl.BlockSpec((None, None, 1, D_MODEL), lambda s: (layer, 2 * half, 0, 0))
    hbm = pl.BlockSpec(memory_space=pl.ANY)
    return pl.pallas_call(
        functools.partial(_ffn_kernel, layer=layer, half=half, n_tiles=n_tiles, tiles_per_group=tpg, sub=sub),
        grid=(n_tiles + 1,),
        in_specs=[
            xp_spec,
            xs_spec,
            pl.BlockSpec((None, 3, groups, D_MODEL), lambda s: (layer, 2 * half, 0, 0)),
            pl.BlockSpec((None, 3, n_sample, D_MODEL), lambda s: (layer, 2 * half, 0, 0)),
            gain,
            gain,
            hbm,
            hbm,
            hbm,
        ],
        out_specs=[xp_spec, xs_spec],
        out_shape=[jax.ShapeDtypeStruct(xp.shape, F32), jax.ShapeDtypeStruct(xs.shape, F32)],
        scratch_shapes=[
            pltpu.VMEM((D_MODEL, D_FF), BF16),
            pltpu.VMEM((D_MODEL, D_FF), BF16),
            pltpu.VMEM((D_FF, D_MODEL), BF16),
            pltpu.VMEM((2, 2, D_MODEL, sub), F32),
            pltpu.VMEM((2, sub, D_MODEL), F32),
            pltpu.SemaphoreType.DMA((2, 3)),
            pltpu.VMEM((max(rows, n_sample), D_MODEL), BF16),
            pltpu.VMEM((max(rows, n_sample), D_MODEL), F32),
        ],
        compiler_params=_params("arbitrary"),
        name="ffn",
    )(xp, xs, mod_p, mod_s, g_pre, g_post, wg, wu, wd)


def _block_diag(a, b):
    za = jnp.zeros((a.shape[0], b.shape[1]), a.dtype)
    zb = jnp.zeros((b.shape[0], a.shape[1]), a.dtype)
    return jnp.concatenate([jnp.concatenate([a, za], axis=1), jnp.concatenate([zb, b], axis=1)], axis=0)


def _mlstm_kernel(x_ref, mod_ref, gpre_ref, gpost_ref, wt_ref, bif_ref, ght_ref, wout_ref,
                  o_ref, c_out_ref, n_out_ref, m_out_ref,
                  st_s, m_s, yt_s, *, chunk):
    t = pl.program_id(1)

    @pl.when(t == 0)
    def _():
        st_s[...] = jnp.zeros_like(st_s)
        m_s[...] = jnp.zeros_like(m_s)

    x = x_ref[...]
    rows = x.shape[0]
    h_in = _norm_mod(x, gpre_ref[...], _group_mod(mod_ref, 1), _group_mod(mod_ref, 0)).astype(BF16)
    proj_q = _dot_nt(wt_ref[0:D_HQ], h_in)
    proj_t = _dot_nt(wt_ref[2 * D_HQ:], h_in)
    k_all = _dot_nt(h_in, wt_ref[D_HQ:2 * D_HQ])
    r_v, r_o, r_i, r_f = 0, D_HV, 2 * D_HV, 2 * D_HV + N_HEADS

    s_idx = lax.broadcasted_iota(jnp.int32, (chunk, chunk), 0)
    t_idx = lax.broadcasted_iota(jnp.int32, (chunk, chunk), 1)
    causal = s_idx <= t_idx
    tri = jnp.where(causal, 1.0, 0.0).astype(BF16)
    lane = lax.broadcasted_iota(jnp.int32, (1, LANES), 1)
    low = lane < D_QK
    ones_rows = jnp.ones((BF16_ROWS, 2 * chunk), BF16)
    zero_rows = jnp.zeros((LANES - N_HEADS, chunk), F32)

    ig_all = GATE_SOFTCAP * jnp.tanh((proj_t[r_i:r_i + N_HEADS, :] + bif_ref[0:N_HEADS, :]) / GATE_SOFTCAP)
    lf_all = _log_sigmoid(proj_t[r_f:r_f + N_HEADS, :] + bif_ref[N_HEADS:, :])

    for c in range(rows // chunk):
        cs = slice(c * chunk, (c + 1) * chunk)
        lf = lf_all[:, cs]
        lf_hi = lf.astype(BF16).astype(F32)
        lf_mid = (lf - lf_hi).astype(BF16).astype(F32)
        lf_lo = lf - lf_hi - lf_mid
        b3 = _dot(jnp.concatenate([lf_hi, lf_mid, lf_lo, jnp.zeros_like(lf)], axis=0).astype(BF16), tri)
        b = b3[0:N_HEADS] + b3[N_HEADS:2 * N_HEADS] + b3[2 * N_HEADS:3 * N_HEADS]
        col = ig_all[:, cs] - b
        b_last = jnp.broadcast_to(b[:, chunk - 1:chunk], b.shape)
        col_s = jnp.concatenate([col, zero_rows], axis=0).T

        for p in range(N_HEADS // 2):
            h0, h1 = 2 * p, 2 * p + 1
            qt = [(proj_q[h * D_QK:(h + 1) * D_QK, cs] * (D_QK ** -0.5)).astype(BF16) for h in (h0, h1)]
            vt = [proj_t[r_v + h * D_V:r_v + (h + 1) * D_V, cs] for h in (h0, h1)]
            k_pair = k_all[cs, p * LANES:(p + 1) * LANES]
            state = st_s[p]
            lhs1 = jnp.concatenate([k_pair.astype(BF16), state.astype(BF16)], axis=0)
            r1 = _dot(lhs1, _block_diag(qt[0], qt[1]))
            probs, w_state, inv_floor, w_k, w_decay = [], [], [], [], []
            for i, h in enumerate((h0, h1)):
                m_prev = m_s[h:h + 1, :]
                col_m = jnp.where(causal, col_s[:, h:h + 1], -jnp.inf)
                g = jnp.maximum(m_prev, jnp.max(col_m, axis=0, keepdims=True))
                probs.append((jnp.exp(col_m - g) * r1[:chunk, i * chunk:(i + 1) * chunk]).astype(BF16))
                m_t = b[h:h + 1, :] + g
                m_new = jnp.broadcast_to(m_t[:, chunk - 1:chunk], m_t.shape)
                w_state.append(jnp.exp(m_prev - g))
                inv_floor.append(jnp.exp(-m_t))
                w_decay.append(jnp.exp(b_last[h:h + 1, :] + m_prev - m_new))
                w_k.append(jnp.exp(b_last[h:h + 1, :] - m_new + col[h:h + 1, :]))
                m_s[h:h + 1, :] = m_new
            lhs2 = jnp.concatenate([jnp.concatenate([vt[0], vt[1]], axis=1).astype(BF16), ones_rows], axis=0)
            r2 = _dot(lhs2, _block_diag(probs[0], probs[1]))
            for i, h in enumerate((h0, h1)):
                ls = slice(i * chunk, (i + 1) * chunk)
                den = w_state[i] * r1[chunk + D_V:chunk + D_V + 1, ls] + r2[D_V:D_V + 1, ls]
                inv = 1.0 / jnp.maximum(jnp.abs(den), inv_floor[i])
                ht = (w_state[i] * r1[chunk:chunk + D_V, ls] + r2[:D_V, ls]) * inv
                rn = lax.rsqrt(jnp.mean(ht * ht, axis=0, keepdims=True) + EPS)
                og = proj_t[r_o + h * D_V:r_o + (h + 1) * D_V, cs]
                yt_s[h * D_V:(h + 1) * D_V, cs] = (_sigmoid(og) * (ht * rn * ght_ref[h])).astype(BF16)
            lhs3 = jnp.concatenate(
                [jnp.concatenate([vt[i] * w_k[i] for i in range(2)], axis=1),
                 jnp.concatenate([jnp.broadcast_to(w_k[i], (BF16_ROWS, chunk)) for i in range(2)], axis=1)],
                axis=0).astype(BF16)
            rhs3 = jnp.concatenate([jnp.where(low, k_pair, 0.0), jnp.where(low, 0.0, k_pair)], axis=0).astype(BF16)
            decay = jnp.where(low, w_decay[0], w_decay[1])
            st_s[p] = state * decay + _dot(lhs3, rhs3)

    y = _dot_tn(yt_s[...], wout_ref[...])
    o_ref[...] = x + y * _rms(y) * (gpost_ref[...] * (1.0 + _group_mod(mod_ref, 2)))

    @pl.when(t == pl.num_programs(1) - 1)
    def _():
        for p in range(N_HEADS // 2):
            state = st_s[p]
            c_out_ref[p * LANES:(p + 1) * LANES, :] = state[:D_V, :].T
            n_out_ref[:, p * LANES:(p + 1) * LANES] = state[D_V:D_V + 1, :]
        m_out_ref[...] = m_s[...]


def _mixer_mod_specs(layer, batch):
    gain = pl.BlockSpec((None, None, 1, D_MODEL), lambda b, t: (layer, 1, 0, 0))
    return [pl.BlockSpec((None, 3, batch, D_MODEL), lambda b, t: (layer, 1, 0, 0)), gain, gain]


def _mlstm_prompt(x, mod, g_pre, g_post, layer, w_t, b_if, g_head_t, w_out):
    batch, seq, _ = x.shape
    rows = MIX_ROWS
    return pl.pallas_call(
        functools.partial(_mlstm_kernel, chunk=ML_CHUNK),
        grid=(batch, seq // rows),
        in_specs=[
            pl.BlockSpec((None, rows, D_MODEL), lambda b, t: (b, t, 0)),
            *_mixer_mod_specs(layer, batch),
            _resident(w_t.shape),
            _resident(b_if.shape),
            _resident(g_head_t.shape),
            _resident(w_out.shape),
        ],
        out_specs=[
            pl.BlockSpec((None, rows, D_MODEL), lambda b, t: (b, t, 0)),
            pl.BlockSpec((None, D_HQ, D_V), lambda b, t: (b, 0, 0)),
            pl.BlockSpec((None, 1, D_HQ), lambda b, t: (b, 0, 0)),
            pl.BlockSpec((None, N_HEADS, LANES), lambda b, t: (b, 0, 0)),
        ],
        out_shape=[
            jax.ShapeDtypeStruct(x.shape, F32),
            jax.ShapeDtypeStruct((batch, D_HQ, D_V), F32),
            jax.ShapeDtypeStruct((batch, 1, D_HQ), F32),
            jax.ShapeDtypeStruct((batch, N_HEADS, LANES), F32),
        ],
        scratch_shapes=[
            pltpu.VMEM((N_HEADS // 2, ML_STATE_ROWS, LANES), F32),
            pltpu.VMEM((N_HEADS, LANES), F32),
            pltpu.VMEM((D_HV, rows), BF16),
        ],
        compiler_params=_params("arbitrary", "arbitrary"),
        name="mlstm_prompt",
    )(x, mod, g_pre, g_post, w_t, b_if, g_head_t, w_out)


def _pair_expand(cols):
    lane = lax.broadcasted_iota(jnp.int32, (cols[0].shape[0], LANES), 1)
    return jnp.concatenate(
        [jnp.where(lane < D_QK, cols[2 * p], cols[2 * p + 1]) for p in range(N_HEADS // 2)], axis=1)


def _row_to_head_tile(rows, b):
    return jnp.concatenate([rows[b:b + 1, h * D_V:(h + 1) * D_V] for h in range(N_HEADS)], axis=0)


def _mlstm_step_kernel(x_ref, mod_ref, gpre_ref, gpost_ref, wt_ref, bif_ref, ghead_ref, wout_ref,
                       c_ref, n_ref, m_ref,
                       o_ref, c_out_ref, n_out_ref, m_out_ref,
                       v_s, og_s, a_s, b_s, y_s, qt_s, kwt_s, wst_s):
    i = pl.program_id(0)
    blk = c_ref.shape[0]
    rows = x_ref.shape[0]

    @pl.when(i == 0)
    def _():
        x = x_ref[...]
        h_in = _norm_mod(x, gpre_ref[...], mod_ref[1], mod_ref[0]).astype(BF16)
        proj = _dot_nt(h_in, wt_ref[...])
        gates = proj[:, 2 * D_HQ + 2 * D_HV:] + bif_ref[...]
        ig = (GATE_SOFTCAP * jnp.tanh(gates / GATE_SOFTCAP))[:, :N_HEADS]
        lf = _log_sigmoid(gates)[:, N_HEADS:]
        m_prev = m_ref[...]
        st = lf + m_prev
        m_t = jnp.maximum(st, ig)
        w_i = jnp.exp(ig - m_t)
        w_s = jnp.exp(st - m_t)
        floor = jnp.exp(-m_t)
        m_out_ref[...] = m_t
        q = proj[:, :D_HQ] * (D_QK ** -0.5)
        k = proj[:, D_HQ:2 * D_HQ]
        n_prev = n_ref[...]
        lane = lax.broadcasted_iota(jnp.int32, (rows, LANES), 1)
        lo = lane < D_QK

        def head_sums(z):
            out = []
            for p in range(N_HEADS // 2):
                zp = z[:, p * LANES:(p + 1) * LANES]
                out.append(jnp.sum(jnp.where(lo, zp, 0.0), axis=-1, keepdims=True))
                out.append(jnp.sum(jnp.where(lo, 0.0, zp), axis=-1, keepdims=True))
            return out

        qk_h = head_sums(q * k)
        qn_h = head_sums(q * n_prev)
        ws_cols, wi_cols = [], []
        for hd in range(N_HEADS):
            ws_h = w_s[:, hd:hd + 1]
            wi_h = w_i[:, hd:hd + 1]
            s_h = qk_h[hd] * wi_h
            den = ws_h * qn_h[hd] + s_h
            inv = 1.0 / jnp.maximum(jnp.abs(den), floor[:, hd:hd + 1])
            a_s[:, hd * D_V:(hd + 1) * D_V] = jnp.broadcast_to(ws_h * inv, (rows, D_V))
            b_s[:, hd * D_V:(hd + 1) * D_V] = jnp.broadcast_to(s_h * inv, (rows, D_V))
            ws_cols.append(ws_h)
            wi_cols.append(wi_h)
        ws_x = _pair_expand(ws_cols)
        kw = k * _pair_expand(wi_cols)
        n_out_ref[...] = ws_x * n_prev + kw
        v_s[...] = proj[:, 2 * D_HQ:2 * D_HQ + D_HV]
        og_s[...] = proj[:, 2 * D_HQ + D_HV:2 * D_HQ + 2 * D_HV]
        for dst, src in ((qt_s, q), (kwt_s, kw), (wst_s, ws_x)):
            src_t = src.T
            for j in range(rows // blk):
                dst[j] = src_t[:, j * blk:(j + 1) * blk]

    block_rows = pl.ds(pl.multiple_of(i * blk, blk), blk)
    v_blk, og_blk, a_blk, b_blk = v_s[block_rows, :], og_s[block_rows, :], a_s[block_rows, :], b_s[block_rows, :]
    y_tiles = []
    for b in range(blk):
        v_i = _row_to_head_tile(v_blk, b)
        readout = []
        for hd in range(N_HEADS):
            rs = slice(hd * D_QK, (hd + 1) * D_QK)
            c_h = c_ref[b, hd]
            readout.append(jnp.sum(c_h * qt_s[i, rs, b:b + 1], axis=0, keepdims=True))
            c_out_ref[b, hd] = c_h * wst_s[i, rs, b:b + 1] + kwt_s[i, rs, b:b + 1] * v_i[hd:hd + 1, :]
        hh = _row_to_head_tile(a_blk, b) * jnp.concatenate(readout, axis=0) + _row_to_head_tile(b_blk, b) * v_i
        y_tiles.append(_sigmoid(_row_to_head_tile(og_blk, b)) * (hh * _rms(hh) * ghead_ref[...]))
    y_s[block_rows, :] = jnp.concatenate(
        [jnp.concatenate([y_tiles[b][hd:hd + 1, :] for b in range(blk)], axis=0) for hd in range(N_HEADS)], axis=1)

    @pl.when(i == pl.num_programs(0) - 1)
    def _():
        y = _dot(y_s[...].astype(BF16), wout_ref[...])
        o_ref[...] = x_ref[...] + y * _rms(y) * (gpost_ref[...] * (1.0 + mod_ref[2]))


def _mlstm_step(x, mod, g_pre, g_post, layer, w_t, b_if, g_head, w_out, c0, n0, m0):
    rows = x.shape[0]
    blk = SAMPLE_BLOCK
    nblk = rows // blk
    const = lambda shape: pl.BlockSpec(shape, lambda i: tuple(0 for _ in shape))
    gain = pl.BlockSpec((None, None, 1, D_MODEL), lambda i: (layer, 1, 0, 0))
    c_spec = pl.BlockSpec((blk, N_HEADS, D_QK, D_V), lambda i: (i, 0, 0, 0))
    n_flat = n0.reshape(rows, D_HQ)
    y, c_new, n_new, m_new = pl.pallas_call(
        _mlstm_step_kernel,
        grid=(nblk,),
        in_specs=[
            const(x.shape),
            pl.BlockSpec((None, 3, rows, D_MODEL), lambda i: (layer, 1, 0, 0)),
            gain,
            gain,
            _resident(w_t.shape),
            const(b_if.shape),
            const(g_head.shape),
            _resident(w_out.shape),
            c_spec,
            const(n_flat.shape),
            const(m0.shape),
        ],
        out_specs=[const(x.shape), c_spec, const(n_flat.shape), const(m0.shape)],
        out_shape=[jax.ShapeDtypeStruct(x.shape, F32), jax.ShapeDtypeStruct(c0.shape, F32),
                   jax.ShapeDtypeStruct(n_flat.shape, F32), jax.ShapeDtypeStruct(m0.shape, F32)],
        scratch_shapes=[pltpu.VMEM((rows, D_HV), F32)] * 5 + [pltpu.VMEM((nblk, D_HQ, blk), F32)] * 3,
        compiler_params=_params("arbitrary"),
        name="mlstm_step",
    )(x, mod, g_pre, g_post, w_t, b_if, g_head, w_out, c0, n_flat, m0)
    return y, c_new, n_new.reshape(rows, N_HEADS, D_QK), m_new


def _conv_kernel(x_ref, mod_ref, gpre_ref, gpost_ref, win_ref, cw_ref, wout_ref, o_ref, buf_out_ref, tail_s):
    t = pl.program_id(1)

    @pl.when(t == 0)
    def _():
        tail_s[...] = jnp.zeros_like(tail_s)

    x = x_ref[...]
    rows = x.shape[0]
    h_in = _norm_mod(x, gpre_ref[...], _group_mod(mod_ref, 1), _group_mod(mod_ref, 0)).astype(BF16)
    p = _dot(h_in, win_ref[...])
    bg = p[:, :D_MODEL]
    u = p[:, D_MODEL:2 * D_MODEL] * p[:, 2 * D_MODEL:]
    prev1 = tail_s[7:8, :]
    prev2 = tail_s[6:7, :]
    ridx = lax.broadcasted_iota(jnp.int32, u.shape, 0)
    u1 = jnp.where(ridx == 0, prev1, pltpu.roll(u, 1, 0))
    u2 = jnp.where(ridx == 0, prev2, jnp.where(ridx == 1, prev1, pltpu.roll(u, 2, 0)))
    conv = cw_ref[0:1, :] * u2 + cw_ref[1:2, :] * u1 + cw_ref[2:3, :] * u
    tail_s[...] = u[rows - 8:, :]
    y = _dot((bg * conv).astype(BF16), wout_ref[...])
    o_ref[...] = x + y * _rms(y) * (gpost_ref[...] * (1.0 + _group_mod(mod_ref, 2)))

    @pl.when(t == pl.num_programs(1) - 1)
    def _():
        buf_out_ref[...] = u[rows - (CONV_W - 1):, :]


def _conv_prompt(x, mod, g_pre, g_post, layer, w_in, conv_w, w_out):
    batch, seq, _ = x.shape
    rows = MIX_ROWS
    return pl.pallas_call(
        _conv_kernel,
        grid=(batch, seq // rows),
        in_specs=[
            pl.BlockSpec((None, rows, D_MODEL), lambda b, t: (b, t, 0)),
            *_mixer_mod_specs(layer, batch),
            _resident(w_in.shape),
            _resident(conv_w.shape),
            _resident(w_out.shape),
        ],
        out_specs=[
            pl.BlockSpec((None, rows, D_MODEL), lambda b, t: (b, t, 0)),
            pl.BlockSpec((None, CONV_W - 1, D_MODEL), lambda b, t: (b, 0, 0)),
        ],
        out_shape=[jax.ShapeDtypeStruct(x.shape, F32),
                   jax.ShapeDtypeStruct((batch, CONV_W - 1, D_MODEL), F32)],
        scratch_shapes=[pltpu.VMEM((8, D_MODEL), F32)],
        compiler_params=_params("arbitrary", "arbitrary"),
        name="conv_prompt",
    )(x, mod, g_pre, g_post, w_in, conv_w, w_out)


def _conv_sample_kernel(x_ref, mod_ref, gpre_ref, gpost_ref, win_ref, cw_ref, wout_ref, buf_ref,
                        o_ref, buf_out_ref):
    x = x_ref[...]
    h_in = _norm_mod(x, gpre_ref[...], mod_ref[1], mod_ref[0]).astype(BF16)
    p = _dot(h_in, win_ref[...])
    bg = p[:, :D_MODEL]
    u = p[:, D_MODEL:2 * D_MODEL] * p[:, 2 * D_MODEL:]
    conv = cw_ref[0:1, :] * buf_ref[0] + cw_ref[1:2, :] * buf_ref[1] + cw_ref[2:3, :] * u
    y = _dot((bg * conv).astype(BF16), wout_ref[...])
    o_ref[...] = x + (1.0 + mod_ref[2]) * (y * _rms(y) * gpost_ref[...])
    buf_out_ref[0] = buf_ref[1]
    buf_out_ref[1] = u


def _conv_sample(x, mod, g_pre, g_post, w_in, conv_w, w_out, buf):
    full = lambda shape: pl.BlockSpec(shape, lambda: tuple(0 for _ in shape))
    ins = (x, mod, g_pre.reshape(1, D_MODEL), g_post.reshape(1, D_MODEL), w_in, conv_w, w_out, buf)
    return pl.pallas_call(
        _conv_sample_kernel,
        in_specs=[full(z.shape) for z in ins],
        out_specs=[full(x.shape), full(buf.shape)],
        out_shape=[jax.ShapeDtypeStruct(x.shape, F32), jax.ShapeDtypeStruct(buf.shape, F32)],
        compiler_params=_params(),
        name="conv_sample",
    )(*ins)


def kernel(x_prompt, x_sample, c_prompt, c_sample, state_mlstm_C, state_mlstm_n, state_mlstm_m, state_conv,
           w_ada, b_ada, g_pre, g_post, ffn_wg, ffn_wu, ffn_wd,
           ml_w_in, ml_b_i, ml_b_f, ml_g_head, ml_w_out, cv_w_in, cv_conv_w, cv_w_out):
    depth = w_ada.shape[0]
    batch, seq, _ = x_prompt.shape
    n_sample = x_sample.shape[0]
    n_ml = ml_w_in.shape[0]
    assert x_sample.shape[1] == 1 and seq % ML_CHUNK == 0

    mod_p, mod_s = _ada(c_prompt, c_sample, w_ada, b_ada)
    gp = g_pre.reshape(depth, 3, 1, D_MODEL)
    gq = g_post.reshape(depth, 3, 1, D_MODEL)

    ml_t = ml_w_in.transpose(0, 2, 1).astype(BF16)
    ml_bif = jnp.concatenate([ml_b_i, ml_b_f], axis=-1)
    ml_bif_t = jnp.broadcast_to(ml_bif[:, :, None], (n_ml, 2 * N_HEADS, MIX_ROWS))
    ml_ghead_t = jnp.broadcast_to(ml_g_head[:, :, :, None], (n_ml, N_HEADS, D_V, LANES))
    ml_out = ml_w_out.astype(BF16)
    cv_in = cv_w_in.astype(BF16)
    cv_out = cv_w_out.astype(BF16)

    xp = x_prompt
    xs = x_sample.reshape(n_sample, D_MODEL)
    p_c, p_n, p_m, p_buf, s_c, s_n, s_m, s_buf = [], [], [], [], [], [], [], []
    for l in range(depth):
        ms = lambda s: mod_s[l, 3 * s:3 * s + 3]
        xp, xs = _ffn(xp, xs, mod_p, mod_s, gp, gq, ffn_wg, ffn_wu, ffn_wd, l, 0)
        j = l // 2
        if l % 2 == 0:
            xp, c_j, n_j, m_j = _mlstm_prompt(xp, mod_p, gp, gq, l, ml_t[j], ml_bif_t[j], ml_ghead_t[j], ml_out[j])
            p_c.append(c_j.reshape(batch, N_HEADS, D_QK, D_V))
            p_n.append(n_j.reshape(batch, N_HEADS, D_QK))
            p_m.append(m_j[:, :, 0])
            ys, c_j, n_j, m_j = _mlstm_step(xs, mod_s, gp, gq, l, ml_t[j], ml_bif[j][None], ml_g_head[j], ml_out[j],
                                            state_mlstm_C[j], state_mlstm_n[j], state_mlstm_m[j])
            xs = ys
            s_c.append(c_j)
            s_n.append(n_j)
            s_m.append(m_j)
        else:
            xp, buf_j = _conv_prompt(xp, mod_p, gp, gq, l, cv_in[j], cv_conv_w[j], cv_out[j])
            p_buf.append(buf_j)
            ys, buf_j = _conv_sample(xs, ms(1), g_pre[l, 1], g_post[l, 1], cv_in[j], cv_conv_w[j], cv_out[j],
                                     state_conv[j].transpose(1, 0, 2))
            xs = ys
            s_buf.append(buf_j.transpose(1, 0, 2))
        xp, xs = _ffn(xp, xs, mod_p, mod_s, gp, gq, ffn_wg, ffn_wu, ffn_wd, l, 1)

    return (xp, xs.reshape(n_sample, 1, D_MODEL),
            jnp.stack(p_c), jnp.stack(p_n), jnp.stack(p_m), jnp.stack(p_buf),
            jnp.stack(s_c), jnp.stack(s_n), jnp.stack(s_m), jnp.stack(s_buf))
```

```python
import functools

import jax
import jax.numpy as jnp
from jax import lax
from jax.experimental import pallas as pl
from jax.experimental.pallas import tpu as pltpu

F32 = jnp.float32
BF16 = jnp.bfloat16

D_MODEL = 1024
N_HEADS = 8
D_QK = 64
D_V = 128
D_HQ = N_HEADS * D_QK
D_HV = N_HEADS * D_V
D_FF = 2816
N_ADA = 9
CONV_W = 3
GATE_SOFTCAP = 15.0
EPS = 1e-6

LANES = 128
BF16_ROWS = 16
MXU_COLS = 256
VMEM_LIMIT_BYTES = 56 * 1024 * 1024

FFN_ROWS = 512
FFN_SUB = MXU_COLS
ADA_COLS = 3 * D_MODEL
MIX_ROWS = 1024
ML_CHUNK = LANES
ML_STATE_ROWS = D_V + BF16_ROWS
SAMPLE_BLOCK = 16


def _params(*sem):
    return pltpu.CompilerParams(dimension_semantics=sem, vmem_limit_bytes=VMEM_LIMIT_BYTES)


def _sigmoid(x):
    return 0.5 * jnp.tanh(0.5 * x) + 0.5


def _rms(x):
    return lax.rsqrt(jnp.mean(x * x, axis=-1, keepdims=True) + EPS)


def _norm_mod(x, gain, scale, shift):
    return x * _rms(x) * (gain * (1.0 + scale)) + shift


def _dot(a, b):
    return jnp.dot(a, b, preferred_element_type=F32)


def _dot_nt(a, b):
    return lax.dot_general(a, b, (((1,), (1,)), ((), ())), preferred_element_type=F32)


def _dot_tn(a, b):
    return lax.dot_general(a, b, (((0,), (0,)), ((), ())), preferred_element_type=F32)


def _log_sigmoid(x):
    return -(jnp.maximum(-x, 0.0) + jnp.log1p(jnp.exp(-jnp.abs(x))))


def _resident(shape):
    return pl.BlockSpec(shape, lambda *_: tuple(0 for _ in shape), pipeline_mode=pl.Buffered(1))


def _ada_kernel(cp_ref, cs_ref, w_ref, b_ref, op_ref, os_ref):
    w = w_ref[...].astype(BF16)
    for c_ref, o_ref in ((cp_ref, op_ref), (cs_ref, os_ref)):
        c = c_ref[...]
        res = _dot((c * _sigmoid(c)).astype(BF16), w)
        for k in range(o_ref.shape[0]):
            o_ref[k] = res[:, k * D_MODEL:(k + 1) * D_MODEL] + b_ref[k]


def _ada(c_prompt, c_sample, w_ada, b_ada):
    depth = w_ada.shape[0]
    per_step = ADA_COLS // D_MODEL
    out = lambda c: (pl.BlockSpec((None, per_step, c.shape[0], D_MODEL), lambda l, j: (l, j, 0, 0)),
                     jax.ShapeDtypeStruct((depth, N_ADA, c.shape[0], D_MODEL), F32))
    (spec_p, shape_p), (spec_s, shape_s) = out(c_prompt), out(c_sample)
    return pl.pallas_call(
        _ada_kernel,
        grid=(depth, N_ADA // per_step),
        in_specs=[
            pl.BlockSpec(c_prompt.shape, lambda l, j: (0, 0)),
            pl.BlockSpec(c_sample.shape, lambda l, j: (0, 0)),
            pl.BlockSpec((None, D_MODEL, ADA_COLS), lambda l, j: (l, 0, j)),
            pl.BlockSpec((None, per_step, 1, D_MODEL), lambda l, j: (l, j, 0, 0)),
        ],
        out_specs=[spec_p, spec_s],
        out_shape=[shape_p, shape_s],
        compiler_params=_params("arbitrary", "arbitrary"),
        name="ada",
    )(c_prompt, c_sample, w_ada, b_ada.reshape(depth, N_ADA, 1, D_MODEL))


def _group_mod(mod_ref, j):
    return mod_ref[j, pl.ds(pl.program_id(0), 1), :]


def _swiglu_chunk(hn, wg_ref, wu_ref, wd_ref, c, sub):
    cols = slice(c * sub, (c + 1) * sub)
    g = _dot(hn, wg_ref[:, cols])
    u = _dot(hn, wu_ref[:, cols])
    return _dot((g * _sigmoid(g) * u).astype(BF16), wd_ref[cols, :])


def _ffn_kernel(xp_ref, xs_ref, modp_ref, mods_ref, gpre_ref, gpost_ref, wg_hbm, wu_hbm, wd_hbm,
                op_ref, os_ref,
                wg_s, wu_s, wd_s, up_stage, dn_stage, sem, hn_s, acc_s, *,
                layer, half, n_tiles, tiles_per_group, sub):
    s = pl.program_id(0)
    n_chunks = wg_s.shape[1] // sub

    def chunk_copies(c):
        slot = c % 2
        cols = pl.ds(c * sub, sub)
        return (pltpu.make_async_copy(wg_hbm.at[layer, half, :, cols], up_stage.at[slot, 0], sem.at[slot, 0]),
                pltpu.make_async_copy(wu_hbm.at[layer, half, :, cols], up_stage.at[slot, 1], sem.at[slot, 1]),
                pltpu.make_async_copy(wd_hbm.at[layer, half, cols, :], dn_stage.at[slot], sem.at[slot, 2]))

    def stage_chunk(c):
        if c + 1 < n_chunks:
            for copy in chunk_copies(c + 1):
                copy.start()
        for copy in chunk_copies(c):
            copy.wait()
        cols = slice(c * sub, (c + 1) * sub)
        wg_s[:, cols] = up_stage[c % 2, 0].astype(BF16)
        wu_s[:, cols] = up_stage[c % 2, 1].astype(BF16)
        wd_s[cols, :] = dn_stage[c % 2].astype(BF16)

    def half_step(x_ref, o_ref, scale, shift, gate, stage_weights=False):
        rows = x_ref.shape[0]
        x = x_ref[...]
        hn_s[0:rows, :] = _norm_mod(x, gpre_ref[...], scale, shift).astype(BF16)
        hn = hn_s[0:rows, :]
        for c in range(n_chunks):
            if stage_weights:
                stage_chunk(c)
            down = _swiglu_chunk(hn, wg_s, wu_s, wd_s, c, sub)
            if c == 0:
                acc_s[0:rows, :] = down
            else:
                acc_s[0:rows, :] += down
        y = acc_s[0:rows, :]
        o_ref[...] = x + y * _rms(y) * (gpost_ref[...] * (0.5 * (1.0 + gate)))

    group = jnp.minimum(s, n_tiles - 1) // tiles_per_group
    group_mod = lambda j: modp_ref[j, pl.ds(group, 1), :]

    @pl.when(s == 0)
    def _():
        for copy in chunk_copies(0):
            copy.start()
        half_step(xp_ref, op_ref, group_mod(1), group_mod(0), group_mod(2), stage_weights=True)

    @pl.when((s > 0) & (s < n_tiles))
    def _():
        half_step(xp_ref, op_ref, group_mod(1), group_mod(0), group_mod(2))

    @pl.when(s == n_tiles)
    def _():
        half_step(xs_ref, os_ref, mods_ref[1], mods_ref[0], mods_ref[2])


def _ffn(xp, xs, mod_p, mod_s, g_pre, g_post, wg, wu, wd, layer, half):
    groups, seq, _ = xp.shape
    n_sample = xs.shape[0]
    rows = FFN_ROWS
    tpg = seq // rows
    n_tiles = groups * tpg
    sub = FFN_SUB
    tile = lambda s: jnp.minimum(s, n_tiles - 1)
    xp_spec = pl.BlockSpec((None, rows, D_MODEL), lambda s: (tile(s) // tpg, tile(s) % tpg, 0))
    xs_spec = pl.BlockSpec((n_sample, D_MODEL), lambda s: (0, 0))
    gain = pl.BlockSpec((None, None, 1, D_MODEL), lambda s: (layer, 2 * half, 0, 0))
    hbm = pl.BlockSpec(memory_space=pl.ANY)
    return pl.pallas_call(
        functools.partial(_ffn_kernel, layer=layer, half=half, n_tiles=n_tiles, tiles_per_group=tpg, sub=sub),
        grid=(n_tiles + 1,),
        in_specs=[
            xp_spec,
            xs_spec,
            pl.BlockSpec((None, 3, groups, D_MODEL), lambda s: (layer, 2 * half, 0, 0)),
            pl.BlockSpec((None, 3, n_sample, D_MODEL), lambda s: (layer, 2 * half, 0, 0)),
            gain,
            gain,
            hbm,
            hbm,
            hbm,
        ],
        out_specs=[xp_spec, xs_spec],
        out_shape=[jax.ShapeDtypeStruct(xp.shape, F32), jax.ShapeDtypeStruct(xs.shape, F32)],
        scratch_shapes=[
            pltpu.VMEM((D_MODEL, D_FF), BF16),
            pltpu.VMEM((D_MODEL, D_FF), BF16),
            pltpu.VMEM((D_FF, D_MODEL), BF16),
            pltpu.VMEM((2, 2, D_MODEL, sub), F32),
            pltpu.VMEM((2, sub, D_MODEL), F32),
            pltpu.SemaphoreType.DMA((2, 3)),
            pltpu.VMEM((max(rows, n_sample), D_MODEL), BF16),
            pltpu.VMEM((max(rows, n_sample), D_MODEL), F32),
        ],
        compiler_params=_params("arbitrary"),
        name="ffn",
    )(xp, xs, mod_p, mod_s, g_pre, g_post, wg, wu, wd)


def _block_diag(a, b):
    za = jnp.zeros((a.shape[0], b.shape[1]), a.dtype)
    zb = jnp.zeros((b.shape[0], a.shape[1]), a.dtype)
    return jnp.concatenate([jnp.concatenate([a, za], axis=1), jnp.concatenate([zb, b], axis=1)], axis=0)


def _mlstm_kernel(x_ref, mod_ref, gpre_ref, gpost_ref, wt_ref, bif_ref, ght_ref, wout_ref,
                  o_ref, c_out_ref, n_out_ref, m_out_ref,
                  st_s, m_s, yt_s, *, chunk):
    t = pl.program_id(1)

    @pl.when(t == 0)
    def _():
        st_s[...] = jnp.zeros_like(st_s)
        m_s[...] = jnp.zeros_like(m_s)

    x = x_ref[...]
    rows = x.shape[0]
    h_in = _norm_mod(x, gpre_ref[...], _group_mod(mod_ref, 1), _group_mod(mod_ref, 0)).astype(BF16)
    proj_q = _dot_nt(wt_ref[0:D_HQ], h_in)
    proj_t = _dot_nt(wt_ref[2 * D_HQ:], h_in)
    k_all = _dot_nt(h_in, wt_ref[D_HQ:2 * D_HQ])
    r_v, r_o, r_i, r_f = 0, D_HV, 2 * D_HV, 2 * D_HV + N_HEADS

    s_idx = lax.broadcasted_iota(jnp.int32, (chunk, chunk), 0)
    t_idx = lax.broadcasted_iota(jnp.int32, (chunk, chunk), 1)
    causal = s_idx <= t_idx
    tri = jnp.where(causal, 1.0, 0.0).astype(BF16)
    lane = lax.broadcasted_iota(jnp.int32, (1, LANES), 1)
    low = lane < D_QK
    ones_rows = jnp.ones((BF16_ROWS, 2 * chunk), BF16)
    zero_rows = jnp.zeros((LANES - N_HEADS, chunk), F32)

    ig_all = GATE_SOFTCAP * jnp.tanh((proj_t[r_i:r_i + N_HEADS, :] + bif_ref[0:N_HEADS, :]) / GATE_SOFTCAP)
    lf_all = _log_sigmoid(proj_t[r_f:r_f + N_HEADS, :] + bif_ref[N_HEADS:, :])

    for c in range(rows // chunk):
        cs = slice(c * chunk, (c + 1) * chunk)
        lf = lf_all[:, cs]
        lf_hi = lf.astype(BF16).astype(F32)
        lf_mid = (lf - lf_hi).astype(BF16).astype(F32)
        lf_lo = lf - lf_hi - lf_mid
        b3 = _dot(jnp.concatenate([lf_hi, lf_mid, lf_lo, jnp.zeros_like(lf)], axis=0).astype(BF16), tri)
        b = b3[0:N_HEADS] + b3[N_HEADS:2 * N_HEADS] + b3[2 * N_HEADS:3 * N_HEADS]
        col = ig_all[:, cs] - b
        b_last = jnp.broadcast_to(b[:, chunk - 1:chunk], b.shape)
        col_s = jnp.concatenate([col, zero_rows], axis=0).T

        for p in range(N_HEADS // 2):
            h0, h1 = 2 * p, 2 * p + 1
            qt = [(proj_q[h * D_QK:(h + 1) * D_QK, cs] * (D_QK ** -0.5)).astype(BF16) for h in (h0, h1)]
            vt = [proj_t[r_v + h * D_V:r_v + (h + 1) * D_V, cs] for h in (h0, h1)]
            k_pair = k_all[cs, p * LANES:(p + 1) * LANES]
            state = st_s[p]
            lhs1 = jnp.concatenate([k_pair.astype(BF16), state.astype(BF16)], axis=0)
            r1 = _dot(lhs1, _block_diag(qt[0], qt[1]))
            probs, w_state, inv_floor, w_k, w_decay = [], [], [], [], []
            for i, h in enumerate((h0, h1)):
                m_prev = m_s[h:h + 1, :]
                col_m = jnp.where(causal, col_s[:, h:h + 1], -jnp.inf)
                g = jnp.maximum(m_prev, jnp.max(col_m, axis=0, keepdims=True))
                probs.append((jnp.exp(col_m - g) * r1[:chunk, i * chunk:(i + 1) * chunk]).astype(BF16))
                m_t = b[h:h + 1, :] + g
                m_new = jnp.broadcast_to(m_t[:, chunk - 1:chunk], m_t.shape)
                w_state.append(jnp.exp(m_prev - g))
                inv_floor.append(jnp.exp(-m_t))
                w_decay.append(jnp.exp(b_last[h:h + 1, :] + m_prev - m_new))
                w_k.append(jnp.exp(b_last[h:h + 1, :] - m_new + col[h:h + 1, :]))
                m_s[h:h + 1, :] = m_new
            lhs2 = jnp.concatenate([jnp.concatenate([vt[0], vt[1]], axis=1).astype(BF16), ones_rows], axis=0)
            r2 = _dot(lhs2, _block_diag(probs[0], probs[1]))
            for i, h in enumerate((h0, h1)):
                ls = slice(i * chunk, (i + 1) * chunk)
                den = w_state[i] * r1[chunk + D_V:chunk + D_V + 1, ls] + r2[D_V:D_V + 1, ls]
                inv = 1.0 / jnp.maximum(jnp.abs(den), inv_floor[i])
                ht = (w_state[i] * r1[chunk:chunk + D_V, ls] + r2[:D_V, ls]) * inv
                rn = lax.rsqrt(jnp.mean(ht * ht, axis=0, keepdims=True) + EPS)
                og = proj_t[r_o + h * D_V:r_o + (h + 1) * D_V, cs]
                yt_s[h * D_V:(h + 1) * D_V, cs] = (_sigmoid(og) * (ht * rn * ght_ref[h])).astype(BF16)
            lhs3 = jnp.concatenate(
                [jnp.concatenate([vt[i] * w_k[i] for i in range(2)], axis=1),
                 jnp.concatenate([jnp.broadcast_to(w_k[i], (BF16_ROWS, chunk)) for i in range(2)], axis=1)],
                axis=0).astype(BF16)
            rhs3 = jnp.concatenate([jnp.where(low, k_pair, 0.0), jnp.where(low, 0.0, k_pair)], axis=0).astype(BF16)
            decay = jnp.where(low, w_decay[0], w_decay[1])
            st_s[p] = state * decay + _dot(lhs3, rhs3)

    y = _dot_tn(yt_s[...], wout_ref[...])
    o_ref[...] = x + y * _rms(y) * (gpost_ref[...] * (1.0 + _group_mod(mod_ref, 2)))

    @pl.when(t == pl.num_programs(1) - 1)
    def _():
        for p in range(N_HEADS // 2):
            state = st_s[p]
            c_out_ref[p * LANES:(p + 1) * LANES, :] = state[:D_V, :].T
            n_out_ref[:, p * LANES:(p + 1) * LANES] = state[D_V:D_V + 1, :]
        m_out_ref[...] = m_s[...]


def _mixer_mod_specs(layer, batch):
    gain = pl.BlockSpec((None, None, 1, D_MODEL), lambda b, t: (layer, 1, 0, 0))
    return [pl.BlockSpec((None, 3, batch, D_MODEL), lambda b, t: (layer, 1, 0, 0)), gain, gain]


def _mlstm_prompt(x, mod, g_pre, g_post, layer, w_t, b_if, g_head_t, w_out):
    batch, seq, _ = x.shape
    rows = MIX_ROWS
    return pl.pallas_call(
        functools.partial(_mlstm_kernel, chunk=ML_CHUNK),
        grid=(batch, seq // rows),
        in_specs=[
            pl.BlockSpec((None, rows, D_MODEL), lambda b, t: (b, t, 0)),
            *_mixer_mod_specs(layer, batch),
            _resident(w_t.shape),
            _resident(b_if.shape),
            _resident(g_head_t.shape),
            _resident(w_out.shape),
        ],
        out_specs=[
            pl.BlockSpec((None, rows, D_MODEL), lambda b, t: (b, t, 0)),
            pl.BlockSpec((None, D_HQ, D_V), lambda b, t: (b, 0, 0)),
            pl.BlockSpec((None, 1, D_HQ), lambda b, t: (b, 0, 0)),
            pl.BlockSpec((None, N_HEADS, LANES), lambda b, t: (b, 0, 0)),
        ],
        out_shape=[
            jax.ShapeDtypeStruct(x.shape, F32),
            jax.ShapeDtypeStruct((batch, D_HQ, D_V), F32),
            jax.ShapeDtypeStruct((batch, 1, D_HQ), F32),
            jax.ShapeDtypeStruct((batch, N_HEADS, LANES), F32),
        ],
        scratch_shapes=[
            pltpu.VMEM((N_HEADS // 2, ML_STATE_ROWS, LANES), F32),
            pltpu.VMEM((N_HEADS, LANES), F32),
            pltpu.VMEM((D_HV, rows), BF16),
        ],
        compiler_params=_params("arbitrary", "arbitrary"),
        name="mlstm_prompt",
    )(x, mod, g_pre, g_post, w_t, b_if, g_head_t, w_out)


def _pair_expand(cols):
    lane = lax.broadcasted_iota(jnp.int32, (cols[0].shape[0], LANES), 1)
    return jnp.concatenate(
        [jnp.where(lane < D_QK, cols[2 * p], cols[2 * p + 1]) for p in range(N_HEADS // 2)], axis=1)


def _row_to_head_tile(rows, b):
    return jnp.concatenate([rows[b:b + 1, h * D_V:(h + 1) * D_V] for h in range(N_HEADS)], axis=0)


def _mlstm_step_kernel(x_ref, mod_ref, gpre_ref, gpost_ref, wt_ref, bif_ref, ghead_ref, wout_ref,
                       c_ref, n_ref, m_ref,
                       o_ref, c_out_ref, n_out_ref, m_out_ref,
                       v_s, og_s, a_s, b_s, y_s, qt_s, kwt_s, wst_s):
    i = pl.program_id(0)
    blk = c_ref.shape[0]
    rows = x_ref.shape[0]

    @pl.when(i == 0)
    def _():
        x = x_ref[...]
        h_in = _norm_mod(x, gpre_ref[...], mod_ref[1], mod_ref[0]).astype(BF16)
        proj = _dot_nt(h_in, wt_ref[...])
        gates = proj[:, 2 * D_HQ + 2 * D_HV:] + bif_ref[...]
        ig = (GATE_SOFTCAP * jnp.tanh(gates / GATE_SOFTCAP))[:, :N_HEADS]
        lf = _log_sigmoid(gates)[:, N_HEADS:]
        m_prev = m_ref[...]
        st = lf + m_prev
        m_t = jnp.maximum(st, ig)
        w_i = jnp.exp(ig - m_t)
        w_s = jnp.exp(st - m_t)
        floor = jnp.exp(-m_t)
        m_out_ref[...] = m_t
        q = proj[:, :D_HQ] * (D_QK ** -0.5)
        k = proj[:, D_HQ:2 * D_HQ]
        n_prev = n_ref[...]
        lane = lax.broadcasted_iota(jnp.int32, (rows, LANES), 1)
        lo = lane < D_QK

        def head_sums(z):
            out = []
            for p in range(N_HEADS // 2):
                zp = z[:, p * LANES:(p + 1) * LANES]
                out.append(jnp.sum(jnp.where(lo, zp, 0.0), axis=-1, keepdims=True))
                out.append(jnp.sum(jnp.where(lo, 0.0, zp), axis=-1, keepdims=True))
            return out

        qk_h = head_sums(q * k)
        qn_h = head_sums(q * n_prev)
        ws_cols, wi_cols = [], []
        for hd in range(N_HEADS):
            ws_h = w_s[:, hd:hd + 1]
            wi_h = w_i[:, hd:hd + 1]
            s_h = qk_h[hd] * wi_h
            den = ws_h * qn_h[hd] + s_h
            inv = 1.0 / jnp.maximum(jnp.abs(den), floor[:, hd:hd + 1])
            a_s[:, hd * D_V:(hd + 1) * D_V] = jnp.broadcast_to(ws_h * inv, (rows, D_V))
            b_s[:, hd * D_V:(hd + 1) * D_V] = jnp.broadcast_to(s_h * inv, (rows, D_V))
            ws_cols.append(ws_h)
            wi_cols.append(wi_h)
        ws_x = _pair_expand(ws_cols)
        kw = k * _pair_expand(wi_cols)
        n_out_ref[...] = ws_x * n_prev + kw
        v_s[...] = proj[:, 2 * D_HQ:2 * D_HQ + D_HV]
        og_s[...] = proj[:, 2 * D_HQ + D_HV:2 * D_HQ + 2 * D_HV]
        for dst, src in ((qt_s, q), (kwt_s, kw), (wst_s, ws_x)):
            src_t = src.T
            for j in range(rows // blk):
                dst[j] = src_t[:, j * blk:(j + 1) * blk]

    block_rows = pl.ds(pl.multiple_of(i * blk, blk), blk)
    v_blk, og_blk, a_blk, b_blk = v_s[block_rows, :], og_s[block_rows, :], a_s[block_rows, :], b_s[block_rows, :]
    y_tiles = []
    for b in range(blk):
        v_i = _row_to_head_tile(v_blk, b)
        readout = []
        for hd in range(N_HEADS):
            rs = slice(hd * D_QK, (hd + 1) * D_QK)
            c_h = c_ref[b, hd]
            readout.append(jnp.sum(c_h * qt_s[i, rs, b:b + 1], axis=0, keepdims=True))
            c_out_ref[b, hd] = c_h * wst_s[i, rs, b:b + 1] + kwt_s[i, rs, b:b + 1] * v_i[hd:hd + 1, :]
        hh = _row_to_head_tile(a_blk, b) * jnp.concatenate(readout, axis=0) + _row_to_head_tile(b_blk, b) * v_i
        y_tiles.append(_sigmoid(_row_to_head_tile(og_blk, b)) * (hh * _rms(hh) * ghead_ref[...]))
    y_s[block_rows, :] = jnp.concatenate(
        [jnp.concatenate([y_tiles[b][hd:hd + 1, :] for b in range(blk)], axis=0) for hd in range(N_HEADS)], axis=1)

    @pl.when(i == pl.num_programs(0) - 1)
    def _():
        y = _dot(y_s[...].astype(BF16), wout_ref[...])
        o_ref[...] = x_ref[...] + y * _rms(y) * (gpost_ref[...] * (1.0 + mod_ref[2]))


def _mlstm_step(x, mod, g_pre, g_post, layer, w_t, b_if, g_head, w_out, c0, n0, m0):
    rows = x.shape[0]
    blk = SAMPLE_BLOCK
    nblk = rows // blk
    const = lambda shape: pl.BlockSpec(shape, lambda i: tuple(0 for _ in shape))
    gain = pl.BlockSpec((None, None, 1, D_MODEL), lambda i: (layer, 1, 0, 0))
    c_spec = pl.BlockSpec((blk, N_HEADS, D_QK, D_V), lambda i: (i, 0, 0, 0))
    n_flat = n0.reshape(rows, D_HQ)
    y, c_new, n_new, m_new = pl.pallas_call(
        _mlstm_step_kernel,
        grid=(nblk,),
        in_specs=[
            const(x.shape),
            pl.BlockSpec((None, 3, rows, D_MODEL), lambda i: (layer, 1, 0, 0)),
            gain,
            gain,
            _resident(w_t.shape),
            const(b_if.shape),
            const(g_head.shape),
            _resident(w_out.shape),
            c_spec,
            const(n_flat.shape),
            const(m0.shape),
        ],
        out_specs=[const(x.shape), c_spec, const(n_flat.shape), const(m0.shape)],
        out_shape=[jax.ShapeDtypeStruct(x.shape, F32), jax.ShapeDtypeStruct(c0.shape, F32),
                   jax.ShapeDtypeStruct(n_flat.shape, F32), jax.ShapeDtypeStruct(m0.shape, F32)],
        scratch_shapes=[pltpu.VMEM((rows, D_HV), F32)] * 5 + [pltpu.VMEM((nblk, D_HQ, blk), F32)] * 3,
        compiler_params=_params("arbitrary"),
        name="mlstm_step",
    )(x, mod, g_pre, g_post, w_t, b_if, g_head, w_out, c0, n_flat, m0)
    return y, c_new, n_new.reshape(rows, N_HEADS, D_QK), m_new


def _conv_kernel(x_ref, mod_ref, gpre_ref, gpost_ref, win_ref, cw_ref, wout_ref, o_ref, buf_out_ref, tail_s):
    t = pl.program_id(1)

    @pl.when(t == 0)
    def _():
        tail_s[...] = jnp.zeros_like(tail_s)

    x = x_ref[...]
    rows = x.shape[0]
    h_in = _norm_mod(x, gpre_ref[...], _group_mod(mod_ref, 1), _group_mod(mod_ref, 0)).astype(BF16)
    p = _dot(h_in, win_ref[...])
    bg = p[:, :D_MODEL]
    u = p[:, D_MODEL:2 * D_MODEL] * p[:, 2 * D_MODEL:]
    prev1 = tail_s[7:8, :]
    prev2 = tail_s[6:7, :]
    ridx = lax.broadcasted_iota(jnp.int32, u.shape, 0)
    u1 = jnp.where(ridx == 0, prev1, pltpu.roll(u, 1, 0))
    u2 = jnp.where(ridx == 0, prev2, jnp.where(ridx == 1, prev1, pltpu.roll(u, 2, 0)))
    conv = cw_ref[0:1, :] * u2 + cw_ref[1:2, :] * u1 + cw_ref[2:3, :] * u
    tail_s[...] = u[rows - 8:, :]
    y = _dot((bg * conv).astype(BF16), wout_ref[...])
    o_ref[...] = x + y * _rms(y) * (gpost_ref[...] * (1.0 + _group_mod(mod_ref, 2)))

    @pl.when(t == pl.num_programs(1) - 1)
    def _():
        buf_out_ref[...] = u[rows - (CONV_W - 1):, :]


def _conv_prompt(x, mod, g_pre, g_post, layer, w_in, conv_w, w_out):
    batch, seq, _ = x.shape
    rows = MIX_ROWS
    return pl.pallas_call(
        _conv_kernel,
        grid=(batch, seq // rows),
        in_specs=[
            pl.BlockSpec((None, rows, D_MODEL), lambda b, t: (b, t, 0)),
            *_mixer_mod_specs(layer, batch),
            _resident(w_in.shape),
            _resident(conv_w.shape),
            _resident(w_out.shape),
        ],
        out_specs=[
            pl.BlockSpec((None, rows, D_MODEL), lambda b, t: (b, t, 0)),
            pl.BlockSpec((None, CONV_W - 1, D_MODEL), lambda b, t: (b, 0, 0)),
        ],
        out_shape=[jax.ShapeDtypeStruct(x.shape, F32),
                   jax.ShapeDtypeStruct((batch, CONV_W - 1, D_MODEL), F32)],
        scratch_shapes=[pltpu.VMEM((8, D_MODEL), F32)],
        compiler_params=_params("arbitrary", "arbitrary"),
        name="conv_prompt",
    )(x, mod, g_pre, g_post, w_in, conv_w, w_out)


def _conv_sample_kernel(x_ref, mod_ref, gpre_ref, gpost_ref, win_ref, cw_ref, wout_ref, buf_ref,
                        o_ref, buf_out_ref):
    x = x_ref[...]
    h_in = _norm_mod(x, gpre_ref[...], mod_ref[1], mod_ref[0]).astype(BF16)
    p = _dot(h_in, win_ref[...])
    bg = p[:, :D_MODEL]
    u = p[:, D_MODEL:2 * D_MODEL] * p[:, 2 * D_MODEL:]
    conv = cw_ref[0:1, :] * buf_ref[0] + cw_ref[1:2, :] * buf_ref[1] + cw_ref[2:3, :] * u
    y = _dot((bg * conv).astype(BF16), wout_ref[...])
    o_ref[...] = x + (1.0 + mod_ref[2]) * (y * _rms(y) * gpost_ref[...])
    buf_out_ref[0] = buf_ref[1]
    buf_out_ref[1] = u


def _conv_sample(x, mod, g_pre, g_post, w_in, conv_w, w_out, buf):
    full = lambda shape: pl.BlockSpec(shape, lambda: tuple(0 for _ in shape))
    ins = (x, mod, g_pre.reshape(1, D_MODEL), g_post.reshape(1, D_MODEL), w_in, conv_w, w_out, buf)
    return pl.pallas_call(
        _conv_sample_kernel,
        in_specs=[full(z.shape) for z in ins],
        out_specs=[full(x.shape), full(buf.shape)],
        out_shape=[jax.ShapeDtypeStruct(x.shape, F32), jax.ShapeDtypeStruct(buf.shape, F32)],
        compiler_params=_params(),
        name="conv_sample",
    )(*ins)


def kernel(x_prompt, x_sample, c_prompt, c_sample, state_mlstm_C, state_mlstm_n, state_mlstm_m, state_conv,
           w_ada, b_ada, g_pre, g_post, ffn_wg, ffn_wu, ffn_wd,
           ml_w_in, ml_b_i, ml_b_f, ml_g_head, ml_w_out, cv_w_in, cv_conv_w, cv_w_out):
    depth = w_ada.shape[0]
    batch, seq, _ = x_prompt.shape
    n_sample = x_sample.shape[0]
    n_ml = ml_w_in.shape[0]
    assert x_sample.shape[1] == 1 and seq % ML_CHUNK == 0

    mod_p, mod_s = _ada(c_prompt, c_sample, w_ada, b_ada)
    gp = g_pre.reshape(depth, 3, 1, D_MODEL)
    gq = g_post.reshape(depth, 3, 1, D_MODEL)

    ml_t = ml_w_in.transpose(0, 2, 1).astype(BF16)
    ml_bif = jnp.concatenate([ml_b_i, ml_b_f], axis=-1)
    ml_bif_t = jnp.broadcast_to(ml_bif[:, :, None], (n_ml, 2 * N_HEADS, MIX_ROWS))
    ml_ghead_t = jnp.broadcast_to(ml_g_head[:, :, :, None], (n_ml, N_HEADS, D_V, LANES))
    ml_out = ml_w_out.astype(BF16)
    cv_in = cv_w_in.astype(BF16)
    cv_out = cv_w_out.astype(BF16)

    xp = x_prompt
    xs = x_sample.reshape(n_sample, D_MODEL)
    p_c, p_n, p_m, p_buf, s_c, s_n, s_m, s_buf = [], [], [], [], [], [], [], []
    for l in range(depth):
        ms = lambda s: mod_s[l, 3 * s:3 * s + 3]
        xp, xs = _ffn(xp, xs, mod_p, mod_s, gp, gq, ffn_wg, ffn_wu, ffn_wd, l, 0)
        j = l // 2
        if l % 2 == 0:
            xp, c_j, n_j, m_j = _mlstm_prompt(xp, mod_p, gp, gq, l, ml_t[j], ml_bif_t[j], ml_ghead_t[j], ml_out[j])
            p_c.append(c_j.reshape(batch, N_HEADS, D_QK, D_V))
            p_n.append(n_j.reshape(batch, N_HEADS, D_QK))
            p_m.append(m_j[:, :, 0])
            ys, c_j, n_j, m_j = _mlstm_step(xs, mod_s, gp, gq, l, ml_t[j], ml_bif[j][None], ml_g_head[j], ml_out[j],
                                            state_mlstm_C[j], state_mlstm_n[j], state_mlstm_m[j])
            xs = ys
            s_c.append(c_j)
            s_n.append(n_j)
            s_m.append(m_j)
        else:
            xp, buf_j = _conv_prompt(xp, mod_p, gp, gq, l, cv_in[j], cv_conv_w[j], cv_out[j])
            p_buf.append(buf_j)
            ys, buf_j = _conv_sample(xs, ms(1), g_pre[l, 1], g_post[l, 1], cv_in[j], cv_conv_w[j], cv_out[j],
                                     state_conv[j].transpose(1, 0, 2))
            xs = ys
            s_buf.append(buf_j.transpose(1, 0, 2))
        xp, xs = _ffn(xp, xs, mod_p, mod_s, gp, gq, ffn_wg, ffn_wu, ffn_wd, l, 1)

    return (xp, xs.reshape(n_sample, 1, D_MODEL),
            jnp.stack(p_c), jnp.stack(p_n), jnp.stack(p_m), jnp.stack(p_buf),
            jnp.stack(s_c), jnp.stack(s_n), jnp.stack(s_m), jnp.stack(s_buf))
```

```python
import functools

import jax
import jax.numpy as jnp
from jax import lax
from jax.experimental import pallas as pl
from jax.experimental.pallas import tpu as pltpu

F32 = jnp.float32
BF16 = jnp.bfloat16

D_MODEL = 1024
N_HEADS = 8
D_QK = 64
D_V = 128
D_HQ = N_HEADS * D_QK
D_HV = N_HEADS * D_V
D_FF = 2816
N_ADA = 9
CONV_W = 3
GATE_SOFTCAP = 15.0
EPS = 1e-6

LANES = 128
BF16_ROWS = 16
MXU_COLS = 256
VMEM_LIMIT_BYTES = 56 * 1024 * 1024

FFN_ROWS = 512
FFN_SUB = MXU_COLS
ADA_COLS = 3 * D_MODEL
MIX_ROWS = 1024
ML_CHUNK = LANES
ML_STATE_ROWS = D_V + BF16_ROWS
SAMPLE_BLOCK = 16


def _params(*sem):
    return pltpu.CompilerParams(dimension_semantics=sem, vmem_limit_bytes=VMEM_LIMIT_BYTES)


def _sigmoid(x):
    return 0.5 * jnp.tanh(0.5 * x) + 0.5


def _rms(x):
    return lax.rsqrt(jnp.mean(x * x, axis=-1, keepdims=True) + EPS)


def _norm_mod(x, gain, scale, shift):
    return x * _rms(x) * (gain * (1.0 + scale)) + shift


def _dot(a, b):
    return jnp.dot(a, b, preferred_element_type=F32)


def _dot_nt(a, b):
    return lax.dot_general(a, b, (((1,), (1,)), ((), ())), preferred_element_type=F32)


def _dot_tn(a, b):
    return lax.dot_general(a, b, (((0,), (0,)), ((), ())), preferred_element_type=F32)


def _log_sigmoid(x):
    return -(jnp.maximum(-x, 0.0) + jnp.log1p(jnp.exp(-jnp.abs(x))))


def _resident(shape):
    return pl.BlockSpec(shape, lambda *_: tuple(0 for _ in shape), pipeline_mode=pl.Buffered(1))


def _ada_kernel(cp_ref, cs_ref, w_ref, b_ref, op_ref, os_ref):
    w = w_ref[...].astype(BF16)
    for c_ref, o_ref in ((cp_ref, op_ref), (cs_ref, os_ref)):
        c = c_ref[...]
        res = _dot((c * _sigmoid(c)).astype(BF16), w)
        for k in range(o_ref.shape[0]):
            o_ref[k] = res[:, k * D_MODEL:(k + 1) * D_MODEL] + b_ref[k]


def _ada(c_prompt, c_sample, w_ada, b_ada):
    depth = w_ada.shape[0]
    per_step = ADA_COLS // D_MODEL
    out = lambda c: (pl.BlockSpec((None, per_step, c.shape[0], D_MODEL), lambda l, j: (l, j, 0, 0)),
                     jax.ShapeDtypeStruct((depth, N_ADA, c.shape[0], D_MODEL), F32))
    (spec_p, shape_p), (spec_s, shape_s) = out(c_prompt), out(c_sample)
    return pl.pallas_call(
        _ada_kernel,
        grid=(depth, N_ADA // per_step),
        in_specs=[
            pl.BlockSpec(c_prompt.shape, lambda l, j: (0, 0)),
            pl.BlockSpec(c_sample.shape, lambda l, j: (0, 0)),
            pl.BlockSpec((None, D_MODEL, ADA_COLS), lambda l, j: (l, 0, j)),
            pl.BlockSpec((None, per_step, 1, D_MODEL), lambda l, j: (l, j, 0, 0)),
        ],
        out_specs=[spec_p, spec_s],
        out_shape=[shape_p, shape_s],
        compiler_params=_params("arbitrary", "arbitrary"),
        name="ada",
    )(c_prompt, c_sample, w_ada, b_ada.reshape(depth, N_ADA, 1, D_MODEL))


def _group_mod(mod_ref, j):
    return mod_ref[j, pl.ds(pl.program_id(0), 1), :]


def _swiglu_chunk(hn, wg_ref, wu_ref, wd_ref, c):
    g = _dot(hn, wg_ref[c])
    u = _dot(hn, wu_ref[c])
    return _dot((g * _sigmoid(g) * u).astype(BF16), wd_ref[c])


def _ffn_kernel(xp_ref, xs_ref, modp_ref, mods_ref, gpre_ref, gpost_ref, wg_hbm, wu_hbm, wd_hbm,
                op_ref, os_ref,
                wg_s, wu_s, wd_s, up_stage, dn_stage, sem, hn_s, acc_s, *,
                layer, half, n_tiles, tiles_per_group, sub):
    s = pl.program_id(0)
    n_chunks = wg_s.shape[0]

    def chunk_copies(c):
        slot = c % 2
        cols = pl.ds(c * sub, sub)
        return (pltpu.make_async_copy(wg_hbm.at[layer, half, :, cols], up_stage.at[slot, 0], sem.at[slot, 0]),
                pltpu.make_async_copy(wu_hbm.at[layer, half, :, cols], up_stage.at[slot, 1], sem.at[slot, 1]),
                pltpu.make_async_copy(wd_hbm.at[layer, half, cols, :], dn_stage.at[slot], sem.at[slot, 2]))

    def stage_chunk(c):
        if c + 1 < n_chunks:
            for copy in chunk_copies(c + 1):
                copy.start()
        for copy in chunk_copies(c):
            copy.wait()
        wg_s[c] = up_stage[c % 2, 0].astype(BF16)
        wu_s[c] = up_stage[c % 2, 1].astype(BF16)
        wd_s[c] = dn_stage[c % 2].astype(BF16)

    def half_step(x_ref, o_ref, scale, shift, gate, stage_weights=False, unrolled=True):
        rows = x_ref.shape[0]
        x = x_ref[...]
        hn_s[0:rows, :] = _norm_mod(x, gpre_ref[...], scale, shift).astype(BF16)
        if unrolled:
            hn = hn_s[0:rows, :]
            for c in range(n_chunks):
                if stage_weights:
                    stage_chunk(c)
                down = _swiglu_chunk(hn, wg_s, wu_s, wd_s, c)
                if c == 0:
                    acc_s[0:rows, :] = down
                else:
                    acc_s[0:rows, :] += down
        else:
            acc_s[0:rows, :] = jnp.zeros((rows, D_MODEL), F32)

            @pl.loop(0, n_chunks)
            def _(c):
                acc_s[0:rows, :] += _swiglu_chunk(hn_s[0:rows, :], wg_s, wu_s, wd_s, c)
        y = acc_s[0:rows, :]
        o_ref[...] = x + y * _rms(y) * (gpost_ref[...] * (0.5 * (1.0 + gate)))

    group = jnp.minimum(s, n_tiles - 1) // tiles_per_group
    group_mod = lambda j: modp_ref[j, pl.ds(group, 1), :]

    @pl.when(s == 0)
    def _():
        for copy in chunk_copies(0):
            copy.start()
        half_step(xp_ref, op_ref, group_mod(1), group_mod(0), group_mod(2), stage_weights=True)

    @pl.when((s > 0) & (s < n_tiles))
    def _():
        half_step(xp_ref, op_ref, group_mod(1), group_mod(0), group_mod(2))

    @pl.when(s == n_tiles)
    def _():
        half_step(xs_ref, os_ref, mods_ref[1], mods_ref[0], mods_ref[2], unrolled=False)


def _ffn(xp, xs, mod_p, mod_s, g_pre, g_post, wg, wu, wd, layer, half):
    groups, seq, _ = xp.shape
    n_sample = xs.shape[0]
    rows = FFN_ROWS
    tpg = seq // rows
    n_tiles = groups * tpg
    sub = FFN_SUB
    tile = lambda s: jnp.minimum(s, n_tiles - 1)
    xp_spec = pl.BlockSpec((None, rows, D_MODEL), lambda s: (tile(s) // tpg, tile(s) % tpg, 0))
    xs_spec = pl.BlockSpec((n_sample, D_MODEL), lambda s: (0, 0))
    gain = pl.BlockSpec((None, None, 1, D_MODEL), lambda s: (layer, 2 * half, 0, 0))
    hbm = pl.BlockSpec(memory_space=pl.ANY)
    return pl.pallas_call(
        functools.partial(_ffn_kernel, layer=layer, half=half, n_tiles=n_tiles, tiles_per_group=tpg, sub=sub),
        grid=(n_tiles + 1,),
        in_specs=[
            xp_spec,
            xs_spec,
            pl.BlockSpec((None, 3, groups, D_MODEL), lambda s: (layer, 2 * half, 0, 0)),
            pl.BlockSpec((None, 3, n_sample, D_MODEL), lambda s: (layer, 2 * half, 0, 0)),
            gain,
            gain,
            hbm,
            hbm,
            hbm,
        ],
        out_specs=[xp_spec, xs_spec],
        out_shape=[jax.ShapeDtypeStruct(xp.shape, F32), jax.ShapeDtypeStruct(xs.shape, F32)],
        scratch_shapes=[
            pltpu.VMEM((D_FF // sub, D_MODEL, sub), BF16),
            pltpu.VMEM((D_FF // sub, D_MODEL, sub), BF16),
            pltpu.VMEM((D_FF // sub, sub, D_MODEL), BF16),
            pltpu.VMEM((2, 2, D_MODEL, sub), F32),
            pltpu.VMEM((2, sub, D_MODEL), F32),
            pltpu.SemaphoreType.DMA((2, 3)),
            pltpu.VMEM((max(rows, n_sample), D_MODEL), BF16),
            pltpu.VMEM((max(rows, n_sample), D_MODEL), F32),
        ],
        compiler_params=_params("arbitrary"),
        name="ffn",
    )(xp, xs, mod_p, mod_s, g_pre, g_post, wg, wu, wd)


def _block_diag(a, b):
    za = jnp.zeros((a.shape[0], b.shape[1]), a.dtype)
    zb = jnp.zeros((b.shape[0], a.shape[1]), a.dtype)
    return jnp.concatenate([jnp.concatenate([a, za], axis=1), jnp.concatenate([zb, b], axis=1)], axis=0)


def _mlstm_kernel(x_ref, mod_ref, gpre_ref, gpost_ref, wt_ref, bif_ref, ght_ref, wout_ref,
                  o_ref, c_out_ref, n_out_ref, m_out_ref,
                  st_s, m_s, yt_s, *, chunk):
    t = pl.program_id(1)

    @pl.when(t == 0)
    def _():
        st_s[...] = jnp.zeros_like(st_s)
        m_s[...] = jnp.zeros_like(m_s)

    x = x_ref[...]
    rows = x.shape[0]
    h_in = _norm_mod(x, gpre_ref[...], _group_mod(mod_ref, 1), _group_mod(mod_ref, 0)).astype(BF16)
    proj_q = _dot_nt(wt_ref[0:D_HQ], h_in)
    proj_t = _dot_nt(wt_ref[2 * D_HQ:], h_in)
    k_all = _dot_nt(h_in, wt_ref[D_HQ:2 * D_HQ])
    r_v, r_o, r_i, r_f = 0, D_HV, 2 * D_HV, 2 * D_HV + N_HEADS

    s_idx = lax.broadcasted_iota(jnp.int32, (chunk, chunk), 0)
    t_idx = lax.broadcasted_iota(jnp.int32, (chunk, chunk), 1)
    causal = s_idx <= t_idx
    tri = jnp.where(causal, 1.0, 0.0).astype(BF16)
    lane = lax.broadcasted_iota(jnp.int32, (1, LANES), 1)
    low = lane < D_QK
    ones_rows = jnp.ones((BF16_ROWS, 2 * chunk), BF16)
    zero_rows = jnp.zeros((LANES - N_HEADS, chunk), F32)

    ig_all = GATE_SOFTCAP * jnp.tanh((proj_t[r_i:r_i + N_HEADS, :] + bif_ref[0:N_HEADS, :]) / GATE_SOFTCAP)
    lf_all = _log_sigmoid(proj_t[r_f:r_f + N_HEADS, :] + bif_ref[N_HEADS:, :])

    for c in range(rows // chunk):
        cs = slice(c * chunk, (c + 1) * chunk)
        lf = lf_all[:, cs]
        lf_hi = lf.astype(BF16).astype(F32)
        lf_mid = (lf - lf_hi).astype(BF16).astype(F32)
        lf_lo = lf - lf_hi - lf_mid
        b3 = _dot(jnp.concatenate([lf_hi, lf_mid, lf_lo, jnp.zeros_like(lf)], axis=0).astype(BF16), tri)
        b = b3[0:N_HEADS] + b3[N_HEADS:2 * N_HEADS] + b3[2 * N_HEADS:3 * N_HEADS]
        col = ig_all[:, cs] - b
        b_last = jnp.broadcast_to(b[:, chunk - 1:chunk], b.shape)
        col_s = jnp.concatenate([col, zero_rows], axis=0).T

        for p in range(N_HEADS // 2):
            h0, h1 = 2 * p, 2 * p + 1
            qt = [(proj_q[h * D_QK:(h + 1) * D_QK, cs] * (D_QK ** -0.5)).astype(BF16) for h in (h0, h1)]
            vt = [proj_t[r_v + h * D_V:r_v + (h + 1) * D_V, cs] for h in (h0, h1)]
            k_pair = k_all[cs, p * LANES:(p + 1) * LANES]
            state = st_s[p]
            lhs1 = jnp.concatenate([k_pair.astype(BF16), state.astype(BF16)], axis=0)
            r1 = _dot(lhs1, _block_diag(qt[0], qt[1]))
            probs, w_state, inv_floor, w_k, w_decay = [], [], [], [], []
            for i, h in enumerate((h0, h1)):
                m_prev = m_s[h:h + 1, :]
                col_m = jnp.where(causal, col_s[:, h:h + 1], -jnp.inf)
                g = jnp.maximum(m_prev, jnp.max(col_m, axis=0, keepdims=True))
                probs.append((jnp.exp(col_m - g) * r1[:chunk, i * chunk:(i + 1) * chunk]).astype(BF16))
                m_t = b[h:h + 1, :] + g
                m_new = jnp.broadcast_to(m_t[:, chunk - 1:chunk], m_t.shape)
                w_state.append(jnp.exp(m_prev - g))
                inv_floor.append(jnp.exp(-m_t))
                w_decay.append(jnp.exp(b_last[h:h + 1, :] + m_prev - m_new))
                w_k.append(jnp.exp(b_last[h:h + 1, :] - m_new + col[h:h + 1, :]))
                m_s[h:h + 1, :] = m_new
            lhs2 = jnp.concatenate([jnp.concatenate([vt[0], vt[1]], axis=1).astype(BF16), ones_rows], axis=0)
            r2 = _dot(lhs2, _block_diag(probs[0], probs[1]))
            for i, h in enumerate((h0, h1)):
                ls = slice(i * chunk, (i + 1) * chunk)
                den = w_state[i] * r1[chunk + D_V:chunk + D_V + 1, ls] + r2[D_V:D_V + 1, ls]
                inv = 1.0 / jnp.maximum(jnp.abs(den), inv_floor[i])
                ht = (w_state[i] * r1[chunk:chunk + D_V, ls] + r2[:D_V, ls]) * inv
                rn = lax.rsqrt(jnp.mean(ht * ht, axis=0, keepdims=True) + EPS)
                og = proj_t[r_o + h * D_V:r_o + (h + 1) * D_V, cs]
                yt_s[h * D_V:(h + 1) * D_V, cs] = (_sigmoid(og) * (ht * rn * ght_ref[h])).astype(BF16)
            lhs3 = jnp.concatenate(
                [jnp.concatenate([vt[i] * w_k[i] for i in range(2)], axis=1),
                 jnp.concatenate([jnp.broadcast_to(w_k[i], (BF16_ROWS, chunk)) for i in range(2)], axis=1)],
                axis=0).astype(BF16)
            rhs3 = jnp.concatenate([jnp.where(low, k_pair, 0.0), jnp.where(low, 0.0, k_pair)], axis=0).astype(BF16)
            decay = jnp.where(low, w_decay[0], w_decay[1])
            st_s[p] = state * decay + _dot(lhs3, rhs3)

    y = _dot_tn(yt_s[...], wout_ref[...])
    o_ref[...] = x + y * _rms(y) * (gpost_ref[...] * (1.0 + _group_mod(mod_ref, 2)))

    @pl.when(t == pl.num_programs(1) - 1)
    def _():
        for p in range(N_HEADS // 2):
            state = st_s[p]
            c_out_ref[p * LANES:(p + 1) * LANES, :] = state[:D_V, :].T
            n_out_ref[:, p * LANES:(p + 1) * LANES] = state[D_V:D_V + 1, :]
        m_out_ref[...] = m_s[...]


def _mixer_mod_specs(layer, batch):
    gain = pl.BlockSpec((None, None, 1, D_MODEL), lambda b, t: (layer, 1, 0, 0))
    return [pl.BlockSpec((None, 3, batch, D_MODEL), lambda b, t: (layer, 1, 0, 0)), gain, gain]


def _mlstm_prompt(x, mod, g_pre, g_post, layer, w_t, b_if, g_head_t, w_out):
    batch, seq, _ = x.shape
    rows = MIX_ROWS
    return pl.pallas_call(
        functools.partial(_mlstm_kernel, chunk=ML_CHUNK),
        grid=(batch, seq // rows),
        in_specs=[
            pl.BlockSpec((None, rows, D_MODEL), lambda b, t: (b, t, 0)),
            *_mixer_mod_specs(layer, batch),
            _resident(w_t.shape),
            _resident(b_if.shape),
            _resident(g_head_t.shape),
            _resident(w_out.shape),
        ],
        out_specs=[
            pl.BlockSpec((None, rows, D_MODEL), lambda b, t: (b, t, 0)),
            pl.BlockSpec((None, D_HQ, D_V), lambda b, t: (b, 0, 0)),
            pl.BlockSpec((None, 1, D_HQ), lambda b, t: (b, 0, 0)),
            pl.BlockSpec((None, N_HEADS, LANES), lambda b, t: (b, 0, 0)),
        ],
        out_shape=[
            jax.ShapeDtypeStruct(x.shape, F32),
            jax.ShapeDtypeStruct((batch, D_HQ, D_V), F32),
            jax.ShapeDtypeStruct((batch, 1, D_HQ), F32),
            jax.ShapeDtypeStruct((batch, N_HEADS, LANES), F32),
        ],
        scratch_shapes=[
            pltpu.VMEM((N_HEADS // 2, ML_STATE_ROWS, LANES), F32),
            pltpu.VMEM((N_HEADS, LANES), F32),
            pltpu.VMEM((D_HV, rows), BF16),
        ],
        compiler_params=_params("arbitrary", "arbitrary"),
        name="mlstm_prompt",
    )(x, mod, g_pre, g_post, w_t, b_if, g_head_t, w_out)


def _pair_expand(cols):
    lane = lax.broadcasted_iota(jnp.int32, (cols[0].shape[0], LANES), 1)
    return jnp.concatenate(
        [jnp.where(lane < D_QK, cols[2 * p], cols[2 * p + 1]) for p in range(N_HEADS // 2)], axis=1)


def _row_to_head_tile(rows, b):
    return jnp.concatenate([rows[b:b + 1, h * D_V:(h + 1) * D_V] for h in range(N_HEADS)], axis=0)


def _mlstm_step_kernel(x_ref, mod_ref, gpre_ref, gpost_ref, wt_ref, bif_ref, ghead_ref, wout_ref,
                       c_ref, n_ref, m_ref,
                       o_ref, c_out_ref, n_out_ref, m_out_ref,
                       v_s, og_s, a_s, b_s, y_s, qt_s, kwt_s, wst_s):
    i = pl.program_id(0)
    blk = c_ref.shape[0]
    rows = x_ref.shape[0]

    @pl.when(i == 0)
    def _():
        x = x_ref[...]
        h_in = _norm_mod(x, gpre_ref[...], mod_ref[1], mod_ref[0]).astype(BF16)
        proj = _dot_nt(h_in, wt_ref[...])
        gates = proj[:, 2 * D_HQ + 2 * D_HV:] + bif_ref[...]
        ig = (GATE_SOFTCAP * jnp.tanh(gates / GATE_SOFTCAP))[:, :N_HEADS]
        lf = _log_sigmoid(gates)[:, N_HEADS:]
        m_prev = m_ref[...]
        st = lf + m_prev
        m_t = jnp.maximum(st, ig)
        w_i = jnp.exp(ig - m_t)
        w_s = jnp.exp(st - m_t)
        floor = jnp.exp(-m_t)
        m_out_ref[...] = m_t
        q = proj[:, :D_HQ] * (D_QK ** -0.5)
        k = proj[:, D_HQ:2 * D_HQ]
        n_prev = n_ref[...]
        lane = lax.broadcasted_iota(jnp.int32, (rows, LANES), 1)
        lo = lane < D_QK

        def head_sums(z):
            out = []
            for p in range(N_HEADS // 2):
                zp = z[:, p * LANES:(p + 1) * LANES]
                out.append(jnp.sum(jnp.where(lo, zp, 0.0), axis=-1, keepdims=True))
                out.append(jnp.sum(jnp.where(lo, 0.0, zp), axis=-1, keepdims=True))
            return out

        qk_h = head_sums(q * k)
        qn_h = head_sums(q * n_prev)
        ws_cols, wi_cols = [], []
        for hd in range(N_HEADS):
            ws_h = w_s[:, hd:hd + 1]
            wi_h = w_i[:, hd:hd + 1]
            s_h = qk_h[hd] * wi_h
            den = ws_h * qn_h[hd] + s_h
            inv = 1.0 / jnp.maximum(jnp.abs(den), floor[:, hd:hd + 1])
            a_s[:, hd * D_V:(hd + 1) * D_V] = jnp.broadcast_to(ws_h * inv, (rows, D_V))
            b_s[:, hd * D_V:(hd + 1) * D_V] = jnp.broadcast_to(s_h * inv, (rows, D_V))
            ws_cols.append(ws_h)
            wi_cols.append(wi_h)
        ws_x = _pair_expand(ws_cols)
        kw = k * _pair_expand(wi_cols)
        n_out_ref[...] = ws_x * n_prev + kw
        v_s[...] = proj[:, 2 * D_HQ:2 * D_HQ + D_HV]
        og_s[...] = proj[:, 2 * D_HQ + D_HV:2 * D_HQ + 2 * D_HV]
        for dst, src in ((qt_s, q), (kwt_s, kw), (wst_s, ws_x)):
            src_t = src.T
            for j in range(rows // blk):
                dst[j] = src_t[:, j * blk:(j + 1) * blk]

    block_rows = pl.ds(pl.multiple_of(i * blk, blk), blk)
    v_blk, og_blk, a_blk, b_blk = v_s[block_rows, :], og_s[block_rows, :], a_s[block_rows, :], b_s[block_rows, :]
    y_tiles = []
    for b in range(blk):
        v_i = _row_to_head_tile(v_blk, b)
        readout = []
        for hd in range(N_HEADS):
            rs = slice(hd * D_QK, (hd + 1) * D_QK)
            c_h = c_ref[b, hd]
            readout.append(jnp.sum(c_h * qt_s[i, rs, b:b + 1], axis=0, keepdims=True))
            c_out_ref[b, hd] = c_h * wst_s[i, rs, b:b + 1] + kwt_s[i, rs, b:b + 1] * v_i[hd:hd + 1, :]
        hh = _row_to_head_tile(a_blk, b) * jnp.concatenate(readout, axis=0) + _row_to_head_tile(b_blk, b) * v_i
        y_tiles.append(_sigmoid(_row_to_head_tile(og_blk, b)) * (hh * _rms(hh) * ghead_ref[...]))
    y_s[block_rows, :] = jnp.concatenate(
        [jnp.concatenate([y_tiles[b][hd:hd + 1, :] for b in range(blk)], axis=0) for hd in range(N_HEADS)], axis=1)

    @pl.when(i == pl.num_programs(0) - 1)
    def _():
        y = _dot(y_s[...].astype(BF16), wout_ref[...])
        o_ref[...] = x_ref[...] + y * _rms(y) * (gpost_ref[...] * (1.0 + mod_ref[2]))


def _mlstm_step(x, mod, g_pre, g_post, layer, w_t, b_if, g_head, w_out, c0, n0, m0):
    rows = x.shape[0]
    blk = SAMPLE_BLOCK
    nblk = rows // blk
    const = lambda shape: pl.BlockSpec(shape, lambda i: tuple(0 for _ in shape))
    gain = pl.BlockSpec((None, None, 1, D_MODEL), lambda i: (layer, 1, 0, 0))
    c_spec = pl.BlockSpec((blk, N_HEADS, D_QK, D_V), lambda i: (i, 0, 0, 0))
    n_flat = n0.reshape(rows, D_HQ)
    y, c_new, n_new, m_new = pl.pallas_call(
        _mlstm_step_kernel,
        grid=(nblk,),
        in_specs=[
            const(x.shape),
            pl.BlockSpec((None, 3, rows, D_MODEL), lambda i: (layer, 1, 0, 0)),
            gain,
            gain,
            _resident(w_t.shape),
            const(b_if.shape),
            const(g_head.shape),
            _resident(w_out.shape),
            c_spec,
            const(n_flat.shape),
            const(m0.shape),
        ],
        out_specs=[const(x.shape), c_spec, const(n_flat.shape), const(m0.shape)],
        out_shape=[jax.ShapeDtypeStruct(x.shape, F32), jax.ShapeDtypeStruct(c0.shape, F32),
                   jax.ShapeDtypeStruct(n_flat.shape, F32), jax.ShapeDtypeStruct(m0.shape, F32)],
        scratch_shapes=[pltpu.VMEM((rows, D_HV), F32)] * 5 + [pltpu.VMEM((nblk, D_HQ, blk), F32)] * 3,
        compiler_params=_params("arbitrary"),
        name="mlstm_step",
    )(x, mod, g_pre, g_post, w_t, b_if, g_head, w_out, c0, n_flat, m0)
    return y, c_new, n_new.reshape(rows, N_HEADS, D_QK), m_new


def _conv_kernel(x_ref, mod_ref, gpre_ref, gpost_ref, win_ref, cw_ref, wout_ref, o_ref, buf_out_ref, tail_s):
    t = pl.program_id(1)

    @pl.when(t == 0)
    def _():
        tail_s[...] = jnp.zeros_like(tail_s)

    x = x_ref[...]
    rows = x.shape[0]
    h_in = _norm_mod(x, gpre_ref[...], _group_mod(mod_ref, 1), _group_mod(mod_ref, 0)).astype(BF16)
    p = _dot(h_in, win_ref[...])
    bg = p[:, :D_MODEL]
    u = p[:, D_MODEL:2 * D_MODEL] * p[:, 2 * D_MODEL:]
    prev1 = tail_s[7:8, :]
    prev2 = tail_s[6:7, :]
    ridx = lax.broadcasted_iota(jnp.int32, u.shape, 0)
    u1 = jnp.where(ridx == 0, prev1, pltpu.roll(u, 1, 0))
    u2 = jnp.where(ridx == 0, prev2, jnp.where(ridx == 1, prev1, pltpu.roll(u, 2, 0)))
    conv = cw_ref[0:1, :] * u2 + cw_ref[1:2, :] * u1 + cw_ref[2:3, :] * u
    tail_s[...] = u[rows - 8:, :]
    y = _dot((bg * conv).astype(BF16), wout_ref[...])
    o_ref[...] = x + y * _rms(y) * (gpost_ref[...] * (1.0 + _group_mod(mod_ref, 2)))

    @pl.when(t == pl.num_programs(1) - 1)
    def _():
        buf_out_ref[...] = u[rows - (CONV_W - 1):, :]


def _conv_prompt(x, mod, g_pre, g_post, layer, w_in, conv_w, w_out):
    batch, seq, _ = x.shape
    rows = MIX_ROWS
    return pl.pallas_call(
        _conv_kernel,
        grid=(batch, seq // rows),
        in_specs=[
            pl.BlockSpec((None, rows, D_MODEL), lambda b, t: (b, t, 0)),
            *_mixer_mod_specs(layer, batch),
            _resident(w_in.shape),
            _resident(conv_w.shape),
            _resident(w_out.shape),
        ],
        out_specs=[
            pl.BlockSpec((None, rows, D_MODEL), lambda b, t: (b, t, 0)),
            pl.BlockSpec((None, CONV_W - 1, D_MODEL), lambda b, t: (b, 0, 0)),
        ],
        out_shape=[jax.ShapeDtypeStruct(x.shape, F32),
                   jax.ShapeDtypeStruct((batch, CONV_W - 1, D_MODEL), F32)],
        scratch_shapes=[pltpu.VMEM((8, D_MODEL), F32)],
        compiler_params=_params("arbitrary", "arbitrary"),
        name="conv_prompt",
    )(x, mod, g_pre, g_post, w_in, conv_w, w_out)


def _conv_sample_kernel(x_ref, mod_ref, gpre_ref, gpost_ref, win_ref, cw_ref, wout_ref, buf_ref,
                        o_ref, buf_out_ref):
    x = x_ref[...]
    h_in = _norm_mod(x, gpre_ref[...], mod_ref[1], mod_ref[0]).astype(BF16)
    p = _dot(h_in, win_ref[...])
    bg = p[:, :D_MODEL]
    u = p[:, D_MODEL:2 * D_MODEL] * p[:, 2 * D_MODEL:]
    conv = cw_ref[0:1, :] * buf_ref[0] + cw_ref[1:2, :] * buf_ref[1] + cw_ref[2:3, :] * u
    y = _dot((bg * conv).astype(BF16), wout_ref[...])
    o_ref[...] = x + (1.0 + mod_ref[2]) * (y * _rms(y) * gpost_ref[...])
    buf_out_ref[0] = buf_ref[1]
    buf_out_ref[1] = u


def _conv_sample(x, mod, g_pre, g_post, w_in, conv_w, w_out, buf):
    full = lambda shape: pl.BlockSpec(shape, lambda: tuple(0 for _ in shape))
    ins = (x, mod, g_pre.reshape(1, D_MODEL), g_post.reshape(1, D_MODEL), w_in, conv_w, w_out, buf)
    return pl.pallas_call(
        _conv_sample_kernel,
        in_specs=[full(z.shape) for z in ins],
        out_specs=[full(x.shape), full(buf.shape)],
        out_shape=[jax.ShapeDtypeStruct(x.shape, F32), jax.ShapeDtypeStruct(buf.shape, F32)],
        compiler_params=_params(),
        name="conv_sample",
    )(*ins)


def kernel(x_prompt, x_sample, c_prompt, c_sample, state_mlstm_C, state_mlstm_n, state_mlstm_m, state_conv,
           w_ada, b_ada, g_pre, g_post, ffn_wg, ffn_wu, ffn_wd,
           ml_w_in, ml_b_i, ml_b_f, ml_g_head, ml_w_out, cv_w_in, cv_conv_w, cv_w_out):
    depth = w_ada.shape[0]
    batch, seq, _ = x_prompt.shape
    n_sample = x_sample.shape[0]
    n_ml = ml_w_in.shape[0]
    assert x_sample.shape[1] == 1 and seq % ML_CHUNK == 0

    mod_p, mod_s = _ada(c_prompt, c_sample, w_ada, b_ada)
    gp = g_pre.reshape(depth, 3, 1, D_MODEL)
    gq = g_post.reshape(depth, 3, 1, D_MODEL)

    ml_t = ml_w_in.transpose(0, 2, 1).astype(BF16)
    ml_bif = jnp.concatenate([ml_b_i, ml_b_f], axis=-1)
    ml_bif_t = jnp.broadcast_to(ml_bif[:, :, None], (n_ml, 2 * N_HEADS, MIX_ROWS))
    ml_ghead_t = jnp.broadcast_to(ml_g_head[:, :, :, None], (n_ml, N_HEADS, D_V, LANES))
    ml_out = ml_w_out.astype(BF16)
    cv_in = cv_w_in.astype(BF16)
    cv_out = cv_w_out.astype(BF16)

    xp = x_prompt
    xs = x_sample.reshape(n_sample, D_MODEL)
    p_c, p_n, p_m, p_buf, s_c, s_n, s_m, s_buf = [], [], [], [], [], [], [], []
    for l in range(depth):
        ms = lambda s: mod_s[l, 3 * s:3 * s + 3]
        xp, xs = _ffn(xp, xs, mod_p, mod_s, gp, gq, ffn_wg, ffn_wu, ffn_wd, l, 0)
        j = l // 2
        if l % 2 == 0:
            xp, c_j, n_j, m_j = _mlstm_prompt(xp, mod_p, gp, gq, l, ml_t[j], ml_bif_t[j], ml_ghead_t[j], ml_out[j])
            p_c.append(c_j.reshape(batch, N_HEADS, D_QK, D_V))
            p_n.append(n_j.reshape(batch, N_HEADS, D_QK))
            p_m.append(m_j[:, :, 0])
            ys, c_j, n_j, m_j = _mlstm_step(xs, mod_s, gp, gq, l, ml_t[j], ml_bif[j][None], ml_g_head[j], ml_out[j],
                                            state_mlstm_C[j], state_mlstm_n[j], state_mlstm_m[j])
            xs = ys
            s_c.append(c_j)
            s_n.append(n_j)
            s_m.append(m_j)
        else:
            xp, buf_j = _conv_prompt(xp, mod_p, gp, gq, l, cv_in[j], cv_conv_w[j], cv_out[j])
            p_buf.append(buf_j)
            ys, buf_j = _conv_sample(xs, ms(1), g_pre[l, 1], g_post[l, 1], cv_in[j], cv_conv_w[j], cv_out[j],
                                     state_conv[j].transpose(1, 0, 2))
            xs = ys
            s_buf.append(buf_j.transpose(1, 0, 2))
        xp, xs = _ffn(xp, xs, mod_p, mod_s, gp, gq, ffn_wg, ffn_wu, ffn_wd, l, 1)

    return (xp, xs.reshape(n_sample, 1, D_MODEL),
            jnp.stack(p_c), jnp.stack(p_n), jnp.stack(p_m), jnp.stack(p_buf),
            jnp.stack(s_c), jnp.stack(s_n), jnp.stack(s_m), jnp.stack(s_buf))
```

```python
import functools

import jax
import jax.numpy as jnp
from jax import lax
from jax.experimental import pallas as pl
from jax.experimental.pallas import tpu as pltpu

F32 = jnp.float32
BF16 = jnp.bfloat16

D_MODEL = 1024
N_HEADS = 8
D_QK = 64
D_V = 128
D_HQ = N_HEADS * D_QK
D_HV = N_HEADS * D_V
D_FF = 2816
N_ADA = 9
CONV_W = 3
GATE_SOFTCAP = 15.0
EPS = 1e-6

LANES = 128
BF16_ROWS = 16
MXU_COLS = 256
VMEM_LIMIT_BYTES = 56 * 1024 * 1024

FFN_ROWS = 512
FFN_SUB = MXU_COLS
ADA_ROWS = LANES
MIX_ROWS = 1024
ML_CHUNK = LANES
ML_STATE_ROWS = D_V + BF16_ROWS
SAMPLE_BLOCK = 16


def _params(*sem):
    return pltpu.CompilerParams(dimension_semantics=sem, vmem_limit_bytes=VMEM_LIMIT_BYTES)


def _sigmoid(x):
    return 0.5 * jnp.tanh(0.5 * x) + 0.5


def _rms(x):
    return lax.rsqrt(jnp.mean(x * x, axis=-1, keepdims=True) + EPS)


def _norm_mod(x, gain, scale, shift):
    return x * _rms(x) * (gain * (1.0 + scale)) + shift


def _dot(a, b):
    return jnp.dot(a, b, preferred_element_type=F32)


def _dot_nt(a, b):
    return lax.dot_general(a, b, (((1,), (1,)), ((), ())), preferred_element_type=F32)


def _dot_tn(a, b):
    return lax.dot_general(a, b, (((0,), (0,)), ((), ())), preferred_element_type=F32)


def _log_sigmoid(x):
    return -(jnp.maximum(-x, 0.0) + jnp.log1p(jnp.exp(-jnp.abs(x))))


def _resident(shape):
    return pl.BlockSpec(shape, lambda *_: tuple(0 for _ in shape), pipeline_mode=pl.Buffered(1))


def _ada_kernel(cp_ref, cs_ref, w_ref, b_ref, op_ref, os_ref):
    k = pl.program_id(1)
    n_p = cp_ref.shape[0]

    @pl.when(k == 0)
    def _():
        for j in range(N_ADA):
            op_ref[j] = jnp.broadcast_to(b_ref[j], op_ref.shape[1:])
            os_ref[j] = jnp.broadcast_to(b_ref[j], os_ref.shape[1:])

    c = jnp.concatenate([cp_ref[...], cs_ref[...]], axis=0)
    res = _dot((c * _sigmoid(c)).astype(BF16), w_ref[...].astype(BF16))
    for j in range(N_ADA):
        cols = slice(j * D_MODEL, (j + 1) * D_MODEL)
        op_ref[j] += res[:n_p, cols]
        os_ref[j] += res[n_p:, cols]


def _ada(c_prompt, c_sample, w_ada, b_ada):
    depth = w_ada.shape[0]
    out = lambda c: (pl.BlockSpec((None, N_ADA, c.shape[0], D_MODEL), lambda l, k: (l, 0, 0, 0)),
                     jax.ShapeDtypeStruct((depth, N_ADA, c.shape[0], D_MODEL), F32))
    (spec_p, shape_p), (spec_s, shape_s) = out(c_prompt), out(c_sample)
    return pl.pallas_call(
        _ada_kernel,
        grid=(depth, D_MODEL // ADA_ROWS),
        in_specs=[
            pl.BlockSpec((c_prompt.shape[0], ADA_ROWS), lambda l, k: (0, k)),
            pl.BlockSpec((c_sample.shape[0], ADA_ROWS), lambda l, k: (0, k)),
            pl.BlockSpec((None, ADA_ROWS, N_ADA * D_MODEL), lambda l, k: (l, k, 0)),
            pl.BlockSpec((None, N_ADA, 1, D_MODEL), lambda l, k: (l, 0, 0, 0)),
        ],
        out_specs=[spec_p, spec_s],
        out_shape=[shape_p, shape_s],
        compiler_params=_params("arbitrary", "arbitrary"),
        name="ada",
    )(c_prompt, c_sample, w_ada, b_ada.reshape(depth, N_ADA, 1, D_MODEL))


def _group_mod(mod_ref, j):
    return mod_ref[j, pl.ds(pl.program_id(0), 1), :]


def _swiglu_chunk(hn, wg_ref, wu_ref, wd_ref, c):
    g = _dot(hn, wg_ref[c])
    u = _dot(hn, wu_ref[c])
    return _dot((g * _sigmoid(g) * u).astype(BF16), wd_ref[c])


def _ffn_kernel(xp_ref, xs_ref, modp_ref, mods_ref, gpre_ref, gpost_ref, wg_hbm, wu_hbm, wd_hbm,
                op_ref, os_ref,
                wg_s, wu_s, wd_s, up_stage, dn_stage, sem, hn_s, acc_s, *,
                layer, half, n_tiles, tiles_per_group, sub):
    s = pl.program_id(0)
    n_chunks = wg_s.shape[0]

    def chunk_copies(c):
        slot = c % 2
        cols = pl.ds(c * sub, sub)
        return (pltpu.make_async_copy(wg_hbm.at[layer, half, :, cols], up_stage.at[slot, 0], sem.at[slot, 0]),
                pltpu.make_async_copy(wu_hbm.at[layer, half, :, cols], up_stage.at[slot, 1], sem.at[slot, 1]),
                pltpu.make_async_copy(wd_hbm.at[layer, half, cols, :], dn_stage.at[slot], sem.at[slot, 2]))

    def stage_chunk(c):
        if c + 1 < n_chunks:
            for copy in chunk_copies(c + 1):
                copy.start()
        for copy in chunk_copies(c):
            copy.wait()
        wg_s[c] = up_stage[c % 2, 0].astype(BF16)
        wu_s[c] = up_stage[c % 2, 1].astype(BF16)
        wd_s[c] = dn_stage[c % 2].astype(BF16)

    def half_step(x_ref, o_ref, scale, shift, gate, stage_weights=False, unrolled=True):
        rows = x_ref.shape[0]
        x = x_ref[...]
        hn_s[0:rows, :] = _norm_mod(x, gpre_ref[...], scale, shift).astype(BF16)
        if unrolled:
            hn = hn_s[0:rows, :]
            for c in range(n_chunks):
                if stage_weights:
                    stage_chunk(c)
                down = _swiglu_chunk(hn, wg_s, wu_s, wd_s, c)
                if c == 0:
                    acc_s[0:rows, :] = down
                else:
                    acc_s[0:rows, :] += down
        else:
            acc_s[0:rows, :] = jnp.zeros((rows, D_MODEL), F32)

            @pl.loop(0, n_chunks)
            def _(c):
                acc_s[0:rows, :] += _swiglu_chunk(hn_s[0:rows, :], wg_s, wu_s, wd_s, c)
        y = acc_s[0:rows, :]
        o_ref[...] = x + y * _rms(y) * (gpost_ref[...] * (0.5 * (1.0 + gate)))

    group = jnp.minimum(s, n_tiles - 1) // tiles_per_group
    group_mod = lambda j: modp_ref[j, pl.ds(group, 1), :]

    @pl.when(s == 0)
    def _():
        for copy in chunk_copies(0):
            copy.start()
        half_step(xp_ref, op_ref, group_mod(1), group_mod(0), group_mod(2), stage_weights=True)

    @pl.when((s > 0) & (s < n_tiles))
    def _():
        half_step(xp_ref, op_ref, group_mod(1), group_mod(0), group_mod(2))

    @pl.when(s == n_tiles)
    def _():
        half_step(xs_ref, os_ref, mods_ref[1], mods_ref[0], mods_ref[2], unrolled=False)


def _ffn(xp, xs, mod_p, mod_s, g_pre, g_post, wg, wu, wd, layer, half):
    groups, seq, _ = xp.shape
    n_sample = xs.shape[0]
    rows = FFN_ROWS
    tpg = seq // rows
    n_tiles = groups * tpg
    sub = FFN_SUB
    tile = lambda s: jnp.minimum(s, n_tiles - 1)
    xp_spec = pl.BlockSpec((None, rows, D_MODEL), lambda s: (tile(s) // tpg, tile(s) % tpg, 0))
    xs_spec = pl.BlockSpec((n_sample, D_MODEL), lambda s: (0, 0))
    gain = pl.BlockSpec((None, None, 1, D_MODEL), lambda s: (layer, 2 * half, 0, 0))
    hbm = pl.BlockSpec(memory_space=pl.ANY)
    return pl.pallas_call(
        functools.partial(_ffn_kernel, layer=layer, half=half, n_tiles=n_tiles, tiles_per_group=tpg, sub=sub),
        grid=(n_tiles + 1,),
        in_specs=[
            xp_spec,
            xs_spec,
            pl.BlockSpec((None, 3, groups, D_MODEL), lambda s: (layer, 2 * half, 0, 0)),
            pl.BlockSpec((None, 3, n_sample, D_MODEL), lambda s: (layer, 2 * half, 0, 0)),
            gain,
            gain,
            hbm,
            hbm,
            hbm,
        ],
        out_specs=[xp_spec, xs_spec],
        out_shape=[jax.ShapeDtypeStruct(xp.shape, F32), jax.ShapeDtypeStruct(xs.shape, F32)],
        scratch_shapes=[
            pltpu.VMEM((D_FF // sub, D_MODEL, sub), BF16),
            pltpu.VMEM((D_FF // sub, D_MODEL, sub), BF16),
            pltpu.VMEM((D_FF // sub, sub, D_MODEL), BF16),
            pltpu.VMEM((2, 2, D_MODEL, sub), F32),
            pltpu.VMEM((2, sub, D_MODEL), F32),
            pltpu.SemaphoreType.DMA((2, 3)),
            pltpu.VMEM((max(rows, n_sample), D_MODEL), BF16),
            pltpu.VMEM((max(rows, n_sample), D_MODEL), F32),
        ],
        compiler_params=_params("arbitrary"),
        name="ffn",
    )(xp, xs, mod_p, mod_s, g_pre, g_post, wg, wu, wd)


def _block_diag(a, b):
    za = jnp.zeros((a.shape[0], b.shape[1]), a.dtype)
    zb = jnp.zeros((b.shape[0], a.shape[1]), a.dtype)
    return jnp.concatenate([jnp.concatenate([a, za], axis=1), jnp.concatenate([zb, b], axis=1)], axis=0)


def _mlstm_kernel(x_ref, mod_ref, gpre_ref, gpost_ref, wt_ref, bif_ref, ght_ref, wout_ref,
                  o_ref, c_out_ref, n_out_ref, m_out_ref,
                  st_s, m_s, yt_s, *, chunk):
    t = pl.program_id(1)

    @pl.when(t == 0)
    def _():
        st_s[...] = jnp.zeros_like(st_s)
        m_s[...] = jnp.zeros_like(m_s)

    x = x_ref[...]
    rows = x.shape[0]
    h_in = _norm_mod(x, gpre_ref[...], _group_mod(mod_ref, 1), _group_mod(mod_ref, 0)).astype(BF16)
    proj_q = _dot_nt(wt_ref[0:D_HQ], h_in)
    proj_t = _dot_nt(wt_ref[2 * D_HQ:], h_in)
    k_all = _dot_nt(h_in, wt_ref[D_HQ:2 * D_HQ])
    r_v, r_o, r_i, r_f = 0, D_HV, 2 * D_HV, 2 * D_HV + N_HEADS

    s_idx = lax.broadcasted_iota(jnp.int32, (chunk, chunk), 0)
    t_idx = lax.broadcasted_iota(jnp.int32, (chunk, chunk), 1)
    causal = s_idx <= t_idx
    tri = jnp.where(causal, 1.0, 0.0).astype(BF16)
    lane = lax.broadcasted_iota(jnp.int32, (1, LANES), 1)
    low = lane < D_QK
    ones_rows = jnp.ones((BF16_ROWS, 2 * chunk), BF16)
    zero_rows = jnp.zeros((LANES - N_HEADS, chunk), F32)

    ig_all = GATE_SOFTCAP * jnp.tanh((proj_t[r_i:r_i + N_HEADS, :] + bif_ref[0:N_HEADS, :]) / GATE_SOFTCAP)
    lf_all = _log_sigmoid(proj_t[r_f:r_f + N_HEADS, :] + bif_ref[N_HEADS:, :])

    for c in range(rows // chunk):
        cs = slice(c * chunk, (c + 1) * chunk)
        lf = lf_all[:, cs]
        lf_hi = lf.astype(BF16).astype(F32)
        lf_mid = (lf - lf_hi).astype(BF16).astype(F32)
        lf_lo = lf - lf_hi - lf_mid
        b3 = _dot(jnp.concatenate([lf_hi, lf_mid, lf_lo, jnp.zeros_like(lf)], axis=0).astype(BF16), tri)
        b = b3[0:N_HEADS] + b3[N_HEADS:2 * N_HEADS] + b3[2 * N_HEADS:3 * N_HEADS]
        col = ig_all[:, cs] - b
        b_last = jnp.broadcast_to(b[:, chunk - 1:chunk], b.shape)
        col_s = jnp.concatenate([col, zero_rows], axis=0).T

        for p in range(N_HEADS // 2):
            h0, h1 = 2 * p, 2 * p + 1
            qt = [(proj_q[h * D_QK:(h + 1) * D_QK, cs] * (D_QK ** -0.5)).astype(BF16) for h in (h0, h1)]
            vt = [proj_t[r_v + h * D_V:r_v + (h + 1) * D_V, cs] for h in (h0, h1)]
            k_pair = k_all[cs, p * LANES:(p + 1) * LANES]
            state = st_s[p]
            lhs1 = jnp.concatenate([k_pair.astype(BF16), state.astype(BF16)], axis=0)
            r1 = _dot(lhs1, _block_diag(qt[0], qt[1]))
            probs, w_state, inv_floor, w_k, w_decay = [], [], [], [], []
            for i, h in enumerate((h0, h1)):
                m_prev = m_s[h:h + 1, :]
                col_m = jnp.where(causal, col_s[:, h:h + 1], -jnp.inf)
                g = jnp.maximum(m_prev, jnp.max(col_m, axis=0, keepdims=True))
                probs.append((jnp.exp(col_m - g) * r1[:chunk, i * chunk:(i + 1) * chunk]).astype(BF16))
                m_t = b[h:h + 1, :] + g
                m_new = jnp.broadcast_to(m_t[:, chunk - 1:chunk], m_t.shape)
                w_state.append(jnp.exp(m_prev - g))
                inv_floor.append(jnp.exp(-m_t))
                w_decay.append(jnp.exp(b_last[h:h + 1, :] + m_prev - m_new))
                w_k.append(jnp.exp(b_last[h:h + 1, :] - m_new + col[h:h + 1, :]))
                m_s[h:h + 1, :] = m_new
            lhs2 = jnp.concatenate([jnp.concatenate([vt[0], vt[1]], axis=1).astype(BF16), ones_rows], axis=0)
            r2 = _dot(lhs2, _block_diag(probs[0], probs[1]))
            for i, h in enumerate((h0, h1)):
                ls = slice(i * chunk, (i + 1) * chunk)
                den = w_state[i] * r1[chunk + D_V:chunk + D_V + 1, ls] + r2[D_V:D_V + 1, ls]
                inv = 1.0 / jnp.maximum(jnp.abs(den), inv_floor[i])
                ht = (w_state[i] * r1[chunk:chunk + D_V, ls] + r2[:D_V, ls]) * inv
                rn = lax.rsqrt(jnp.mean(ht * ht, axis=0, keepdims=True) + EPS)
                og = proj_t[r_o + h * D_V:r_o + (h + 1) * D_V, cs]
                yt_s[h * D_V:(h + 1) * D_V, cs] = (_sigmoid(og) * (ht * rn * ght_ref[h])).astype(BF16)
            lhs3 = jnp.concatenate(
                [jnp.concatenate([vt[i] * w_k[i] for i in range(2)], axis=1),
                 jnp.concatenate([jnp.broadcast_to(w_k[i], (BF16_ROWS, chunk)) for i in range(2)], axis=1)],
                axis=0).astype(BF16)
            rhs3 = jnp.concatenate([jnp.where(low, k_pair, 0.0), jnp.where(low, 0.0, k_pair)], axis=0).astype(BF16)
            decay = jnp.where(low, w_decay[0], w_decay[1])
            st_s[p] = state * decay + _dot(lhs3, rhs3)

    y = _dot_tn(yt_s[...], wout_ref[...])
    o_ref[...] = x + y * _rms(y) * (gpost_ref[...] * (1.0 + _group_mod(mod_ref, 2)))

    @pl.when(t == pl.num_programs(1) - 1)
    def _():
        for p in range(N_HEADS // 2):
            state = st_s[p]
            c_out_ref[p * LANES:(p + 1) * LANES, :] = state[:D_V, :].T
            n_out_ref[:, p * LANES:(p + 1) * LANES] = state[D_V:D_V + 1, :]
        m_out_ref[...] = m_s[...]


def _mixer_mod_specs(layer, batch):
    gain = pl.BlockSpec((None, None, 1, D_MODEL), lambda b, t: (layer, 1, 0, 0))
    return [pl.BlockSpec((None, 3, batch, D_MODEL), lambda b, t: (layer, 1, 0, 0)), gain, gain]


def _mlstm_prompt(x, mod, g_pre, g_post, layer, w_t, b_if, g_head_t, w_out):
    batch, seq, _ = x.shape
    rows = MIX_ROWS
    return pl.pallas_call(
        functools.partial(_mlstm_kernel, chunk=ML_CHUNK),
        grid=(batch, seq // rows),
        in_specs=[
            pl.BlockSpec((None, rows, D_MODEL), lambda b, t: (b, t, 0)),
            *_mixer_mod_specs(layer, batch),
            _resident(w_t.shape),
            _resident(b_if.shape),
            _resident(g_head_t.shape),
            _resident(w_out.shape),
        ],
        out_specs=[
            pl.BlockSpec((None, rows, D_MODEL), lambda b, t: (b, t, 0)),
            pl.BlockSpec((None, D_HQ, D_V), lambda b, t: (b, 0, 0)),
            pl.BlockSpec((None, 1, D_HQ), lambda b, t: (b, 0, 0)),
            pl.BlockSpec((None, N_HEADS, LANES), lambda b, t: (b, 0, 0)),
        ],
        out_shape=[
            jax.ShapeDtypeStruct(x.shape, F32),
            jax.ShapeDtypeStruct((batch, D_HQ, D_V), F32),
            jax.ShapeDtypeStruct((batch, 1, D_HQ), F32),
            jax.ShapeDtypeStruct((batch, N_HEADS, LANES), F32),
        ],
        scratch_shapes=[
            pltpu.VMEM((N_HEADS // 2, ML_STATE_ROWS, LANES), F32),
            pltpu.VMEM((N_HEADS, LANES), F32),
            pltpu.VMEM((D_HV, rows), BF16),
        ],
        compiler_params=_params("arbitrary", "arbitrary"),
        name="mlstm_prompt",
    )(x, mod, g_pre, g_post, w_t, b_if, g_head_t, w_out)


def _pair_expand(cols):
    lane = lax.broadcasted_iota(jnp.int32, (cols[0].shape[0], LANES), 1)
    return jnp.concatenate(
        [jnp.where(lane < D_QK, cols[2 * p], cols[2 * p + 1]) for p in range(N_HEADS // 2)], axis=1)


def _row_to_head_tile(rows, b):
    return jnp.concatenate([rows[b:b + 1, h * D_V:(h + 1) * D_V] for h in range(N_HEADS)], axis=0)


def _mlstm_step_kernel(x_ref, mod_ref, gpre_ref, gpost_ref, wt_ref, bif_ref, ghead_ref, wout_ref,
                       c_ref, n_ref, m_ref,
                       o_ref, c_out_ref, n_out_ref, m_out_ref,
                       v_s, og_s, a_s, b_s, y_s, qt_s, kwt_s, wst_s):
    i = pl.program_id(0)
    blk = c_ref.shape[0]
    rows = x_ref.shape[0]

    @pl.when(i == 0)
    def _():
        x = x_ref[...]
        h_in = _norm_mod(x, gpre_ref[...], mod_ref[1], mod_ref[0]).astype(BF16)
        proj = _dot_nt(h_in, wt_ref[...])
        gates = proj[:, 2 * D_HQ + 2 * D_HV:] + bif_ref[...]
        ig = (GATE_SOFTCAP * jnp.tanh(gates / GATE_SOFTCAP))[:, :N_HEADS]
        lf = _log_sigmoid(gates)[:, N_HEADS:]
        m_prev = m_ref[...]
        st = lf + m_prev
        m_t = jnp.maximum(st, ig)
        w_i = jnp.exp(ig - m_t)
        w_s = jnp.exp(st - m_t)
        floor = jnp.exp(-m_t)
        m_out_ref[...] = m_t
        q = proj[:, :D_HQ] * (D_QK ** -0.5)
        k = proj[:, D_HQ:2 * D_HQ]
        n_prev = n_ref[...]
        lane = lax.broadcasted_iota(jnp.int32, (rows, LANES), 1)
        lo = lane < D_QK

        def head_sums(z):
            out = []
            for p in range(N_HEADS // 2):
                zp = z[:, p * LANES:(p + 1) * LANES]
                out.append(jnp.sum(jnp.where(lo, zp, 0.0), axis=-1, keepdims=True))
                out.append(jnp.sum(jnp.where(lo, 0.0, zp), axis=-1, keepdims=True))
            return out

        qk_h = head_sums(q * k)
        qn_h = head_sums(q * n_prev)
        ws_cols, wi_cols = [], []
        for hd in range(N_HEADS):
            ws_h = w_s[:, hd:hd + 1]
            wi_h = w_i[:, hd:hd + 1]
            s_h = qk_h[hd] * wi_h
            den = ws_h * qn_h[hd] + s_h
            inv = 1.0 / jnp.maximum(jnp.abs(den), floor[:, hd:hd + 1])
            a_s[:, hd * D_V:(hd + 1) * D_V] = jnp.broadcast_to(ws_h * inv, (rows, D_V))
            b_s[:, hd * D_V:(hd + 1) * D_V] = jnp.broadcast_to(s_h * inv, (rows, D_V))
            ws_cols.append(ws_h)
            wi_cols.append(wi_h)
        ws_x = _pair_expand(ws_cols)
        kw = k * _pair_expand(wi_cols)
        n_out_ref[...] = ws_x * n_prev + kw
        v_s[...] = proj[:, 2 * D_HQ:2 * D_HQ + D_HV]
        og_s[...] = proj[:, 2 * D_HQ + D_HV:2 * D_HQ + 2 * D_HV]
        for dst, src in ((qt_s, q), (kwt_s, kw), (wst_s, ws_x)):
            src_t = src.T
            for j in range(rows // blk):
                dst[j] = src_t[:, j * blk:(j + 1) * blk]

    block_rows = pl.ds(pl.multiple_of(i * blk, blk), blk)
    v_blk, og_blk, a_blk, b_blk = v_s[block_rows, :], og_s[block_rows, :], a_s[block_rows, :], b_s[block_rows, :]
    y_tiles = []
    for b in range(blk):
        v_i = _row_to_head_tile(v_blk, b)
        readout = []
        for hd in range(N_HEADS):
            rs = slice(hd * D_QK, (hd + 1) * D_QK)
            c_h = c_ref[b, hd]
            readout.append(jnp.sum(c_h * qt_s[i, rs, b:b + 1], axis=0, keepdims=True))
            c_out_ref[b, hd] = c_h * wst_s[i, rs, b:b + 1] + kwt_s[i, rs, b:b + 1] * v_i[hd:hd + 1, :]
        hh = _row_to_head_tile(a_blk, b) * jnp.concatenate(readout, axis=0) + _row_to_head_tile(b_blk, b) * v_i
        y_tiles.append(_sigmoid(_row_to_head_tile(og_blk, b)) * (hh * _rms(hh) * ghead_ref[...]))
    y_s[block_rows, :] = jnp.concatenate(
        [jnp.concatenate([y_tiles[b][hd:hd + 1, :] for b in range(blk)], axis=0) for hd in range(N_HEADS)], axis=1)

    @pl.when(i == pl.num_programs(0) - 1)
    def _():
        y = _dot(y_s[...].astype(BF16), wout_ref[...])
        o_ref[...] = x_ref[...] + y * _rms(y) * (gpost_ref[...] * (1.0 + mod_ref[2]))


def _mlstm_step(x, mod, g_pre, g_post, layer, w_t, b_if, g_head, w_out, c0, n0, m0):
    rows = x.shape[0]
    blk = SAMPLE_BLOCK
    nblk = rows // blk
    const = lambda shape: pl.BlockSpec(shape, lambda i: tuple(0 for _ in shape))
    gain = pl.BlockSpec((None, None, 1, D_MODEL), lambda i: (layer, 1, 0, 0))
    c_spec = pl.BlockSpec((blk, N_HEADS, D_QK, D_V), lambda i: (i, 0, 0, 0))
    n_flat = n0.reshape(rows, D_HQ)
    y, c_new, n_new, m_new = pl.pallas_call(
        _mlstm_step_kernel,
        grid=(nblk,),
        in_specs=[
            const(x.shape),
            pl.BlockSpec((None, 3, rows, D_MODEL), lambda i: (layer, 1, 0, 0)),
            gain,
            gain,
            _resident(w_t.shape),
            const(b_if.shape),
            const(g_head.shape),
            _resident(w_out.shape),
            c_spec,
            const(n_flat.shape),
            const(m0.shape),
        ],
        out_specs=[const(x.shape), c_spec, const(n_flat.shape), const(m0.shape)],
        out_shape=[jax.ShapeDtypeStruct(x.shape, F32), jax.ShapeDtypeStruct(c0.shape, F32),
                   jax.ShapeDtypeStruct(n_flat.shape, F32), jax.ShapeDtypeStruct(m0.shape, F32)],
        scratch_shapes=[pltpu.VMEM((rows, D_HV), F32)] * 5 + [pltpu.VMEM((nblk, D_HQ, blk), F32)] * 3,
        compiler_params=_params("arbitrary"),
        name="mlstm_step",
    )(x, mod, g_pre, g_post, w_t, b_if, g_head, w_out, c0, n_flat, m0)
    return y, c_new, n_new.reshape(rows, N_HEADS, D_QK), m_new


def _conv_kernel(x_ref, mod_ref, gpre_ref, gpost_ref, win_ref, cw_ref, wout_ref, o_ref, buf_out_ref, tail_s):
    t = pl.program_id(1)

    @pl.when(t == 0)
    def _():
        tail_s[...] = jnp.zeros_like(tail_s)

    x = x_ref[...]
    rows = x.shape[0]
    h_in = _norm_mod(x, gpre_ref[...], _group_mod(mod_ref, 1), _group_mod(mod_ref, 0)).astype(BF16)
    p = _dot(h_in, win_ref[...])
    bg = p[:, :D_MODEL]
    u = p[:, D_MODEL:2 * D_MODEL] * p[:, 2 * D_MODEL:]
    prev1 = tail_s[7:8, :]
    prev2 = tail_s[6:7, :]
    ridx = lax.broadcasted_iota(jnp.int32, u.shape, 0)
    u1 = jnp.where(ridx == 0, prev1, pltpu.roll(u, 1, 0))
    u2 = jnp.where(ridx == 0, prev2, jnp.where(ridx == 1, prev1, pltpu.roll(u, 2, 0)))
    conv = cw_ref[0:1, :] * u2 + cw_ref[1:2, :] * u1 + cw_ref[2:3, :] * u
    tail_s[...] = u[rows - 8:, :]
    y = _dot((bg * conv).astype(BF16), wout_ref[...])
    o_ref[...] = x + y * _rms(y) * (gpost_ref[...] * (1.0 + _group_mod(mod_ref, 2)))

    @pl.when(t == pl.num_programs(1) - 1)
    def _():
        buf_out_ref[...] = u[rows - (CONV_W - 1):, :]


def _conv_prompt(x, mod, g_pre, g_post, layer, w_in, conv_w, w_out):
    batch, seq, _ = x.shape
    rows = MIX_ROWS
    return pl.pallas_call(
        _conv_kernel,
        grid=(batch, seq // rows),
        in_specs=[
            pl.BlockSpec((None, rows, D_MODEL), lambda b, t: (b, t, 0)),
            *_mixer_mod_specs(layer, batch),
            _resident(w_in.shape),
            _resident(conv_w.shape),
            _resident(w_out.shape),
        ],
        out_specs=[
            pl.BlockSpec((None, rows, D_MODEL), lambda b, t: (b, t, 0)),
            pl.BlockSpec((None, CONV_W - 1, D_MODEL), lambda b, t: (b, 0, 0)),
        ],
        out_shape=[jax.ShapeDtypeStruct(x.shape, F32),
                   jax.ShapeDtypeStruct((batch, CONV_W - 1, D_MODEL), F32)],
        scratch_shapes=[pltpu.VMEM((8, D_MODEL), F32)],
        compiler_params=_params("arbitrary", "arbitrary"),
        name="conv_prompt",
    )(x, mod, g_pre, g_post, w_in, conv_w, w_out)


def _conv_sample_kernel(x_ref, mod_ref, gpre_ref, gpost_ref, win_ref, cw_ref, wout_ref, buf_ref,
                        o_ref, buf_out_ref):
    x = x_ref[...]
    h_in = _norm_mod(x, gpre_ref[...], mod_ref[1], mod_ref[0]).astype(BF16)
    p = _dot(h_in, win_ref[...])
    bg = p[:, :D_MODEL]
    u = p[:, D_MODEL:2 * D_MODEL] * p[:, 2 * D_MODEL:]
    conv = cw_ref[0:1, :] * buf_ref[0] + cw_ref[1:2, :] * buf_ref[1] + cw_ref[2:3, :] * u
    y = _dot((bg * conv).astype(BF16), wout_ref[...])
    o_ref[...] = x + (1.0 + mod_ref[2]) * (y * _rms(y) * gpost_ref[...])
    buf_out_ref[0] = buf_ref[1]
    buf_out_ref[1] = u


def _conv_sample(x, mod, g_pre, g_post, w_in, conv_w, w_out, buf):
    full = lambda shape: pl.BlockSpec(shape, lambda: tuple(0 for _ in shape))
    ins = (x, mod, g_pre.reshape(1, D_MODEL), g_post.reshape(1, D_MODEL), w_in, conv_w, w_out, buf)
    return pl.pallas_call(
        _conv_sample_kernel,
        in_specs=[full(z.shape) for z in ins],
        out_specs=[full(x.shape), full(buf.shape)],
        out_shape=[jax.ShapeDtypeStruct(x.shape, F32), jax.ShapeDtypeStruct(buf.shape, F32)],
        compiler_params=_params(),
        name="conv_sample",
    )(*ins)


def kernel(x_prompt, x_sample, c_prompt, c_sample, state_mlstm_C, state_mlstm_n, state_mlstm_m, state_conv,
           w_ada, b_ada, g_pre, g_post, ffn_wg, ffn_wu, ffn_wd,
           ml_w_in, ml_b_i, ml_b_f, ml_g_head, ml_w_out, cv_w_in, cv_conv_w, cv_w_out):
    depth = w_ada.shape[0]
    batch, seq, _ = x_prompt.shape
    n_sample = x_sample.shape[0]
    n_ml = ml_w_in.shape[0]
    assert x_sample.shape[1] == 1 and seq % ML_CHUNK == 0

    mod_p, mod_s = _ada(c_prompt, c_sample, w_ada, b_ada)
    gp = g_pre.reshape(depth, 3, 1, D_MODEL)
    gq = g_post.reshape(depth, 3, 1, D_MODEL)

    ml_t = ml_w_in.transpose(0, 2, 1).astype(BF16)
    ml_bif = jnp.concatenate([ml_b_i, ml_b_f], axis=-1)
    ml_bif_t = jnp.broadcast_to(ml_bif[:, :, None], (n_ml, 2 * N_HEADS, MIX_ROWS))
    ml_ghead_t = jnp.broadcast_to(ml_g_head[:, :, :, None], (n_ml, N_HEADS, D_V, LANES))
    ml_out = ml_w_out.astype(BF16)
    cv_in = cv_w_in.astype(BF16)
    cv_out = cv_w_out.astype(BF16)

    xp = x_prompt
    xs = x_sample.reshape(n_sample, D_MODEL)
    p_c, p_n, p_m, p_buf, s_c, s_n, s_m, s_buf = [], [], [], [], [], [], [], []
    for l in range(depth):
        ms = lambda s: mod_s[l, 3 * s:3 * s + 3]
        xp, xs = _ffn(xp, xs, mod_p, mod_s, gp, gq, ffn_wg, ffn_wu, ffn_wd, l, 0)
        j = l // 2
        if l % 2 == 0:
            xp, c_j, n_j, m_j = _mlstm_prompt(xp, mod_p, gp, gq, l, ml_t[j], ml_bif_t[j], ml_ghead_t[j], ml_out[j])
            p_c.append(c_j.reshape(batch, N_HEADS, D_QK, D_V))
            p_n.append(n_j.reshape(batch, N_HEADS, D_QK))
            p_m.append(m_j[:, :, 0])
            ys, c_j, n_j, m_j = _mlstm_step(xs, mod_s, gp, gq, l, ml_t[j], ml_bif[j][None], ml_g_head[j], ml_out[j],
                                            state_mlstm_C[j], state_mlstm_n[j], state_mlstm_m[j])
            xs = ys
            s_c.append(c_j)
            s_n.append(n_j)
            s_m.append(m_j)
        else:
            xp, buf_j = _conv_prompt(xp, mod_p, gp, gq, l, cv_in[j], cv_conv_w[j], cv_out[j])
            p_buf.append(buf_j)
            ys, buf_j = _conv_sample(xs, ms(1), g_pre[l, 1], g_post[l, 1], cv_in[j], cv_conv_w[j], cv_out[j],
                                     state_conv[j].transpose(1, 0, 2))
            xs = ys
            s_buf.append(buf_j.transpose(1, 0, 2))
        xp, xs = _ffn(xp, xs, mod_p, mod_s, gp, gq, ffn_wg, ffn_wu, ffn_wd, l, 1)

    return (xp, xs.reshape(n_sample, 1, D_MODEL),
            jnp.stack(p_c), jnp.stack(p_n), jnp.stack(p_m), jnp.stack(p_buf),
            jnp.stack(s_c), jnp.stack(s_n), jnp.stack(s_m), jnp.stack(s_buf))
```
